```python
import jax
import jax.numpy as jnp
from jax import lax
import numpy as np

D_MODEL = 1024
BATCH = 8
SEQ = 4096
DEPTH = 2

CTX_LEN = 256
GRID_W = 64
EPS = 1e-6
N_MOD = 6

N_HEADS = 8
N_KV_HEADS = 2
GQA_GROUP = N_HEADS // N_KV_HEADS
HEAD_DIM = 64
WINDOW = 128
ATTN_BLOCK = 128
ATTN_W = N_HEADS * HEAD_DIM
KV_W = N_KV_HEADS * HEAD_DIM
ROPE_BASE = 10000.0
ROPE_FREQS = HEAD_DIM // 4

POOL_SIZES = (2, 4, 8, 16)
POOL_GROUP_W = 64
POOL_W = len(POOL_SIZES) * POOL_GROUP_W

SGU_CHUNK = 128
SGU_GROUPS = 4
SGU_W = 256
SGU_GROUP_W = SGU_W // SGU_GROUPS

N_BRANCH = 3
Q_OFF = 0
K_OFF = Q_OFF + ATTN_W
V_OFF = K_OFF + KV_W
POOL_OFF = V_OFF + KV_W
SGU_U_OFF = POOL_OFF + POOL_W
SGU_V_OFF = SGU_U_OFF + SGU_W
GATE_OFF = SGU_V_OFF + SGU_W
IN_W = GATE_OFF + N_BRANCH * D_MODEL

PEER_HEADS = 8
PEER_KEYS = 128
PEER_EXPERTS = PEER_KEYS * PEER_KEYS
PEER_KEY_DIM = 256
PEER_HALF = PEER_KEY_DIM // 2
PEER_TOPK = 16
PEER_BLOCK = 128

kernel_name = "hybrid_dit_window_attn_pool_sgu_peer"


def rms_norm(x, g):
    xf = x.astype(jnp.float32)
    y = xf * lax.rsqrt(jnp.mean(xf * xf, axis=-1, keepdims=True) + EPS)
    return (y * g.astype(jnp.float32)).astype(x.dtype)


def adaln_params(cond, w_mod, b_mod):
    m = jax.nn.silu(cond) @ w_mod + b_mod
    return jnp.split(m[:, None, :], N_MOD, axis=-1)


def modulate(x, g, shift, scale):
    return rms_norm(x, g) * (1.0 + scale) + shift


def axial_rope(length):
    rows = length // GRID_W
    row = jnp.repeat(jnp.arange(rows, dtype=jnp.float32), GRID_W)
    col = jnp.tile(jnp.arange(GRID_W, dtype=jnp.float32), rows)
    inv = ROPE_BASE ** (-jnp.arange(ROPE_FREQS, dtype=jnp.float32) / ROPE_FREQS)
    ang = jnp.stack([row[:, None] * inv, col[:, None] * inv], axis=1)
    return jnp.cos(ang), jnp.sin(ang)


def apply_rope(x, cos, sin):
    b, l, h, _ = x.shape
    xs = x.astype(jnp.float32).reshape(b, l, h, 2, 2, ROPE_FREQS)
    x1, x2 = xs[..., 0, :], xs[..., 1, :]
    c, s = cos[None, :, None], sin[None, :, None]
    out = jnp.stack([x1 * c - x2 * s, x2 * c + x1 * s], axis=-2)
    return out.reshape(b, l, h, HEAD_DIM).astype(x.dtype)


def attention_latent(q, k, v, k_ctx, v_ctx, sink):
    b, l = q.shape[:2]
    n_blocks = l // ATTN_BLOCK
    span = ATTN_BLOCK + 2 * WINDOW
    pad = ((0, 0), (WINDOW, WINDOW), (0, 0), (0, 0))
    kp, vp = jnp.pad(k, pad), jnp.pad(v, pad)
    qb = q.reshape(b, n_blocks, ATTN_BLOCK, N_KV_HEADS, GQA_GROUP, HEAD_DIM)
    qb = jnp.moveaxis(qb, 1, 0)
    scale = HEAD_DIM ** -0.5
    qi = jnp.arange(ATTN_BLOCK)[:, None]
    kj = jnp.arange(span)[None, :]
    in_window = jnp.abs(kj - WINDOW - qi) <= WINDOW
    sink_logit = jnp.broadcast_to(
        sink.astype(jnp.float32).reshape(N_KV_HEADS, GQA_GROUP)[None, :, :, None, None],
        (b, N_KV_HEADS, GQA_GROUP, ATTN_BLOCK, 1))
    n_ctx = k_ctx.shape[1]

    def one_block(args):
        n, q_blk = args
        start = n * ATTN_BLOCK
        k_win = lax.dynamic_slice_in_dim(kp, start, span, axis=1)
        v_win = lax.dynamic_slice_in_dim(vp, start, span, axis=1)
        kpos = start - WINDOW + kj
        valid = in_window & (kpos >= 0) & (kpos < l)
        s_win = jnp.einsum("bqhgd,bkhd->bhgqk", q_blk, k_win, preferred_element_type=jnp.float32) * scale
        s_win = jnp.where(valid, s_win, -jnp.inf)
        s_ctx = jnp.einsum("bqhgd,bkhd->bhgqk", q_blk, k_ctx, preferred_element_type=jnp.float32) * scale
        p = jax.nn.softmax(jnp.concatenate([s_win, s_ctx, sink_logit], axis=-1), axis=-1)
        o = jnp.einsum("bhgqk,bkhd->bqhgd", p[..., :span].astype(v.dtype), v_win)
        o = o + jnp.einsum("bhgqk,bkhd->bqhgd", p[..., span:span + n_ctx].astype(v.dtype), v_ctx)
        return o

    out = lax.map(one_block, (jnp.arange(n_blocks), qb))
    return jnp.moveaxis(out, 0, 1).reshape(b, l, ATTN_W)


def attention_context(q, k, v, sink):
    b, n_ctx = q.shape[:2]
    qg = q.reshape(b, n_ctx, N_KV_HEADS, GQA_GROUP, HEAD_DIM)
    s = jnp.einsum("bqhgd,bkhd->bhgqk", qg, k, preferred_element_type=jnp.float32) * HEAD_DIM ** -0.5
    sink_logit = jnp.broadcast_to(
        sink.astype(jnp.float32).reshape(N_KV_HEADS, GQA_GROUP)[None, :, :, None, None],
        (b, N_KV_HEADS, GQA_GROUP, n_ctx, 1))
    p = jax.nn.softmax(jnp.concatenate([s, sink_logit], axis=-1), axis=-1)
    o = jnp.einsum("bhgqk,bkhd->bqhgd", p[..., :n_ctx].astype(v.dtype), v)
    return o.reshape(b, n_ctx, ATTN_W)


def pool_mixer(z, pool_w, pool_scale):
    b, l, _ = z.shape
    zf = z.astype(jnp.float32)
    cs = jnp.pad(jnp.cumsum(zf, axis=1), ((0, 0), (1, 0), (0, 0)))
    t = jnp.arange(l)
    groups = []
    for g, size in enumerate(POOL_SIZES):
        lo = jnp.clip(t - size // 2, 0, l)
        hi = jnp.clip(t + size // 2, 0, l)
        sl = slice(g * POOL_GROUP_W, (g + 1) * POOL_GROUP_W)
        csg = cs[..., sl]
        mean = (csg[:, hi] - csg[:, lo]) / (hi - lo).astype(jnp.float32)[None, :, None]
        groups.append(mean - zf[..., sl])
    d = jnp.stack(groups, axis=2).astype(z.dtype)
    y = jnp.einsum("blgc,gce->blge", d, pool_w).reshape(b, l, POOL_W)
    return y * pool_scale


def sgu_mixer(u, v, sgu_w, sgu_b):
    b, l, _ = u.shape
    n_chunks = l // SGU_CHUNK
    vf = v.astype(jnp.float32)
    vn = (vf * lax.rsqrt(jnp.mean(vf * vf, axis=-1, keepdims=True) + EPS)).astype(v.dtype)
    vc = vn.reshape(b, n_chunks, SGU_CHUNK, SGU_GROUPS, SGU_GROUP_W)
    mixed = jnp.einsum("hpq,bnqhc->bnphc", sgu_w, vc) + sgu_b.T[None, None, :, :, None]
    return u * mixed.reshape(b, l, SGU_W)


def merge_branches(proj, y_attn, pool_w, pool_scale, sgu_w, sgu_b, w_br_attn, w_br_pool, w_br_sgu, w_out):
    y_pool = pool_mixer(proj[..., POOL_OFF:SGU_U_OFF], pool_w, pool_scale)
    y_sgu = sgu_mixer(jax.nn.gelu(proj[..., SGU_U_OFF:SGU_V_OFF]),
                      jax.nn.gelu(proj[..., SGU_V_OFF:GATE_OFF]), sgu_w, sgu_b)
    gates = jax.nn.sigmoid(proj[..., GATE_OFF:].reshape(*proj.shape[:-1], N_BRANCH, D_MODEL))
    merged = (gates[..., 0, :] * (y_attn @ w_br_attn)
              + gates[..., 1, :] * (y_pool @ w_br_pool)
              + gates[..., 2, :] * (y_sgu @ w_br_sgu))
    return merged @ w_out


def peer_ffn(h, w_q, subkeys, u_tab, v_tab):
    t = h.shape[0]
    blocks = h.reshape(t // PEER_BLOCK, PEER_BLOCK, D_MODEL)

    def one_block(xb):
        q = (xb @ w_q).reshape(PEER_BLOCK, PEER_HEADS, 2, PEER_HALF)
        s = jnp.einsum("thpd,hpkd->thpk", q, subkeys, preferred_element_type=jnp.float32)
        sv, si = lax.top_k(s, PEER_TOPK)
        n_cand = PEER_TOPK * PEER_TOPK
        cand_s = (sv[..., 0, :, None] + sv[..., 1, None, :]).reshape(PEER_BLOCK, PEER_HEADS, n_cand)
        cand_e = (si[..., 0, :, None] * PEER_KEYS + si[..., 1, None, :]).reshape(PEER_BLOCK, PEER_HEADS, n_cand)
        top_s, top_pos = lax.top_k(cand_s, PEER_TOPK)
        experts = jnp.take_along_axis(cand_e, top_pos, axis=-1)
        gate = jax.nn.softmax(top_s, axis=-1)
        u_sel = u_tab[experts]
        v_sel = v_tab[experts]
        act = jax.nn.gelu(jnp.einsum("td,thkd->thk", xb, u_sel, preferred_element_type=jnp.float32))
        return jnp.einsum("thk,thkd->td", (gate * act).astype(xb.dtype), v_sel)

    return lax.map(one_block, blocks).reshape(t, D_MODEL)


def setup_inputs(seed: int = 0) -> dict:
    key = jax.random.key(seed)
    ks = jax.random.split(key, 24)
    nrm = jax.random.normal
    f32 = jnp.float32
    L, D = DEPTH, D_MODEL
    return {
        "x": nrm(ks[0], (BATCH, SEQ, D), f32),
        "c": nrm(ks[1], (BATCH, D), f32),
        "ctx": nrm(ks[2], (BATCH, CTX_LEN, D), f32),
        "c_ctx": nrm(ks[3], (D,), f32),
        "w_mod": nrm(ks[4], (L, D, N_MOD * D), f32) * (0.5 * D ** -0.5),
        "b_mod": 0.02 * nrm(ks[5], (L, N_MOD * D), f32),
        "g_mix": 1.0 + 0.02 * nrm(ks[6], (L, D), f32),
        "g_ffn": 1.0 + 0.02 * nrm(ks[7], (L, D), f32),
        "w_in": nrm(ks[8], (L, D, IN_W), f32) * D ** -0.5,
        "attn_sink": 0.5 * nrm(ks[9], (L, N_HEADS), f32),
        "pool_w": nrm(ks[10], (L, len(POOL_SIZES), POOL_GROUP_W, POOL_GROUP_W), f32) * POOL_GROUP_W ** -0.5,
        "pool_scale": 1.0 + 0.02 * nrm(ks[11], (L, POOL_W), f32),
        "sgu_w": nrm(ks[12], (L, SGU_GROUPS, SGU_CHUNK, SGU_CHUNK), f32) * SGU_CHUNK ** -0.5,
        "sgu_b": 1.0 + 0.02 * nrm(ks[13], (L, SGU_GROUPS, SGU_CHUNK), f32),
        "w_br_attn": nrm(ks[14], (L, ATTN_W, D), f32) * ATTN_W ** -0.5,
        "w_br_pool": nrm(ks[15], (L, POOL_W, D), f32) * POOL_W ** -0.5,
        "w_br_sgu": nrm(ks[16], (L, SGU_W, D), f32) * SGU_W ** -0.5,
        "w_out": nrm(ks[17], (L, D, D), f32) * D ** -0.5,
        "peer_wq": nrm(ks[18], (L, D, PEER_HEADS * PEER_KEY_DIM), f32) * D ** -0.5,
        "peer_subkeys": nrm(ks[19], (L, PEER_HEADS, 2, PEER_KEYS, PEER_HALF), f32) * PEER_HALF ** -0.5,
        "peer_u": nrm(ks[20], (L, PEER_EXPERTS, D), f32) * D ** -0.5,
        "peer_v": 0.5 * nrm(ks[21], (L, PEER_EXPERTS, D), f32),
        "g_final": 1.0 + 0.02 * nrm(ks[22], (D,), f32),
    }


def reference(x, c, ctx, c_ctx, w_mod, b_mod, g_mix, g_ffn, w_in, attn_sink, pool_w, pool_scale,
              sgu_w, sgu_b, w_br_attn, w_br_pool, w_br_sgu, w_out, peer_wq, peer_subkeys,
              peer_u, peer_v, g_final):
    b, l, _ = x.shape
    n_ctx = ctx.shape[1]
    cos, sin = axial_rope(l)
    xc = ctx
    for layer in range(DEPTH):
        last = layer == DEPTH - 1
        sh1, sc1, gt1, sh2, sc2, gt2 = adaln_params(c, w_mod[layer], b_mod[layer])
        csh1, csc1, cgt1, csh2, csc2, cgt2 = adaln_params(c_ctx[None], w_mod[layer], b_mod[layer])

        h = modulate(x, g_mix[layer], sh1, sc1)
        hc = modulate(xc, g_mix[layer], csh1, csc1)
        proj = h @ w_in[layer]
        if last:
            kv_c = hc @ w_in[layer][:, K_OFF:POOL_OFF]
        else:
            proj_c = hc @ w_in[layer]
            kv_c = proj_c[..., K_OFF:POOL_OFF]
        k_c = kv_c[..., :KV_W].reshape(b, n_ctx, N_KV_HEADS, HEAD_DIM)
        v_c = kv_c[..., KV_W:].reshape(b, n_ctx, N_KV_HEADS, HEAD_DIM)

        q = apply_rope(proj[..., Q_OFF:K_OFF].reshape(b, l, N_HEADS, HEAD_DIM), cos, sin)
        k = apply_rope(proj[..., K_OFF:V_OFF].reshape(b, l, N_KV_HEADS, HEAD_DIM), cos, sin)
        v = proj[..., V_OFF:POOL_OFF].reshape(b, l, N_KV_HEADS, HEAD_DIM)
        y_attn = attention_latent(q, k, v, k_c, v_c, attn_sink[layer])
        x = x + gt1 * merge_branches(proj, y_attn, pool_w[layer], pool_scale[layer], sgu_w[layer],
                                     sgu_b[layer], w_br_attn[layer], w_br_pool[layer],
                                     w_br_sgu[layer], w_out[layer])
        if not last:
            q_c = proj_c[..., Q_OFF:K_OFF].reshape(b, n_ctx, N_HEADS, HEAD_DIM)
            y_attn_c = attention_context(q_c, k_c, v_c, attn_sink[layer])
            xc = xc + cgt1 * merge_branches(proj_c, y_attn_c, pool_w[layer], pool_scale[layer],
                                            sgu_w[layer], sgu_b[layer], w_br_attn[layer],
                                            w_br_pool[layer], w_br_sgu[layer], w_out[layer])

        h = modulate(x, g_ffn[layer], sh2, sc2)
        x = x + gt2 * peer_ffn(h.reshape(b * l, D_MODEL), peer_wq[layer], peer_subkeys[layer],
                               peer_u[layer], peer_v[layer]).reshape(b, l, D_MODEL)
        if not last:
            hc = modulate(xc, g_ffn[layer], csh2, csc2)
            xc = xc + cgt2 * peer_ffn(hc.reshape(b * n_ctx, D_MODEL), peer_wq[layer], peer_subkeys[layer],
                                      peer_u[layer], peer_v[layer]).reshape(b, n_ctx, D_MODEL)
    return rms_norm(x, g_final)
```

```python
import functools

import jax
import jax.numpy as jnp
from jax import lax
from jax.experimental import pallas as pl
from jax.experimental.pallas import tpu as pltpu

F32 = jnp.float32
BF16 = jnp.bfloat16

D_MODEL = 1024
EPS = 1e-6
N_MOD = 6
GRID_W = 64

N_HEADS = 8
N_KV_HEADS = 2
GQA_GROUP = N_HEADS // N_KV_HEADS
HEAD_DIM = 64
WINDOW = 128
ATTN_BLOCK = 128
ATTN_W = N_HEADS * HEAD_DIM
KV_W = N_KV_HEADS * HEAD_DIM
QKV_W = ATTN_W + 2 * KV_W
ROPE_BASE = 10000.0
ROPE_FREQS = HEAD_DIM // 4

POOL_SIZES = (2, 4, 8, 16)
POOL_GROUP_W = 64
POOL_W = len(POOL_SIZES) * POOL_GROUP_W
POOL_HALO = 8

SGU_CHUNK = 128
SGU_GROUPS = 4
SGU_W = 256
SGU_GROUP_W = SGU_W // SGU_GROUPS

N_BRANCH = 3
REST_W = POOL_W + 2 * SGU_W + N_BRANCH * D_MODEL
IN_W = QKV_W + REST_W
BR_W = ATTN_W + POOL_W + SGU_W

PEER_HEADS = 8
PEER_KEYS = 128
PEER_HALF = 128
PEER_TOPK = 16
PEER_SLOTS = PEER_HEADS * PEER_TOPK
PEER_QW = PEER_HEADS * 2 * PEER_HALF

LANES = 128
SUBLANES = 8
VMEM_LIMIT = 48 * 1024 * 1024

TOKEN_TILE = 256
PEER_TILE = 128
GATHER_SLOTS = 4
TOKEN_GROUP = 8


def _cparams(sem, **kw):
    return pltpu.CompilerParams(dimension_semantics=sem, vmem_limit_bytes=VMEM_LIMIT, **kw)


def _dot(a, b):
    return jnp.dot(a, b, preferred_element_type=F32)


def _dot_nt(a, b):
    return lax.dot_general(a, b, (((1,), (1,)), ((), ())), preferred_element_type=F32)


def _split_bf16(a):
    hi = a.astype(BF16)
    lo = (a - hi.astype(F32)).astype(BF16)
    return hi, lo


def _adaln_kernel(c_ref, w_ref, b_ref, o_ref):
    c = c_ref[...]
    s = c * (1.0 / (1.0 + jnp.exp(-c)))
    s_hi, s_lo = _split_bf16(s)
    w_hi, w_lo = _split_bf16(w_ref[...])
    acc = _dot(s_hi, w_hi) + (_dot(s_hi, w_lo) + _dot(s_lo, w_hi))
    o_ref[...] = acc + b_ref[...]


def _adaln(cond, w_mod, b_mod):
    depth = w_mod.shape[0]
    r = cond.shape[0]
    n = w_mod.shape[2]
    tn = D_MODEL
    return pl.pallas_call(
        _adaln_kernel,
        out_shape=jax.ShapeDtypeStruct((depth, r, n), F32),
        grid=(depth, n // tn),
        in_specs=[
            pl.BlockSpec((r, D_MODEL), lambda l, j: (0, 0)),
            pl.BlockSpec((None, D_MODEL, tn), lambda l, j: (l, 0, j)),
            pl.BlockSpec((None, 1, tn), lambda l, j: (l, 0, j)),
        ],
        out_specs=pl.BlockSpec((None, r, tn), lambda l, j: (l, 0, j)),
        compiler_params=_cparams(("arbitrary", "arbitrary")),
        name="adaln",
    )(cond, w_mod, b_mod.reshape(depth, 1, n))


def _modulated_norm(x, g, sc, sh):
    y = x * lax.rsqrt(jnp.mean(x * x, axis=-1, keepdims=True) + EPS)
    return (y * g) * (1.0 + sc) + sh


def _rope_tile(x, cos, sin_signed):
    lane = lax.broadcasted_iota(jnp.int32, x.shape, 1)
    first_half = (lane % (2 * ROPE_FREQS)) < ROPE_FREQS
    partner = jnp.where(first_half, pltpu.roll(x, LANES - ROPE_FREQS, 1), pltpu.roll(x, ROPE_FREQS, 1))
    return x * cos + partner * sin_signed


def _in_proj_kernel(x_ref, g_ref, sc_ref, sh_ref, w_ref, *rest, rope):
    if rope:
        cos_ref, sin_ref, qkv_ref, rest_ref = rest
    else:
        qkv_ref, rest_ref = rest
    hb = _modulated_norm(x_ref[...], g_ref[...], sc_ref[...], sh_ref[...]).astype(BF16)
    for c in range(QKV_W // LANES):
        cols = slice(c * LANES, (c + 1) * LANES)
        acc = _dot(hb, w_ref[:, cols])
        if rope and c * LANES < ATTN_W + KV_W:
            acc = _rope_tile(acc, cos_ref[...], sin_ref[...])
        qkv_ref[:, cols] = acc
    chunk = 768
    for c in range(REST_W // chunk):
        rest_ref[:, c * chunk:(c + 1) * chunk] = _dot(hb, w_ref[:, QKV_W + c * chunk:QKV_W + (c + 1) * chunk])


def _in_proj(x2d, g, sc, sh, w_bf16, rows_per_mod, rope_tables, seq_len):
    t = x2d.shape[0]
    tm = min(TOKEN_TILE, t)
    tiles_per_mod = rows_per_mod // tm
    rope = rope_tables is not None
    in_specs = [
        pl.BlockSpec((tm, D_MODEL), lambda i: (i, 0)),
        pl.BlockSpec((1, D_MODEL), lambda i: (0, 0)),
        pl.BlockSpec((None, 1, D_MODEL), lambda i: (i // tiles_per_mod, 0, 0)),
        pl.BlockSpec((None, 1, D_MODEL), lambda i: (i // tiles_per_mod, 0, 0)),
        pl.BlockSpec((D_MODEL, IN_W), lambda i: (0, 0)),
    ]
    args = [x2d, g, sc, sh, w_bf16]
    if rope:
        tiles_per_seq = seq_len // tm
        in_specs += [pl.BlockSpec((tm, LANES), lambda i: (i % tiles_per_seq, 0))] * 2
        args += list(rope_tables)
    return pl.pallas_call(
        functools.partial(_in_proj_kernel, rope=rope),
        out_shape=(jax.ShapeDtypeStruct((t, QKV_W), F32), jax.ShapeDtypeStruct((t, REST_W), F32)),
        grid=(t // tm,),
        in_specs=in_specs,
        out_specs=(pl.BlockSpec((tm, QKV_W), lambda i: (i, 0)), pl.BlockSpec((tm, REST_W), lambda i: (i, 0))),
        compiler_params=_cparams(("parallel",)),
        name="in_proj",
    )(*args)


def _peer_query_kernel(x_ref, g_ref, sc_ref, sh_ref, w_ref, q_ref, h_ref):
    h = _modulated_norm(x_ref[...], g_ref[...], sc_ref[...], sh_ref[...])
    h_ref[...] = h
    hb = h.astype(BF16)
    for hp in range(2 * PEER_HEADS):
        q_ref[hp] = _dot(hb, w_ref[:, hp * PEER_HALF:(hp + 1) * PEER_HALF])


def _peer_query(x2d, g, sc, sh, wq_bf16, rows_per_mod):
    t = x2d.shape[0]
    tm = min(TOKEN_TILE, t)
    tiles_per_mod = rows_per_mod // tm
    return pl.pallas_call(
        _peer_query_kernel,
        out_shape=(jax.ShapeDtypeStruct((2 * PEER_HEADS, t, PEER_HALF), F32),
                   jax.ShapeDtypeStruct((t, D_MODEL), F32)),
        grid=(t // tm,),
        in_specs=[
            pl.BlockSpec((tm, D_MODEL), lambda i: (i, 0)),
            pl.BlockSpec((1, D_MODEL), lambda i: (0, 0)),
            pl.BlockSpec((None, 1, D_MODEL), lambda i: (i // tiles_per_mod, 0, 0)),
            pl.BlockSpec((None, 1, D_MODEL), lambda i: (i // tiles_per_mod, 0, 0)),
            pl.BlockSpec((D_MODEL, PEER_QW), lambda i: (0, 0)),
        ],
        out_specs=(pl.BlockSpec((2 * PEER_HEADS, tm, PEER_HALF), lambda i: (0, i, 0)),
                   pl.BlockSpec((tm, D_MODEL), lambda i: (i, 0))),
        compiler_params=_cparams(("parallel",)),
        name="peer_query",
    )(x2d, g, sc, sh, wq_bf16)


def _attn_kernel(sink_ref, q_ref, *rest, window, n_blocks):
    if window:
        kvp_ref, kvc_ref, kvn_ref, ctx_ref, o_ref = rest
    else:
        ctx_ref, o_ref = rest
    n = pl.program_id(1)
    q = q_ref[...] * (HEAD_DIM ** -0.5)
    sources = []
    if window:
        qi = lax.broadcasted_iota(jnp.int32, (ATTN_BLOCK, ATTN_BLOCK), 0)
        kj = lax.broadcasted_iota(jnp.int32, (ATTN_BLOCK, ATTN_BLOCK), 1)
        sources.append((kvp_ref[...], (kj >= qi) & (n > 0)))
        sources.append((kvc_ref[...], None))
        sources.append((kvn_ref[...], (kj <= qi) & (n < n_blocks - 1)))
    sources.append((ctx_ref[...], None))
    kb = [[kv[:, g * HEAD_DIM:(g + 1) * HEAD_DIM].astype(BF16) for g in range(N_KV_HEADS)] for kv, _ in sources]
    vb = [[kv[:, KV_W + g * HEAD_DIM:KV_W + (g + 1) * HEAD_DIM].astype(BF16) for g in range(N_KV_HEADS)]
          for kv, _ in sources]
    outs = []
    for h in range(N_HEADS):
        g = h // GQA_GROUP
        qh = q[:, h * HEAD_DIM:(h + 1) * HEAD_DIM].astype(BF16)
        sink = sink_ref[h]
        scores = []
        m = jnp.full((ATTN_BLOCK, 1), sink, F32)
        for si, (_, valid) in enumerate(sources):
            s = _dot_nt(qh, kb[si][g])
            if valid is not None:
                s = jnp.where(valid, s, -jnp.inf)
            scores.append(s)
            m = jnp.maximum(m, jnp.max(s, axis=-1, keepdims=True))
        denom = jnp.exp(sink - m)
        acc = jnp.zeros((ATTN_BLOCK, HEAD_DIM), F32)
        for si, s in enumerate(scores):
            p = jnp.exp(s - m)
            denom = denom + jnp.sum(p, axis=-1, keepdims=True)
            acc = acc + _dot(p.astype(BF16), vb[si][g])
        outs.append(acc / denom)
    o_ref[...] = jnp.concatenate(outs, axis=-1)


def _attention(qkv, kv_ctx, sink, window):
    b, l, _ = qkv.shape
    n_ctx = kv_ctx.shape[1]
    nb = l // ATTN_BLOCK
    kv_col = ATTN_W // (2 * KV_W)
    in_specs = [
        pl.BlockSpec(memory_space=pltpu.SMEM),
        pl.BlockSpec((None, ATTN_BLOCK, ATTN_W), lambda bi, n: (bi, n, 0)),
    ]
    args = [sink, qkv]
    if window:
        in_specs += [
            pl.BlockSpec((None, ATTN_BLOCK, 2 * KV_W), lambda bi, n: (bi, jnp.maximum(n - 1, 0), kv_col)),
            pl.BlockSpec((None, ATTN_BLOCK, 2 * KV_W), lambda bi, n: (bi, n, kv_col)),
            pl.BlockSpec((None, ATTN_BLOCK, 2 * KV_W), lambda bi, n: (bi, jnp.minimum(n + 1, nb - 1), kv_col)),
        ]
        args += [qkv, qkv, qkv]
    in_specs.append(pl.BlockSpec((None, n_ctx, 2 * KV_W), lambda bi, n: (bi, 0, kv_col)))
    args.append(kv_ctx)
    return pl.pallas_call(
        functools.partial(_attn_kernel, window=window, n_blocks=nb),
        out_shape=jax.ShapeDtypeStruct((b, l, ATTN_W), F32),
        grid=(b, nb),
        in_specs=in_specs,
        out_specs=pl.BlockSpec((None, ATTN_BLOCK, ATTN_W), lambda bi, n: (bi, n, 0)),
        compiler_params=_cparams(("parallel", "parallel")),
        name="attention_window" if window else "attention_context",
    )(*args)


def _gelu(x):
    return jax.nn.gelu(x, approximate=True)


def _sigmoid(x):
    return 1.0 / (1.0 + jnp.exp(-x))


def _mixer_kernel(rest_ref, hprev_ref, hnext_ref, ya_ref, x_ref, gt_ref, poolw_ref, pscale_ref, sguw_ref,
                  sgub_ref, wbr_ref, wout_ref, o_ref, ext_ref, *, tm, tiles_per_seq, seq_len):
    ti = pl.program_id(0) % tiles_per_seq
    z = rest_ref[:, 0:POOL_W]
    ext_ref[0:POOL_HALO, :] = jnp.where(ti > 0, hprev_ref[...], 0.0)
    ext_ref[POOL_HALO:POOL_HALO + tm, :] = z
    ext_ref[POOL_HALO + tm:2 * POOL_HALO + tm, :] = jnp.where(ti < tiles_per_seq - 1, hnext_ref[...], 0.0)
    pos = ti * tm + lax.broadcasted_iota(jnp.int32, (tm, LANES), 0)
    lane = lax.broadcasted_iota(jnp.int32, (tm, LANES), 1)

    def count(size):
        hi = jnp.minimum(pos + size // 2, seq_len)
        lo = jnp.maximum(pos - size // 2, 0)
        return (hi - lo).astype(F32)

    diffs = []
    for lt in range(POOL_W // LANES):
        cols = slice(lt * LANES, (lt + 1) * LANES)

        def shifted(d, cols=cols):
            return ext_ref[POOL_HALO + d:POOL_HALO + d + tm, cols]

        small, large = POOL_SIZES[2 * lt], POOL_SIZES[2 * lt + 1]
        s = shifted(-1) + shifted(0)
        width = 2
        sums = {}
        while width <= large:
            sums[width] = s
            half = width // 2
            if 2 * width <= large:
                for d in range(half, width):
                    s = s + shifted(-d - 1) + shifted(d)
            width *= 2
        mean = jnp.where(lane < POOL_GROUP_W, sums[small] / count(small), sums[large] / count(large))
        diffs.append(mean - z[:, cols])
    d = jnp.concatenate(diffs, axis=-1).astype(BF16)
    y_pool = _dot(d, poolw_ref[...]) * pscale_ref[...]

    u = _gelu(rest_ref[:, POOL_W:POOL_W + SGU_W])
    v = _gelu(rest_ref[:, POOL_W + SGU_W:POOL_W + 2 * SGU_W])
    vn = (v * lax.rsqrt(jnp.mean(v * v, axis=-1, keepdims=True) + EPS)).astype(BF16)
    group = lax.broadcasted_iota(jnp.int32, (SGU_CHUNK, SGU_W), 1) // SGU_GROUP_W
    mixed = []
    for c in range(tm // SGU_CHUNK):
        vc = vn[c * SGU_CHUNK:(c + 1) * SGU_CHUNK, :]
        mc = sgub_ref[...]
        for hg in range(SGU_GROUPS):
            mc = mc + jnp.where(group == hg, _dot(sguw_ref[hg], vc), 0.0)
        mixed.append(mc)
    y_sgu = u * jnp.concatenate(mixed, axis=0)

    g_off = POOL_W + 2 * SGU_W
    merged = _sigmoid(rest_ref[:, g_off:g_off + D_MODEL]) * _dot(ya_ref[...].astype(BF16), wbr_ref[0:ATTN_W, :])
    merged = merged + _sigmoid(rest_ref[:, g_off + D_MODEL:g_off + 2 * D_MODEL]) * _dot(
        y_pool.astype(BF16), wbr_ref[ATTN_W:ATTN_W + POOL_W, :])
    merged = merged + _sigmoid(rest_ref[:, g_off + 2 * D_MODEL:g_off + 3 * D_MODEL]) * _dot(
        y_sgu.astype(BF16), wbr_ref[ATTN_W + POOL_W:BR_W, :])
    o_ref[...] = x_ref[...] + gt_ref[...] * _dot(merged.astype(BF16), wout_ref[...])


def _mixer(rest, y_attn, x2d, gt, poolw_bd, pscale, sguw, sgub_full, wbr, wout, rows_per_mod, seq_len):
    t = x2d.shape[0]
    tm = min(TOKEN_TILE, seq_len)
    tiles_per_seq = seq_len // tm
    tiles_per_mod = rows_per_mod // tm
    halo_blocks = tm // POOL_HALO
    n_halo = t // POOL_HALO
    kern = functools.partial(_mixer_kernel, tm=tm, tiles_per_seq=tiles_per_seq, seq_len=seq_len)
    return pl.pallas_call(
        kern,
        out_shape=jax.ShapeDtypeStruct((t, D_MODEL), F32),
        grid=(t // tm,),
        in_specs=[
            pl.BlockSpec((tm, REST_W), lambda i: (i, 0)),
            pl.BlockSpec((POOL_HALO, POOL_W), lambda i: (jnp.maximum(i * halo_blocks - 1, 0), 0)),
            pl.BlockSpec((POOL_HALO, POOL_W), lambda i: (jnp.minimum((i + 1) * halo_blocks, n_halo - 1), 0)),
            pl.BlockSpec((tm, ATTN_W), lambda i: (i, 0)),
            pl.BlockSpec((tm, D_MODEL), lambda i: (i, 0)),
            pl.BlockSpec((None, 1, D_MODEL), lambda i: (i // tiles_per_mod, 0, 0)),
            pl.BlockSpec((POOL_W, POOL_W), lambda i: (0, 0)),
            pl.BlockSpec((1, POOL_W), lambda i: (0, 0)),
            pl.BlockSpec((SGU_GROUPS, SGU_CHUNK, SGU_CHUNK), lambda i: (0, 0, 0)),
            pl.BlockSpec((SGU_CHUNK, SGU_W), lambda i: (0, 0)),
            pl.BlockSpec((BR_W, D_MODEL), lambda i: (0, 0)),
            pl.BlockSpec((D_MODEL, D_MODEL), lambda i: (0, 0)),
        ],
        out_specs=pl.BlockSpec((tm, D_MODEL), lambda i: (i, 0)),
        scratch_shapes=[pltpu.VMEM((tm + 2 * POOL_HALO, POOL_W), F32)],
        compiler_params=_cparams(("parallel",)),
        name="mixer",
    )(rest, rest, rest, y_attn, x2d, gt, poolw_bd, pscale, sguw, sgub_full, wbr, wout)


def _top16_rows(s, row_iota, n_rows, emit):
    for r in range(PEER_TOPK):
        m = jnp.max(s, axis=0, keepdims=True)
        idx = jnp.min(jnp.where(s == m, row_iota, n_rows), axis=0, keepdims=True)
        hit = row_iota == idx
        emit(r, m, hit, idx)
        s = jnp.where(hit, -jnp.inf, s)


def _peer_route_kernel(q_ref, sk_ref, e_ref, g_ref, sv_ref, si_ref, ts_ref, te_ref, *, tm):
    key_iota = lax.broadcasted_iota(jnp.int32, (PEER_KEYS, tm), 0)

    def sub_keys(hp, carry):
        s = _dot_nt(sk_ref[hp], q_ref[hp].astype(BF16))

        def emit(r, m, hit, idx):
            sv_ref[hp, r:r + 1, :] = m
            si_ref[hp, r:r + 1, :] = idx

        _top16_rows(s, key_iota, PEER_KEYS, emit)
        return carry

    lax.fori_loop(0, 2 * PEER_HEADS, sub_keys, 0)

    n_cand = PEER_TOPK * PEER_TOPK
    cand_iota = lax.broadcasted_iota(jnp.int32, (n_cand, tm), 0)

    def head(h, carry):
        sv0, sv1 = sv_ref[2 * h], sv_ref[2 * h + 1]
        si0, si1 = si_ref[2 * h], si_ref[2 * h + 1]
        cand_s = jnp.concatenate([sv0[a:a + 1, :] + sv1 for a in range(PEER_TOPK)], axis=0)
        cand_e = jnp.concatenate([si0[a:a + 1, :] * PEER_KEYS + si1 for a in range(PEER_TOPK)], axis=0)
        row0 = pl.multiple_of(h * PEER_TOPK, PEER_TOPK)

        def emit(r, m, hit, idx):
            ts_ref[r:r + 1, :] = m
            te_ref[r:r + 1, :] = jnp.max(jnp.where(hit, cand_e, -1), axis=0, keepdims=True)

        _top16_rows(cand_s, cand_iota, n_cand, emit)
        e_ref[pl.ds(row0, PEER_TOPK), :] = te_ref[...]
        ts = ts_ref[...]
        ex = jnp.exp(ts - jnp.max(ts, axis=0, keepdims=True))
        g_ref[pl.ds(row0, PEER_TOPK), :] = ex / jnp.sum(ex, axis=0, keepdims=True)
        return carry

    lax.fori_loop(0, PEER_HEADS, head, 0)


def _peer_route(q3, subkeys_bf16):
    t = q3.shape[1]
    tm = PEER_TILE
    nblk = t // tm
    return pl.pallas_call(
        functools.partial(_peer_route_kernel, tm=tm),
        out_shape=(jax.ShapeDtypeStruct((nblk, PEER_SLOTS, tm), jnp.int32),
                   jax.ShapeDtypeStruct((nblk, PEER_SLOTS, tm), F32)),
        grid=(nblk,),
        in_specs=[
            pl.BlockSpec((2 * PEER_HEADS, tm, PEER_HALF), lambda i: (0, i, 0)),
            pl.BlockSpec((2 * PEER_HEADS, PEER_KEYS, PEER_HALF), lambda i: (0, 0, 0)),
        ],
        out_specs=(pl.BlockSpec((None, PEER_SLOTS, tm), lambda i: (i, 0, 0)),
                   pl.BlockSpec((None, PEER_SLOTS, tm), lambda i: (i, 0, 0))),
        scratch_shapes=[
            pltpu.VMEM((2 * PEER_HEADS, PEER_TOPK, tm), F32),
            pltpu.VMEM((2 * PEER_HEADS, PEER_TOPK, tm), jnp.int32),
            pltpu.VMEM((PEER_TOPK, tm), F32),
            pltpu.VMEM((PEER_TOPK, tm), jnp.int32),
        ],
        compiler_params=_cparams(("parallel",)),
        name="peer_route",
    )(q3, subkeys_bf16)


def _peer_apply_kernel(idx_ref, gate_ref, h_ref, x_ref, gt_ref, uv_hbm, o_ref, buf_ref, sem_ref, *, tb):
    lookahead = GATHER_SLOTS - 1

    def row_copy(expert, slot, k):
        return pltpu.make_async_copy(uv_hbm.at[pl.ds(expert, 1), :], buf_ref.at[slot, pl.ds(k, 1), :],
                                     sem_ref.at[slot])

    def start_gather(tok, slot):
        for k in range(PEER_SLOTS):
            row_copy(idx_ref[tok, k], slot, k).start(priority=k % 2)

    def wait_gather(slot):
        pltpu.make_async_copy(uv_hbm.at[pl.ds(0, PEER_SLOTS), :], buf_ref.at[slot], sem_ref.at[slot]).wait()

    for tok in range(lookahead):
        start_gather(tok, tok % GATHER_SLOTS)

    tok_lane = lax.broadcasted_iota(jnp.int32, (PEER_SLOTS, tb), 1)

    def group(gi, carry):
        base = pl.multiple_of(gi * TOKEN_GROUP, TOKEN_GROUP)
        h8 = h_ref[pl.ds(base, TOKEN_GROUP), :]
        rows = []
        for j in range(TOKEN_GROUP):
            tok = base + j
            slot = j % GATHER_SLOTS

            @pl.when(tok + lookahead < tb)
            def _():
                start_gather(tok + lookahead, (j + lookahead) % GATHER_SLOTS)

            wait_gather(slot)
            act = jnp.sum(buf_ref[slot, :, 0:D_MODEL] * h8[j:j + 1, :], axis=1, keepdims=True)
            gate = jnp.sum(jnp.where(tok_lane == tok, gate_ref[...], 0.0), axis=1, keepdims=True)
            w = gate * _gelu(act)
            rows.append(jnp.sum(w * buf_ref[slot, :, D_MODEL:2 * D_MODEL], axis=0, keepdims=True))
        out8 = jnp.concatenate(rows, axis=0)
        o_ref[pl.ds(base, TOKEN_GROUP), :] = x_ref[pl.ds(base, TOKEN_GROUP), :] + gt_ref[...] * out8
        return carry

    lax.fori_loop(0, tb // TOKEN_GROUP, group, 0)


def _peer_apply(experts_tok, gates, h2d, x2d, gt, uv_table, rows_per_mod):
    t = x2d.shape[0]
    tb = PEER_TILE
    tiles_per_mod = rows_per_mod // tb
    return pl.pallas_call(
        functools.partial(_peer_apply_kernel, tb=tb),
        out_shape=jax.ShapeDtypeStruct((t, D_MODEL), F32),
        grid=(t // tb,),
        in_specs=[
            pl.BlockSpec((tb, PEER_SLOTS), lambda i: (i, 0), memory_space=pltpu.SMEM),
            pl.BlockSpec((None, PEER_SLOTS, tb), lambda i: (i, 0, 0)),
            pl.BlockSpec((tb, D_MODEL), lambda i: (i, 0)),
            pl.BlockSpec((tb, D_MODEL), lambda i: (i, 0)),
            pl.BlockSpec((None, 1, D_MODEL), lambda i: (i // tiles_per_mod, 0, 0)),
            pl.BlockSpec(memory_space=pl.ANY),
        ],
        out_specs=pl.BlockSpec((tb, D_MODEL), lambda i: (i, 0)),
        scratch_shapes=[
            pltpu.VMEM((GATHER_SLOTS, PEER_SLOTS, 2 * D_MODEL), F32),
            pltpu.SemaphoreType.DMA((GATHER_SLOTS,)),
        ],
        compiler_params=_cparams(("arbitrary",), disable_bounds_checks=True),
        name="peer_apply",
    )(experts_tok, gates, h2d, x2d, gt, uv_table)


def _final_norm_kernel(x_ref, g_ref, o_ref):
    x = x_ref[...]
    o_ref[...] = (x * lax.rsqrt(jnp.mean(x * x, axis=-1, keepdims=True) + EPS)) * g_ref[...]


def _final_norm(x2d, g):
    t = x2d.shape[0]
    tm = min(2 * TOKEN_TILE, t)
    return pl.pallas_call(
        _final_norm_kernel,
        out_shape=jax.ShapeDtypeStruct((t, D_MODEL), F32),
        grid=(t // tm,),
        in_specs=[pl.BlockSpec((tm, D_MODEL), lambda i: (i, 0)), pl.BlockSpec((1, D_MODEL), lambda i: (0, 0))],
        out_specs=pl.BlockSpec((tm, D_MODEL), lambda i: (i, 0)),
        compiler_params=_cparams(("parallel",)),
        name="final_norm",
    )(x2d, g)


def _rope_tables(length):
    rows = length // GRID_W
    row = jnp.repeat(jnp.arange(rows, dtype=F32), GRID_W)
    col = jnp.tile(jnp.arange(GRID_W, dtype=F32), rows)
    inv = ROPE_BASE ** (-jnp.arange(ROPE_FREQS, dtype=F32) / ROPE_FREQS)
    ang = jnp.stack([row[:, None] * inv, col[:, None] * inv], axis=1)
    cos, sin = jnp.cos(ang), jnp.sin(ang)
    cos_h = jnp.broadcast_to(cos[:, :, None, :], (length, 2, 2, ROPE_FREQS)).reshape(length, HEAD_DIM)
    sin_h = jnp.stack([-sin, sin], axis=2).reshape(length, HEAD_DIM)
    reps = LANES // HEAD_DIM
    return jnp.tile(cos_h, (1, reps)), jnp.tile(sin_h, (1, reps))


def _peer_layer(x2d, g, sc, sh, gt, rows_per_mod, wq, subkeys, uv_table):
    t = x2d.shape[0]
    q3, h2d = _peer_query(x2d, g, sc, sh, wq, rows_per_mod)
    experts, gates = _peer_route(q3, subkeys)
    experts_tok = jnp.transpose(experts, (0, 2, 1)).reshape(t, PEER_SLOTS)
    return _peer_apply(experts_tok, gates, h2d, x2d, gt, uv_table, rows_per_mod)


def kernel(x, c, ctx, c_ctx, w_mod, b_mod, g_mix, g_ffn, w_in, attn_sink, pool_w, pool_scale, sgu_w, sgu_b,
           w_br_attn, w_br_pool, w_br_sgu, w_out, peer_wq, peer_subkeys, peer_u, peer_v, g_final):
    b, l, d = x.shape
    n_ctx = ctx.shape[1]
    depth = w_mod.shape[0]
    assert d == D_MODEL and l % TOKEN_TILE == 0 and n_ctx % SGU_CHUNK == 0

    n_rows = -(-(b + 1) // SUBLANES) * SUBLANES
    cond = jnp.zeros((n_rows, d), F32).at[:b].set(c).at[b].set(c_ctx)
    mod = _adaln(cond, w_mod, b_mod).reshape(depth, n_rows, N_MOD, 1, d)

    rope = _rope_tables(l)
    x2d = x.reshape(b * l, d)
    xc2d = ctx.reshape(b * n_ctx, d)

    for layer in range(depth):
        last = layer == depth - 1
        m_lat = [mod[layer, :b, i] for i in range(N_MOD)]
        m_ctx = [mod[layer, b:b + 1, i] for i in range(N_MOD)]
        g1 = g_mix[layer].reshape(1, d)
        g2 = g_ffn[layer].reshape(1, d)

        w_in_b = w_in[layer].astype(BF16)
        qkv, rest = _in_proj(x2d, g1, m_lat[1], m_lat[0], w_in_b, l, rope, l)
        qkv_c, rest_c = _in_proj(xc2d, g1, m_ctx[1], m_ctx[0], w_in_b, b * n_ctx, None, n_ctx)

        y_attn = _attention(qkv.reshape(b, l, QKV_W), qkv_c.reshape(b, n_ctx, QKV_W), attn_sink[layer], True)

        poolw_bd = jnp.zeros((POOL_W, POOL_W), F32)
        for gi in range(len(POOL_SIZES)):
            sl = slice(gi * POOL_GROUP_W, (gi + 1) * POOL_GROUP_W)
            poolw_bd = poolw_bd.at[sl, sl].set(pool_w[layer, gi])
        poolw_bd = poolw_bd.astype(BF16)
        pscale = pool_scale[layer].reshape(1, POOL_W)
        sguw = sgu_w[layer].astype(BF16)
        sgub_full = jnp.repeat(sgu_b[layer].T, SGU_GROUP_W, axis=1)
        wbr = jnp.concatenate([w_br_attn[layer], w_br_pool[layer], w_br_sgu[layer]], axis=0).astype(BF16)
        wout = w_out[layer].astype(BF16)
        mix_w = (poolw_bd, pscale, sguw, sgub_full, wbr, wout)

        x2d = _mixer(rest, y_attn.reshape(b * l, ATTN_W), x2d, m_lat[2], *mix_w, l, l)
        if not last:
            y_attn_c = _attention(qkv_c.reshape(b, n_ctx, QKV_W), qkv_c.reshape(b, n_ctx, QKV_W),
                                  attn_sink[layer], False)
            xc2d = _mixer(rest_c, y_attn_c.reshape(b * n_ctx, ATTN_W), xc2d, m_ctx[2], *mix_w,
                          b * n_ctx, n_ctx)

        wq = peer_wq[layer].astype(BF16)
        subkeys = peer_subkeys[layer].reshape(2 * PEER_HEADS, PEER_KEYS, PEER_HALF).astype(BF16)
        uv_table = jnp.concatenate([peer_u[layer], peer_v[layer]], axis=1)
        x2d = _peer_layer(x2d, g2, m_lat[4], m_lat[3], m_lat[5], l, wq, subkeys, uv_table)
        if not last:
            xc2d = _peer_layer(xc2d, g2, m_ctx[4], m_ctx[3], m_ctx[5], b * n_ctx, wq, subkeys, uv_table)

    return _final_norm(x2d, g_final.reshape(1, d)).reshape(b, l, d)
```

```python
import functools

import jax
import jax.numpy as jnp
from jax import lax
from jax.experimental import pallas as pl
from jax.experimental.pallas import tpu as pltpu

F32 = jnp.float32
BF16 = jnp.bfloat16

D_MODEL = 1024
EPS = 1e-6
N_MOD = 6
GRID_W = 64

N_HEADS = 8
N_KV_HEADS = 2
GQA_GROUP = N_HEADS // N_KV_HEADS
HEAD_DIM = 64
WINDOW = 128
ATTN_BLOCK = 128
ATTN_W = N_HEADS * HEAD_DIM
KV_W = N_KV_HEADS * HEAD_DIM
QKV_W = ATTN_W + 2 * KV_W
ROPE_BASE = 10000.0
ROPE_FREQS = HEAD_DIM // 4

POOL_SIZES = (2, 4, 8, 16)
POOL_GROUP_W = 64
POOL_W = len(POOL_SIZES) * POOL_GROUP_W
POOL_HALO = 8

SGU_CHUNK = 128
SGU_GROUPS = 4
SGU_W = 256
SGU_GROUP_W = SGU_W // SGU_GROUPS

N_BRANCH = 3
REST_W = POOL_W + 2 * SGU_W + N_BRANCH * D_MODEL
IN_W = QKV_W + REST_W
BR_W = ATTN_W + POOL_W + SGU_W

PEER_HEADS = 8
PEER_KEYS = 128
PEER_HALF = 128
PEER_TOPK = 16
PEER_SLOTS = PEER_HEADS * PEER_TOPK
PEER_QW = PEER_HEADS * 2 * PEER_HALF

LANES = 128
SUBLANES = 8
VMEM_LIMIT = 48 * 1024 * 1024

TOKEN_TILE = 256
PEER_TILE = 128
GATHER_SLOTS = 4
TOKEN_GROUP = 8


def _cparams(sem, **kw):
    return pltpu.CompilerParams(dimension_semantics=sem, vmem_limit_bytes=VMEM_LIMIT, **kw)


def _dot(a, b):
    return jnp.dot(a, b, preferred_element_type=F32)


def _dot_nt(a, b):
    return lax.dot_general(a, b, (((1,), (1,)), ((), ())), preferred_element_type=F32)


def _split_bf16(a):
    hi = a.astype(BF16)
    lo = (a - hi.astype(F32)).astype(BF16)
    return hi, lo


def _adaln_kernel(c_ref, w_ref, b_ref, o_ref):
    c = c_ref[...]
    s = c * (1.0 / (1.0 + jnp.exp(-c)))
    s_hi, s_lo = _split_bf16(s)
    w_hi, w_lo = _split_bf16(w_ref[...])
    acc = _dot(s_hi, w_hi) + (_dot(s_hi, w_lo) + _dot(s_lo, w_hi))
    o_ref[...] = acc + b_ref[...]


def _adaln(cond, w_mod, b_mod):
    depth = w_mod.shape[0]
    r = cond.shape[0]
    n = w_mod.shape[2]
    tn = D_MODEL
    return pl.pallas_call(
        _adaln_kernel,
        out_shape=jax.ShapeDtypeStruct((depth, r, n), F32),
        grid=(depth, n // tn),
        in_specs=[
            pl.BlockSpec((r, D_MODEL), lambda l, j: (0, 0)),
            pl.BlockSpec((None, D_MODEL, tn), lambda l, j: (l, 0, j)),
            pl.BlockSpec((None, 1, tn), lambda l, j: (l, 0, j)),
        ],
        out_specs=pl.BlockSpec((None, r, tn), lambda l, j: (l, 0, j)),
        compiler_params=_cparams(("arbitrary", "arbitrary")),
        name="adaln",
    )(cond, w_mod, b_mod.reshape(depth, 1, n))


def _modulated_norm(x, g, sc, sh):
    y = x * lax.rsqrt(jnp.mean(x * x, axis=-1, keepdims=True) + EPS)
    return (y * g) * (1.0 + sc) + sh


def _rope_tile(x, cos, sin_signed):
    lane = lax.broadcasted_iota(jnp.int32, x.shape, 1)
    first_half = (lane % (2 * ROPE_FREQS)) < ROPE_FREQS
    partner = jnp.where(first_half, pltpu.roll(x, LANES - ROPE_FREQS, 1), pltpu.roll(x, ROPE_FREQS, 1))
    return x * cos + partner * sin_signed


def _in_proj_kernel(x_ref, g_ref, sc_ref, sh_ref, w_ref, *rest, rope):
    if rope:
        cos_ref, sin_ref, qkv_ref, rest_ref = rest
    else:
        qkv_ref, rest_ref = rest
    hb = _modulated_norm(x_ref[...], g_ref[...], sc_ref[...], sh_ref[...]).astype(BF16)
    for c in range(QKV_W // LANES):
        cols = slice(c * LANES, (c + 1) * LANES)
        acc = _dot(hb, w_ref[:, cols])
        if rope and c * LANES < ATTN_W + KV_W:
            acc = _rope_tile(acc, cos_ref[...], sin_ref[...])
        qkv_ref[:, cols] = acc
    chunk = 768
    for c in range(REST_W // chunk):
        rest_ref[:, c * chunk:(c + 1) * chunk] = _dot(hb, w_ref[:, QKV_W + c * chunk:QKV_W + (c + 1) * chunk])


def _in_proj(x2d, g, sc, sh, w_bf16, rows_per_mod, rope_tables, seq_len):
    t = x2d.shape[0]
    tm = min(TOKEN_TILE, t)
    tiles_per_mod = rows_per_mod // tm
    rope = rope_tables is not None
    in_specs = [
        pl.BlockSpec((tm, D_MODEL), lambda i: (i, 0)),
        pl.BlockSpec((1, D_MODEL), lambda i: (0, 0)),
        pl.BlockSpec((None, 1, D_MODEL), lambda i: (i // tiles_per_mod, 0, 0)),
        pl.BlockSpec((None, 1, D_MODEL), lambda i: (i // tiles_per_mod, 0, 0)),
        pl.BlockSpec((D_MODEL, IN_W), lambda i: (0, 0)),
    ]
    args = [x2d, g, sc, sh, w_bf16]
    if rope:
        tiles_per_seq = seq_len // tm
        in_specs += [pl.BlockSpec((tm, LANES), lambda i: (i % tiles_per_seq, 0))] * 2
        args += list(rope_tables)
    return pl.pallas_call(
        functools.partial(_in_proj_kernel, rope=rope),
        out_shape=(jax.ShapeDtypeStruct((t, QKV_W), F32), jax.ShapeDtypeStruct((t, REST_W), F32)),
        grid=(t // tm,),
        in_specs=in_specs,
        out_specs=(pl.BlockSpec((tm, QKV_W), lambda i: (i, 0)), pl.BlockSpec((tm, REST_W), lambda i: (i, 0))),
        compiler_params=_cparams(("parallel",)),
        name="in_proj",
    )(*args)


def _peer_query_kernel(x_ref, g_ref, sc_ref, sh_ref, w_ref, q_ref, h_ref):
    h = _modulated_norm(x_ref[...], g_ref[...], sc_ref[...], sh_ref[...])
    h_ref[...] = h
    hb = h.astype(BF16)
    for hp in range(2 * PEER_HEADS):
        q_ref[hp] = _dot(hb, w_ref[:, hp * PEER_HALF:(hp + 1) * PEER_HALF])


def _peer_query(x2d, g, sc, sh, wq_bf16, rows_per_mod):
    t = x2d.shape[0]
    tm = min(TOKEN_TILE, t)
    tiles_per_mod = rows_per_mod // tm
    return pl.pallas_call(
        _peer_query_kernel,
        out_shape=(jax.ShapeDtypeStruct((2 * PEER_HEADS, t, PEER_HALF), F32),
                   jax.ShapeDtypeStruct((t, D_MODEL), F32)),
        grid=(t // tm,),
        in_specs=[
            pl.BlockSpec((tm, D_MODEL), lambda i: (i, 0)),
            pl.BlockSpec((1, D_MODEL), lambda i: (0, 0)),
            pl.BlockSpec((None, 1, D_MODEL), lambda i: (i // tiles_per_mod, 0, 0)),
            pl.BlockSpec((None, 1, D_MODEL), lambda i: (i // tiles_per_mod, 0, 0)),
            pl.BlockSpec((D_MODEL, PEER_QW), lambda i: (0, 0)),
        ],
        out_specs=(pl.BlockSpec((2 * PEER_HEADS, tm, PEER_HALF), lambda i: (0, i, 0)),
                   pl.BlockSpec((tm, D_MODEL), lambda i: (i, 0))),
        compiler_params=_cparams(("parallel",)),
        name="peer_query",
    )(x2d, g, sc, sh, wq_bf16)


def _attn_kernel(sink_ref, q_ref, *rest, window, n_blocks):
    if window:
        kvp_ref, kvc_ref, kvn_ref, ctx_ref, o_ref = rest
    else:
        ctx_ref, o_ref = rest
    n = pl.program_id(1)
    q = q_ref[...] * (HEAD_DIM ** -0.5)
    sources = []
    if window:
        qi = lax.broadcasted_iota(jnp.int32, (ATTN_BLOCK, ATTN_BLOCK), 0)
        kj = lax.broadcasted_iota(jnp.int32, (ATTN_BLOCK, ATTN_BLOCK), 1)
        sources.append((kvp_ref[...], (kj >= qi) & (n > 0)))
        sources.append((kvc_ref[...], None))
        sources.append((kvn_ref[...], (kj <= qi) & (n < n_blocks - 1)))
    sources.append((ctx_ref[...], None))
    kb = [[kv[:, g * HEAD_DIM:(g + 1) * HEAD_DIM].astype(BF16) for g in range(N_KV_HEADS)] for kv, _ in sources]
    vb = [[kv[:, KV_W + g * HEAD_DIM:KV_W + (g + 1) * HEAD_DIM].astype(BF16) for g in range(N_KV_HEADS)]
          for kv, _ in sources]
    outs = []
    for h in range(N_HEADS):
        g = h // GQA_GROUP
        qh = q[:, h * HEAD_DIM:(h + 1) * HEAD_DIM].astype(BF16)
        sink = sink_ref[h]
        scores = []
        m = jnp.full((ATTN_BLOCK, 1), sink, F32)
        for si, (_, valid) in enumerate(sources):
            s = _dot_nt(qh, kb[si][g])
            if valid is not None:
                s = jnp.where(valid, s, -jnp.inf)
            scores.append(s)
            m = jnp.maximum(m, jnp.max(s, axis=-1, keepdims=True))
        denom = jnp.exp(sink - m)
        acc = jnp.zeros((ATTN_BLOCK, HEAD_DIM), F32)
        for si, s in enumerate(scores):
            p = jnp.exp(s - m)
            denom = denom + jnp.sum(p, axis=-1, keepdims=True)
            acc = acc + _dot(p.astype(BF16), vb[si][g])
        outs.append(acc / denom)
    o_ref[...] = jnp.concatenate(outs, axis=-1)


def _attention(qkv, kv_ctx, sink, window):
    b, l, _ = qkv.shape
    n_ctx = kv_ctx.shape[1]
    nb = l // ATTN_BLOCK
    kv_col = ATTN_W // (2 * KV_W)
    in_specs = [
        pl.BlockSpec(memory_space=pltpu.SMEM),
        pl.BlockSpec((None, ATTN_BLOCK, ATTN_W), lambda bi, n: (bi, n, 0)),
    ]
    args = [sink, qkv]
    if window:
        in_specs += [
            pl.BlockSpec((None, ATTN_BLOCK, 2 * KV_W), lambda bi, n: (bi, jnp.maximum(n - 1, 0), kv_col)),
            pl.BlockSpec((None, ATTN_BLOCK, 2 * KV_W), lambda bi, n: (bi, n, kv_col)),
            pl.BlockSpec((None, ATTN_BLOCK, 2 * KV_W), lambda bi, n: (bi, jnp.minimum(n + 1, nb - 1), kv_col)),
        ]
        args += [qkv, qkv, qkv]
    in_specs.append(pl.BlockSpec((None, n_ctx, 2 * KV_W), lambda bi, n: (bi, 0, kv_col)))
    args.append(kv_ctx)
    return pl.pallas_call(
        functools.partial(_attn_kernel, window=window, n_blocks=nb),
        out_shape=jax.ShapeDtypeStruct((b, l, ATTN_W), F32),
        grid=(b, nb),
        in_specs=in_specs,
        out_specs=pl.BlockSpec((None, ATTN_BLOCK, ATTN_W), lambda bi, n: (bi, n, 0)),
        compiler_params=_cparams(("parallel", "parallel")),
        name="attention_window" if window else "attention_context",
    )(*args)


def _gelu(x):
    return jax.nn.gelu(x, approximate=True)


def _sigmoid(x):
    return 1.0 / (1.0 + jnp.exp(-x))


def _mixer_kernel(rest_ref, hprev_ref, hnext_ref, ya_ref, x_ref, gt_ref, poolw_ref, pscale_ref, sguw_ref,
                  sgub_ref, wbr_ref, wout_ref, o_ref, ext_ref, *, tm, tiles_per_seq, seq_len):
    ti = pl.program_id(0) % tiles_per_seq
    z = rest_ref[:, 0:POOL_W]
    ext_ref[0:POOL_HALO, :] = jnp.where(ti > 0, hprev_ref[...], 0.0)
    ext_ref[POOL_HALO:POOL_HALO + tm, :] = z
    ext_ref[POOL_HALO + tm:2 * POOL_HALO + tm, :] = jnp.where(ti < tiles_per_seq - 1, hnext_ref[...], 0.0)
    pos = ti * tm + lax.broadcasted_iota(jnp.int32, (tm, LANES), 0)
    lane = lax.broadcasted_iota(jnp.int32, (tm, LANES), 1)

    def count(size):
        hi = jnp.minimum(pos + size // 2, seq_len)
        lo = jnp.maximum(pos - size // 2, 0)
        return (hi - lo).astype(F32)

    diffs = []
    for lt in range(POOL_W // LANES):
        cols = slice(lt * LANES, (lt + 1) * LANES)

        def shifted(d, cols=cols):
            return ext_ref[POOL_HALO + d:POOL_HALO + d + tm, cols]

        small, large = POOL_SIZES[2 * lt], POOL_SIZES[2 * lt + 1]
        s = shifted(-1) + shifted(0)
        width = 2
        sums = {}
        while width <= large:
            sums[width] = s
            half = width // 2
            if 2 * width <= large:
                for d in range(half, width):
                    s = s + shifted(-d - 1) + shifted(d)
            width *= 2
        mean = jnp.where(lane < POOL_GROUP_W, sums[small] / count(small), sums[large] / count(large))
        diffs.append(mean - z[:, cols])
    d = jnp.concatenate(diffs, axis=-1).astype(BF16)
    y_pool = _dot(d, poolw_ref[...]) * pscale_ref[...]

    u = _gelu(rest_ref[:, POOL_W:POOL_W + SGU_W])
    v = _gelu(rest_ref[:, POOL_W + SGU_W:POOL_W + 2 * SGU_W])
    vn = (v * lax.rsqrt(jnp.mean(v * v, axis=-1, keepdims=True) + EPS)).astype(BF16)
    group = lax.broadcasted_iota(jnp.int32, (SGU_CHUNK, SGU_W), 1) // SGU_GROUP_W
    mixed = []
    for c in range(tm // SGU_CHUNK):
        vc = vn[c * SGU_CHUNK:(c + 1) * SGU_CHUNK, :]
        mc = sgub_ref[...]
        for hg in range(SGU_GROUPS):
            mc = mc + jnp.where(group == hg, _dot(sguw_ref[hg], vc), 0.0)
        mixed.append(mc)
    y_sgu = u * jnp.concatenate(mixed, axis=0)

    g_off = POOL_W + 2 * SGU_W
    merged = _sigmoid(rest_ref[:, g_off:g_off + D_MODEL]) * _dot(ya_ref[...].astype(BF16), wbr_ref[0:ATTN_W, :])
    merged = merged + _sigmoid(rest_ref[:, g_off + D_MODEL:g_off + 2 * D_MODEL]) * _dot(
        y_pool.astype(BF16), wbr_ref[ATTN_W:ATTN_W + POOL_W, :])
    merged = merged + _sigmoid(rest_ref[:, g_off + 2 * D_MODEL:g_off + 3 * D_MODEL]) * _dot(
        y_sgu.astype(BF16), wbr_ref[ATTN_W + POOL_W:BR_W, :])
    o_ref[...] = x_ref[...] + gt_ref[...] * _dot(merged.astype(BF16), wout_ref[...])


def _mixer(rest, y_attn, x2d, gt, poolw_bd, pscale, sguw, sgub_full, wbr, wout, rows_per_mod, seq_len):
    t = x2d.shape[0]
    tm = min(TOKEN_TILE, seq_len)
    tiles_per_seq = seq_len // tm
    tiles_per_mod = rows_per_mod // tm
    halo_blocks = tm // POOL_HALO
    n_halo = t // POOL_HALO
    kern = functools.partial(_mixer_kernel, tm=tm, tiles_per_seq=tiles_per_seq, seq_len=seq_len)
    return pl.pallas_call(
        kern,
        out_shape=jax.ShapeDtypeStruct((t, D_MODEL), F32),
        grid=(t // tm,),
        in_specs=[
            pl.BlockSpec((tm, REST_W), lambda i: (i, 0)),
            pl.BlockSpec((POOL_HALO, POOL_W), lambda i: (jnp.maximum(i * halo_blocks - 1, 0), 0)),
            pl.BlockSpec((POOL_HALO, POOL_W), lambda i: (jnp.minimum((i + 1) * halo_blocks, n_halo - 1), 0)),
            pl.BlockSpec((tm, ATTN_W), lambda i: (i, 0)),
            pl.BlockSpec((tm, D_MODEL), lambda i: (i, 0)),
            pl.BlockSpec((None, 1, D_MODEL), lambda i: (i // tiles_per_mod, 0, 0)),
            pl.BlockSpec((POOL_W, POOL_W), lambda i: (0, 0)),
            pl.BlockSpec((1, POOL_W), lambda i: (0, 0)),
            pl.BlockSpec((SGU_GROUPS, SGU_CHUNK, SGU_CHUNK), lambda i: (0, 0, 0)),
            pl.BlockSpec((SGU_CHUNK, SGU_W), lambda i: (0, 0)),
            pl.BlockSpec((BR_W, D_MODEL), lambda i: (0, 0)),
            pl.BlockSpec((D_MODEL, D_MODEL), lambda i: (0, 0)),
        ],
        out_specs=pl.BlockSpec((tm, D_MODEL), lambda i: (i, 0)),
        scratch_shapes=[pltpu.VMEM((tm + 2 * POOL_HALO, POOL_W), F32)],
        compiler_params=_cparams(("parallel",)),
        name="mixer",
    )(rest, rest, rest, y_attn, x2d, gt, poolw_bd, pscale, sguw, sgub_full, wbr, wout)


def _top16_rows(s, row_iota, n_rows, emit):
    for r in range(PEER_TOPK):
        m = jnp.max(s, axis=0, keepdims=True)
        idx = jnp.min(jnp.where(s == m, row_iota, n_rows), axis=0, keepdims=True)
        hit = row_iota == idx
        emit(r, m, hit, idx)
        s = jnp.where(hit, -jnp.inf, s)


def _peer_route_kernel(q_ref, sk_ref, e_ref, g_ref, sv_ref, si_ref, ts_ref, te_ref, *, tm):
    key_iota = lax.broadcasted_iota(jnp.int32, (PEER_KEYS, tm), 0)

    def sub_keys(hp, carry):
        s = _dot_nt(sk_ref[hp], q_ref[hp].astype(BF16))

        def emit(r, m, hit, idx):
            sv_ref[hp, r:r + 1, :] = m
            si_ref[hp, r:r + 1, :] = idx

        _top16_rows(s, key_iota, PEER_KEYS, emit)
        return carry

    lax.fori_loop(0, 2 * PEER_HEADS, sub_keys, 0)

    n_cand = PEER_TOPK * PEER_TOPK
    cand_iota = lax.broadcasted_iota(jnp.int32, (n_cand, tm), 0)

    def head(h, carry):
        sv0, sv1 = sv_ref[2 * h], sv_ref[2 * h + 1]
        si0, si1 = si_ref[2 * h], si_ref[2 * h + 1]
        cand_s = jnp.concatenate([sv0[a:a + 1, :] + sv1 for a in range(PEER_TOPK)], axis=0)
        cand_e = jnp.concatenate([si0[a:a + 1, :] * PEER_KEYS + si1 for a in range(PEER_TOPK)], axis=0)
        row0 = pl.multiple_of(h * PEER_TOPK, PEER_TOPK)

        def emit(r, m, hit, idx):
            ts_ref[r:r + 1, :] = m
            te_ref[r:r + 1, :] = jnp.max(jnp.where(hit, cand_e, -1), axis=0, keepdims=True)

        _top16_rows(cand_s, cand_iota, n_cand, emit)
        e_ref[pl.ds(row0, PEER_TOPK), :] = te_ref[...]
        ts = ts_ref[...]
        ex = jnp.exp(ts - jnp.max(ts, axis=0, keepdims=True))
        g_ref[pl.ds(row0, PEER_TOPK), :] = ex / jnp.sum(ex, axis=0, keepdims=True)
        return carry

    lax.fori_loop(0, PEER_HEADS, head, 0)


def _peer_route(q3, subkeys_bf16):
    t = q3.shape[1]
    tm = PEER_TILE
    nblk = t // tm
    return pl.pallas_call(
        functools.partial(_peer_route_kernel, tm=tm),
        out_shape=(jax.ShapeDtypeStruct((nblk, PEER_SLOTS, tm), jnp.int32),
                   jax.ShapeDtypeStruct((nblk, PEER_SLOTS, tm), F32)),
        grid=(nblk,),
        in_specs=[
            pl.BlockSpec((2 * PEER_HEADS, tm, PEER_HALF), lambda i: (0, i, 0)),
            pl.BlockSpec((2 * PEER_HEADS, PEER_KEYS, PEER_HALF), lambda i: (0, 0, 0)),
        ],
        out_specs=(pl.BlockSpec((None, PEER_SLOTS, tm), lambda i: (i, 0, 0)),
                   pl.BlockSpec((None, PEER_SLOTS, tm), lambda i: (i, 0, 0))),
        scratch_shapes=[
            pltpu.VMEM((2 * PEER_HEADS, PEER_TOPK, tm), F32),
            pltpu.VMEM((2 * PEER_HEADS, PEER_TOPK, tm), jnp.int32),
            pltpu.VMEM((PEER_TOPK, tm), F32),
            pltpu.VMEM((PEER_TOPK, tm), jnp.int32),
        ],
        compiler_params=_cparams(("parallel",)),
        name="peer_route",
    )(q3, subkeys_bf16)


def _peer_apply_kernel(idx_ref, gate_ref, h_ref, x_ref, gt_ref, uv_hbm, o_ref, buf_ref, sem_ref, *, tb):
    lookahead = GATHER_SLOTS - 1
    n_groups = tb // TOKEN_GROUP

    def start_gather(tok, slot):
        for k in range(PEER_SLOTS):
            pltpu.make_async_copy(uv_hbm.at[idx_ref[tok, k]], buf_ref.at[slot, pl.ds(k, 1), :],
                                  sem_ref.at[slot]).start(priority=k % 2)

    def wait_gather(slot):
        pltpu.make_async_copy(uv_hbm.at[pl.ds(0, PEER_SLOTS), 0], buf_ref.at[slot], sem_ref.at[slot]).wait()

    for tok in range(lookahead):
        start_gather(tok, tok % GATHER_SLOTS)

    tok_lane = lax.broadcasted_iota(jnp.int32, (PEER_SLOTS, tb), 1)

    def group(base, is_last):
        h8 = h_ref[pl.ds(base, TOKEN_GROUP), :]
        rows = []
        for j in range(TOKEN_GROUP):
            tok = base + j
            slot = j % GATHER_SLOTS
            if not is_last or j + lookahead < TOKEN_GROUP:
                start_gather(tok + lookahead, (j + lookahead) % GATHER_SLOTS)
            wait_gather(slot)
            act = jnp.sum(buf_ref[slot, :, 0:D_MODEL] * h8[j:j + 1, :], axis=1, keepdims=True)
            gate = jnp.sum(jnp.where(tok_lane == tok, gate_ref[...], 0.0), axis=1, keepdims=True)
            w = gate * _gelu(act)
            rows.append(jnp.sum(w * buf_ref[slot, :, D_MODEL:2 * D_MODEL], axis=0, keepdims=True))
        out8 = jnp.concatenate(rows, axis=0)
        o_ref[pl.ds(base, TOKEN_GROUP), :] = x_ref[pl.ds(base, TOKEN_GROUP), :] + gt_ref[...] * out8

    def body(gi, carry):
        group(pl.multiple_of(gi * TOKEN_GROUP, TOKEN_GROUP), False)
        return carry

    lax.fori_loop(0, n_groups - 1, body, 0)
    group((n_groups - 1) * TOKEN_GROUP, True)


def _peer_apply(experts_tok, gates, h2d, x2d, gt, uv_table, rows_per_mod):
    t = x2d.shape[0]
    tb = PEER_TILE
    tiles_per_mod = rows_per_mod // tb
    return pl.pallas_call(
        functools.partial(_peer_apply_kernel, tb=tb),
        out_shape=jax.ShapeDtypeStruct((t, D_MODEL), F32),
        grid=(t // tb,),
        in_specs=[
            pl.BlockSpec((tb, PEER_SLOTS), lambda i: (i, 0), memory_space=pltpu.SMEM),
            pl.BlockSpec((None, PEER_SLOTS, tb), lambda i: (i, 0, 0)),
            pl.BlockSpec((tb, D_MODEL), lambda i: (i, 0)),
            pl.BlockSpec((tb, D_MODEL), lambda i: (i, 0)),
            pl.BlockSpec((None, 1, D_MODEL), lambda i: (i // tiles_per_mod, 0, 0)),
            pl.BlockSpec(memory_space=pl.ANY),
        ],
        out_specs=pl.BlockSpec((tb, D_MODEL), lambda i: (i, 0)),
        scratch_shapes=[
            pltpu.VMEM((GATHER_SLOTS, PEER_SLOTS, 2 * D_MODEL), F32),
            pltpu.SemaphoreType.DMA((GATHER_SLOTS,)),
        ],
        compiler_params=_cparams(("arbitrary",), disable_bounds_checks=True),
        name="peer_apply",
    )(experts_tok, gates, h2d, x2d, gt, uv_table)


def _final_norm_kernel(x_ref, g_ref, o_ref):
    x = x_ref[...]
    o_ref[...] = (x * lax.rsqrt(jnp.mean(x * x, axis=-1, keepdims=True) + EPS)) * g_ref[...]


def _final_norm(x2d, g):
    t = x2d.shape[0]
    tm = min(2 * TOKEN_TILE, t)
    return pl.pallas_call(
        _final_norm_kernel,
        out_shape=jax.ShapeDtypeStruct((t, D_MODEL), F32),
        grid=(t // tm,),
        in_specs=[pl.BlockSpec((tm, D_MODEL), lambda i: (i, 0)), pl.BlockSpec((1, D_MODEL), lambda i: (0, 0))],
        out_specs=pl.BlockSpec((tm, D_MODEL), lambda i: (i, 0)),
        compiler_params=_cparams(("parallel",)),
        name="final_norm",
    )(x2d, g)


def _rope_tables(length):
    rows = length // GRID_W
    row = jnp.repeat(jnp.arange(rows, dtype=F32), GRID_W)
    col = jnp.tile(jnp.arange(GRID_W, dtype=F32), rows)
    inv = ROPE_BASE ** (-jnp.arange(ROPE_FREQS, dtype=F32) / ROPE_FREQS)
    ang = jnp.stack([row[:, None] * inv, col[:, None] * inv], axis=1)
    cos, sin = jnp.cos(ang), jnp.sin(ang)
    cos_h = jnp.broadcast_to(cos[:, :, None, :], (length, 2, 2, ROPE_FREQS)).reshape(length, HEAD_DIM)
    sin_h = jnp.stack([-sin, sin], axis=2).reshape(length, HEAD_DIM)
    reps = LANES // HEAD_DIM
    return jnp.tile(cos_h, (1, reps)), jnp.tile(sin_h, (1, reps))


def _peer_layer(x2d, g, sc, sh, gt, rows_per_mod, wq, subkeys, uv_table):
    t = x2d.shape[0]
    q3, h2d = _peer_query(x2d, g, sc, sh, wq, rows_per_mod)
    experts, gates = _peer_route(q3, subkeys)
    experts_tok = jnp.transpose(experts, (0, 2, 1)).reshape(t, PEER_SLOTS)
    return _peer_apply(experts_tok, gates, h2d, x2d, gt, uv_table, rows_per_mod)


def kernel(x, c, ctx, c_ctx, w_mod, b_mod, g_mix, g_ffn, w_in, attn_sink, pool_w, pool_scale, sgu_w, sgu_b,
           w_br_attn, w_br_pool, w_br_sgu, w_out, peer_wq, peer_subkeys, peer_u, peer_v, g_final):
    b, l, d = x.shape
    n_ctx = ctx.shape[1]
    depth = w_mod.shape[0]
    assert d == D_MODEL and l % TOKEN_TILE == 0 and n_ctx % SGU_CHUNK == 0

    n_rows = -(-(b + 1) // SUBLANES) * SUBLANES
    cond = jnp.zeros((n_rows, d), F32).at[:b].set(c).at[b].set(c_ctx)
    mod = _adaln(cond, w_mod, b_mod).reshape(depth, n_rows, N_MOD, 1, d)

    rope = _rope_tables(l)
    x2d = x.reshape(b * l, d)
    xc2d = ctx.reshape(b * n_ctx, d)

    for layer in range(depth):
        last = layer == depth - 1
        m_lat = [mod[layer, :b, i] for i in range(N_MOD)]
        m_ctx = [mod[layer, b:b + 1, i] for i in range(N_MOD)]
        g1 = g_mix[layer].reshape(1, d)
        g2 = g_ffn[layer].reshape(1, d)

        w_in_b = w_in[layer].astype(BF16)
        qkv, rest = _in_proj(x2d, g1, m_lat[1], m_lat[0], w_in_b, l, rope, l)
        qkv_c, rest_c = _in_proj(xc2d, g1, m_ctx[1], m_ctx[0], w_in_b, b * n_ctx, None, n_ctx)

        y_attn = _attention(qkv.reshape(b, l, QKV_W), qkv_c.reshape(b, n_ctx, QKV_W), attn_sink[layer], True)

        poolw_bd = jnp.zeros((POOL_W, POOL_W), F32)
        for gi in range(len(POOL_SIZES)):
            sl = slice(gi * POOL_GROUP_W, (gi + 1) * POOL_GROUP_W)
            poolw_bd = poolw_bd.at[sl, sl].set(pool_w[layer, gi])
        poolw_bd = poolw_bd.astype(BF16)
        pscale = pool_scale[layer].reshape(1, POOL_W)
        sguw = sgu_w[layer].astype(BF16)
        sgub_full = jnp.repeat(sgu_b[layer].T, SGU_GROUP_W, axis=1)
        wbr = jnp.concatenate([w_br_attn[layer], w_br_pool[layer], w_br_sgu[layer]], axis=0).astype(BF16)
        wout = w_out[layer].astype(BF16)
        mix_w = (poolw_bd, pscale, sguw, sgub_full, wbr, wout)

        x2d = _mixer(rest, y_attn.reshape(b * l, ATTN_W), x2d, m_lat[2], *mix_w, l, l)
        if not last:
            y_attn_c = _attention(qkv_c.reshape(b, n_ctx, QKV_W), qkv_c.reshape(b, n_ctx, QKV_W),
                                  attn_sink[layer], False)
            xc2d = _mixer(rest_c, y_attn_c.reshape(b * n_ctx, ATTN_W), xc2d, m_ctx[2], *mix_w,
                          b * n_ctx, n_ctx)

        wq = peer_wq[layer].astype(BF16)
        subkeys = peer_subkeys[layer].reshape(2 * PEER_HEADS, PEER_KEYS, PEER_HALF).astype(BF16)
        uv_table = jnp.concatenate([peer_u[layer], peer_v[layer]], axis=1)[:, None, :]
        x2d = _peer_layer(x2d, g2, m_lat[4], m_lat[3], m_lat[5], l, wq, subkeys, uv_table)
        if not last:
            xc2d = _peer_layer(xc2d, g2, m_ctx[4], m_ctx[3], m_ctx[5], b * n_ctx, wq, subkeys, uv_table)

    return _final_norm(x2d, g_final.reshape(1, d)).reshape(b, l, d)
```

```python
import functools

import jax
import jax.numpy as jnp
from jax import lax
from jax.experimental import pallas as pl
from jax.experimental.pallas import tpu as pltpu

F32 = jnp.float32
BF16 = jnp.bfloat16

D_MODEL = 1024
EPS = 1e-6
N_MOD = 6
GRID_W = 64

N_HEADS = 8
N_KV_HEADS = 2
GQA_GROUP = N_HEADS // N_KV_HEADS
HEAD_DIM = 64
WINDOW = 128
ATTN_BLOCK = 128
ATTN_W = N_HEADS * HEAD_DIM
KV_W = N_KV_HEADS * HEAD_DIM
QKV_W = ATTN_W + 2 * KV_W
ROPE_BASE = 10000.0
ROPE_FREQS = HEAD_DIM // 4

POOL_SIZES = (2, 4, 8, 16)
POOL_GROUP_W = 64
POOL_W = len(POOL_SIZES) * POOL_GROUP_W
POOL_HALO = 8

SGU_CHUNK = 128
SGU_GROUPS = 4
SGU_W = 256
SGU_GROUP_W = SGU_W // SGU_GROUPS

N_BRANCH = 3
REST_W = POOL_W + 2 * SGU_W + N_BRANCH * D_MODEL
IN_W = QKV_W + REST_W
BR_W = ATTN_W + POOL_W + SGU_W

PEER_HEADS = 8
PEER_KEYS = 128
PEER_HALF = 128
PEER_TOPK = 16
PEER_SLOTS = PEER_HEADS * PEER_TOPK
PEER_QW = PEER_HEADS * 2 * PEER_HALF

LANES = 128
SUBLANES = 8
VMEM_LIMIT = 48 * 1024 * 1024

TOKEN_TILE = 256
PEER_TILE = 128
GATHER_SLOTS = 4
TOKEN_GROUP = 8


def _cparams(sem, **kw):
    return pltpu.CompilerParams(dimension_semantics=sem, vmem_limit_bytes=VMEM_LIMIT, **kw)


def _dot(a, b):
    return jnp.dot(a, b, preferred_element_type=F32)


def _dot_nt(a, b):
    return lax.dot_general(a, b, (((1,), (1,)), ((), ())), preferred_element_type=F32)


def _split_bf16(a):
    hi = a.astype(BF16)
    lo = (a - hi.astype(F32)).astype(BF16)
    return hi, lo


def _adaln_kernel(c_ref, w_ref, b_ref, o_ref):
    c = c_ref[...]
    s = c * (1.0 / (1.0 + jnp.exp(-c)))
    s_hi, s_lo = _split_bf16(s)
    w_hi, w_lo = _split_bf16(w_ref[...])
    acc = _dot(s_hi, w_hi) + (_dot(s_hi, w_lo) + _dot(s_lo, w_hi))
    o_ref[...] = acc + b_ref[...]


def _adaln(cond, w_mod, b_mod):
    depth = w_mod.shape[0]
    r = cond.shape[0]
    n = w_mod.shape[2]
    tn = D_MODEL
    return pl.pallas_call(
        _adaln_kernel,
        out_shape=jax.ShapeDtypeStruct((depth, r, n), F32),
        grid=(depth, n // tn),
        in_specs=[
            pl.BlockSpec((r, D_MODEL), lambda l, j: (0, 0)),
            pl.BlockSpec((None, D_MODEL, tn), lambda l, j: (l, 0, j)),
            pl.BlockSpec((None, 1, tn), lambda l, j: (l, 0, j)),
        ],
        out_specs=pl.BlockSpec((None, r, tn), lambda l, j: (l, 0, j)),
        compiler_params=_cparams(("arbitrary", "arbitrary")),
        name="adaln",
    )(cond, w_mod, b_mod.reshape(depth, 1, n))


def _modulated_norm(x, g, sc, sh):
    y = x * lax.rsqrt(jnp.mean(x * x, axis=-1, keepdims=True) + EPS)
    return (y * g) * (1.0 + sc) + sh


def _rope_tile(x, cos, sin_signed):
    lane = lax.broadcasted_iota(jnp.int32, x.shape, 1)
    first_half = (lane % (2 * ROPE_FREQS)) < ROPE_FREQS
    partner = jnp.where(first_half, pltpu.roll(x, LANES - ROPE_FREQS, 1), pltpu.roll(x, ROPE_FREQS, 1))
    return x * cos + partner * sin_signed


def _in_proj_kernel(x_ref, g_ref, sc_ref, sh_ref, w_ref, *rest, rope):
    if rope:
        cos_ref, sin_ref, qkv_ref, rest_ref = rest
    else:
        qkv_ref, rest_ref = rest
    hb = _modulated_norm(x_ref[...], g_ref[...], sc_ref[...], sh_ref[...]).astype(BF16)
    for c in range(QKV_W // LANES):
        cols = slice(c * LANES, (c + 1) * LANES)
        acc = _dot(hb, w_ref[:, cols])
        if rope and c * LANES < ATTN_W + KV_W:
            acc = _rope_tile(acc, cos_ref[...], sin_ref[...])
        qkv_ref[:, cols] = acc
    chunk = 768
    for c in range(REST_W // chunk):
        rest_ref[:, c * chunk:(c + 1) * chunk] = _dot(hb, w_ref[:, QKV_W + c * chunk:QKV_W + (c + 1) * chunk])


def _in_proj(x2d, g, sc, sh, w_bf16, rows_per_mod, rope_tables, seq_len):
    t = x2d.shape[0]
    tm = min(TOKEN_TILE, t)
    tiles_per_mod = rows_per_mod // tm
    rope = rope_tables is not None
    in_specs = [
        pl.BlockSpec((tm, D_MODEL), lambda i: (i, 0)),
        pl.BlockSpec((1, D_MODEL), lambda i: (0, 0)),
        pl.BlockSpec((None, 1, D_MODEL), lambda i: (i // tiles_per_mod, 0, 0)),
        pl.BlockSpec((None, 1, D_MODEL), lambda i: (i // tiles_per_mod, 0, 0)),
        pl.BlockSpec((D_MODEL, IN_W), lambda i: (0, 0)),
    ]
    args = [x2d, g, sc, sh, w_bf16]
    if rope:
        tiles_per_seq = seq_len // tm
        in_specs += [pl.BlockSpec((tm, LANES), lambda i: (i % tiles_per_seq, 0))] * 2
        args += list(rope_tables)
    return pl.pallas_call(
        functools.partial(_in_proj_kernel, rope=rope),
        out_shape=(jax.ShapeDtypeStruct((t, QKV_W), F32), jax.ShapeDtypeStruct((t, REST_W), F32)),
        grid=(t // tm,),
        in_specs=in_specs,
        out_specs=(pl.BlockSpec((tm, QKV_W), lambda i: (i, 0)), pl.BlockSpec((tm, REST_W), lambda i: (i, 0))),
        compiler_params=_cparams(("parallel",)),
        name="in_proj",
    )(*args)


def _peer_query_kernel(x_ref, g_ref, sc_ref, sh_ref, w_ref, q_ref, h_ref):
    h = _modulated_norm(x_ref[...], g_ref[...], sc_ref[...], sh_ref[...])
    h_ref[...] = h
    hb = h.astype(BF16)
    for hp in range(2 * PEER_HEADS):
        q_ref[hp] = _dot(hb, w_ref[:, hp * PEER_HALF:(hp + 1) * PEER_HALF])


def _peer_query(x2d, g, sc, sh, wq_bf16, rows_per_mod):
    t = x2d.shape[0]
    tm = min(TOKEN_TILE, t)
    tiles_per_mod = rows_per_mod // tm
    return pl.pallas_call(
        _peer_query_kernel,
        out_shape=(jax.ShapeDtypeStruct((2 * PEER_HEADS, t, PEER_HALF), F32),
                   jax.ShapeDtypeStruct((t, D_MODEL), F32)),
        grid=(t // tm,),
        in_specs=[
            pl.BlockSpec((tm, D_MODEL), lambda i: (i, 0)),
            pl.BlockSpec((1, D_MODEL), lambda i: (0, 0)),
            pl.BlockSpec((None, 1, D_MODEL), lambda i: (i // tiles_per_mod, 0, 0)),
            pl.BlockSpec((None, 1, D_MODEL), lambda i: (i // tiles_per_mod, 0, 0)),
            pl.BlockSpec((D_MODEL, PEER_QW), lambda i: (0, 0)),
        ],
        out_specs=(pl.BlockSpec((2 * PEER_HEADS, tm, PEER_HALF), lambda i: (0, i, 0)),
                   pl.BlockSpec((tm, D_MODEL), lambda i: (i, 0))),
        compiler_params=_cparams(("parallel",)),
        name="peer_query",
    )(x2d, g, sc, sh, wq_bf16)


def _attn_kernel(sink_ref, q_ref, *rest, window, n_blocks):
    if window:
        kvp_ref, kvc_ref, kvn_ref, ctx_ref, o_ref = rest
    else:
        ctx_ref, o_ref = rest
    n = pl.program_id(1)
    q = q_ref[...] * (HEAD_DIM ** -0.5)
    sources = []
    if window:
        qi = lax.broadcasted_iota(jnp.int32, (ATTN_BLOCK, ATTN_BLOCK), 0)
        kj = lax.broadcasted_iota(jnp.int32, (ATTN_BLOCK, ATTN_BLOCK), 1)
        sources.append((kvp_ref[...], (kj >= qi) & (n > 0)))
        sources.append((kvc_ref[...], None))
        sources.append((kvn_ref[...], (kj <= qi) & (n < n_blocks - 1)))
    sources.append((ctx_ref[...], None))
    kb = [[kv[:, g * HEAD_DIM:(g + 1) * HEAD_DIM].astype(BF16) for g in range(N_KV_HEADS)] for kv, _ in sources]
    vb = [[kv[:, KV_W + g * HEAD_DIM:KV_W + (g + 1) * HEAD_DIM].astype(BF16) for g in range(N_KV_HEADS)]
          for kv, _ in sources]
    outs = []
    for h in range(N_HEADS):
        g = h // GQA_GROUP
        qh = q[:, h * HEAD_DIM:(h + 1) * HEAD_DIM].astype(BF16)
        sink = sink_ref[h]
        scores = []
        m = jnp.full((ATTN_BLOCK, 1), sink, F32)
        for si, (_, valid) in enumerate(sources):
            s = _dot_nt(qh, kb[si][g])
            if valid is not None:
                s = jnp.where(valid, s, -jnp.inf)
            scores.append(s)
            m = jnp.maximum(m, jnp.max(s, axis=-1, keepdims=True))
        denom = jnp.exp(sink - m)
        acc = jnp.zeros((ATTN_BLOCK, HEAD_DIM), F32)
        for si, s in enumerate(scores):
            p = jnp.exp(s - m)
            denom = denom + jnp.sum(p, axis=-1, keepdims=True)
            acc = acc + _dot(p.astype(BF16), vb[si][g])
        outs.append(acc / denom)
    o_ref[...] = jnp.concatenate(outs, axis=-1)


def _attention(qkv, kv_ctx, sink, window):
    b, l, _ = qkv.shape
    n_ctx = kv_ctx.shape[1]
    nb = l // ATTN_BLOCK
    kv_col = ATTN_W // (2 * KV_W)
    in_specs = [
        pl.BlockSpec(memory_space=pltpu.SMEM),
        pl.BlockSpec((None, ATTN_BLOCK, ATTN_W), lambda bi, n: (bi, n, 0)),
    ]
    args = [sink, qkv]
    if window:
        in_specs += [
            pl.BlockSpec((None, ATTN_BLOCK, 2 * KV_W), lambda bi, n: (bi, jnp.maximum(n - 1, 0), kv_col)),
            pl.BlockSpec((None, ATTN_BLOCK, 2 * KV_W), lambda bi, n: (bi, n, kv_col)),
            pl.BlockSpec((None, ATTN_BLOCK, 2 * KV_W), lambda bi, n: (bi, jnp.minimum(n + 1, nb - 1), kv_col)),
        ]
        args += [qkv, qkv, qkv]
    in_specs.append(pl.BlockSpec((None, n_ctx, 2 * KV_W), lambda bi, n: (bi, 0, kv_col)))
    args.append(kv_ctx)
    return pl.pallas_call(
        functools.partial(_attn_kernel, window=window, n_blocks=nb),
        out_shape=jax.ShapeDtypeStruct((b, l, ATTN_W), F32),
        grid=(b, nb),
        in_specs=in_specs,
        out_specs=pl.BlockSpec((None, ATTN_BLOCK, ATTN_W), lambda bi, n: (bi, n, 0)),
        compiler_params=_cparams(("parallel", "parallel")),
        name="attention_window" if window else "attention_context",
    )(*args)


def _gelu(x):
    return jax.nn.gelu(x, approximate=True)


def _sigmoid(x):
    return 1.0 / (1.0 + jnp.exp(-x))


def _mixer_kernel(rest_ref, hprev_ref, hnext_ref, ya_ref, x_ref, gt_ref, poolw_ref, pscale_ref, sguw_ref,
                  sgub_ref, wbr_ref, wout_ref, o_ref, ext_ref, *, tm, tiles_per_seq, seq_len):
    ti = pl.program_id(0) % tiles_per_seq
    z = rest_ref[:, 0:POOL_W]
    ext_ref[0:POOL_HALO, :] = jnp.where(ti > 0, hprev_ref[...], 0.0)
    ext_ref[POOL_HALO:POOL_HALO + tm, :] = z
    ext_ref[POOL_HALO + tm:2 * POOL_HALO + tm, :] = jnp.where(ti < tiles_per_seq - 1, hnext_ref[...], 0.0)
    pos = ti * tm + lax.broadcasted_iota(jnp.int32, (tm, LANES), 0)
    lane = lax.broadcasted_iota(jnp.int32, (tm, LANES), 1)

    def count(size):
        hi = jnp.minimum(pos + size // 2, seq_len)
        lo = jnp.maximum(pos - size // 2, 0)
        return (hi - lo).astype(F32)

    diffs = []
    for lt in range(POOL_W // LANES):
        cols = slice(lt * LANES, (lt + 1) * LANES)

        def shifted(d, cols=cols):
            return ext_ref[POOL_HALO + d:POOL_HALO + d + tm, cols]

        small, large = POOL_SIZES[2 * lt], POOL_SIZES[2 * lt + 1]
        s = shifted(-1) + shifted(0)
        width = 2
        sums = {}
        while width <= large:
            sums[width] = s
            half = width // 2
            if 2 * width <= large:
                for d in range(half, width):
                    s = s + shifted(-d - 1) + shifted(d)
            width *= 2
        mean = jnp.where(lane < POOL_GROUP_W, sums[small] / count(small), sums[large] / count(large))
        diffs.append(mean - z[:, cols])
    d = jnp.concatenate(diffs, axis=-1).astype(BF16)
    y_pool = _dot(d, poolw_ref[...]) * pscale_ref[...]

    u = _gelu(rest_ref[:, POOL_W:POOL_W + SGU_W])
    v = _gelu(rest_ref[:, POOL_W + SGU_W:POOL_W + 2 * SGU_W])
    vn = (v * lax.rsqrt(jnp.mean(v * v, axis=-1, keepdims=True) + EPS)).astype(BF16)
    group = lax.broadcasted_iota(jnp.int32, (SGU_CHUNK, SGU_W), 1) // SGU_GROUP_W
    mixed = []
    for c in range(tm // SGU_CHUNK):
        vc = vn[c * SGU_CHUNK:(c + 1) * SGU_CHUNK, :]
        mc = sgub_ref[...]
        for hg in range(SGU_GROUPS):
            mc = mc + jnp.where(group == hg, _dot(sguw_ref[hg], vc), 0.0)
        mixed.append(mc)
    y_sgu = u * jnp.concatenate(mixed, axis=0)

    g_off = POOL_W + 2 * SGU_W
    merged = _sigmoid(rest_ref[:, g_off:g_off + D_MODEL]) * _dot(ya_ref[...].astype(BF16), wbr_ref[0:ATTN_W, :])
    merged = merged + _sigmoid(rest_ref[:, g_off + D_MODEL:g_off + 2 * D_MODEL]) * _dot(
        y_pool.astype(BF16), wbr_ref[ATTN_W:ATTN_W + POOL_W, :])
    merged = merged + _sigmoid(rest_ref[:, g_off + 2 * D_MODEL:g_off + 3 * D_MODEL]) * _dot(
        y_sgu.astype(BF16), wbr_ref[ATTN_W + POOL_W:BR_W, :])
    o_ref[...] = x_ref[...] + gt_ref[...] * _dot(merged.astype(BF16), wout_ref[...])


def _mixer(rest, y_attn, x2d, gt, poolw_bd, pscale, sguw, sgub_full, wbr, wout, rows_per_mod, seq_len):
    t = x2d.shape[0]
    tm = min(TOKEN_TILE, seq_len)
    tiles_per_seq = seq_len // tm
    tiles_per_mod = rows_per_mod // tm
    halo_blocks = tm // POOL_HALO
    n_halo = t // POOL_HALO
    kern = functools.partial(_mixer_kernel, tm=tm, tiles_per_seq=tiles_per_seq, seq_len=seq_len)
    return pl.pallas_call(
        kern,
        out_shape=jax.ShapeDtypeStruct((t, D_MODEL), F32),
        grid=(t // tm,),
        in_specs=[
            pl.BlockSpec((tm, REST_W), lambda i: (i, 0)),
            pl.BlockSpec((POOL_HALO, POOL_W), lambda i: (jnp.maximum(i * halo_blocks - 1, 0), 0)),
            pl.BlockSpec((POOL_HALO, POOL_W), lambda i: (jnp.minimum((i + 1) * halo_blocks, n_halo - 1), 0)),
            pl.BlockSpec((tm, ATTN_W), lambda i: (i, 0)),
            pl.BlockSpec((tm, D_MODEL), lambda i: (i, 0)),
            pl.BlockSpec((None, 1, D_MODEL), lambda i: (i // tiles_per_mod, 0, 0)),
            pl.BlockSpec((POOL_W, POOL_W), lambda i: (0, 0)),
            pl.BlockSpec((1, POOL_W), lambda i: (0, 0)),
            pl.BlockSpec((SGU_GROUPS, SGU_CHUNK, SGU_CHUNK), lambda i: (0, 0, 0)),
            pl.BlockSpec((SGU_CHUNK, SGU_W), lambda i: (0, 0)),
            pl.BlockSpec((BR_W, D_MODEL), lambda i: (0, 0)),
            pl.BlockSpec((D_MODEL, D_MODEL), lambda i: (0, 0)),
        ],
        out_specs=pl.BlockSpec((tm, D_MODEL), lambda i: (i, 0)),
        scratch_shapes=[pltpu.VMEM((tm + 2 * POOL_HALO, POOL_W), F32)],
        compiler_params=_cparams(("parallel",)),
        name="mixer",
    )(rest, rest, rest, y_attn, x2d, gt, poolw_bd, pscale, sguw, sgub_full, wbr, wout)


def _top16_rows(s, row_iota, n_rows, emit):
    for r in range(PEER_TOPK):
        m = jnp.max(s, axis=0, keepdims=True)
        idx = jnp.min(jnp.where(s == m, row_iota, n_rows), axis=0, keepdims=True)
        hit = row_iota == idx
        emit(r, m, hit, idx)
        s = jnp.where(hit, -jnp.inf, s)


def _peer_route_kernel(q_ref, sk_ref, e_ref, g_ref, sv_ref, si_ref, ts_ref, te_ref, *, tm):
    key_iota = lax.broadcasted_iota(jnp.int32, (PEER_KEYS, tm), 0)

    def sub_keys(hp, carry):
        s = _dot_nt(sk_ref[hp], q_ref[hp].astype(BF16))

        def emit(r, m, hit, idx):
            sv_ref[hp, r:r + 1, :] = m
            si_ref[hp, r:r + 1, :] = idx

        _top16_rows(s, key_iota, PEER_KEYS, emit)
        return carry

    lax.fori_loop(0, 2 * PEER_HEADS, sub_keys, 0)

    n_cand = PEER_TOPK * PEER_TOPK
    cand_iota = lax.broadcasted_iota(jnp.int32, (n_cand, tm), 0)

    def head(h, carry):
        sv0, sv1 = sv_ref[2 * h], sv_ref[2 * h + 1]
        si0, si1 = si_ref[2 * h], si_ref[2 * h + 1]
        cand_s = jnp.concatenate([sv0[a:a + 1, :] + sv1 for a in range(PEER_TOPK)], axis=0)
        cand_e = jnp.concatenate([si0[a:a + 1, :] * PEER_KEYS + si1 for a in range(PEER_TOPK)], axis=0)
        row0 = pl.multiple_of(h * PEER_TOPK, PEER_TOPK)

        def emit(r, m, hit, idx):
            ts_ref[r:r + 1, :] = m
            te_ref[r:r + 1, :] = jnp.max(jnp.where(hit, cand_e, -1), axis=0, keepdims=True)

        _top16_rows(cand_s, cand_iota, n_cand, emit)
        e_ref[pl.ds(row0, PEER_TOPK), :] = te_ref[...]
        ts = ts_ref[...]
        ex = jnp.exp(ts - jnp.max(ts, axis=0, keepdims=True))
        g_ref[pl.ds(row0, PEER_TOPK), :] = ex / jnp.sum(ex, axis=0, keepdims=True)
        return carry

    lax.fori_loop(0, PEER_HEADS, head, 0)


def _peer_route(q3, subkeys_bf16):
    t = q3.shape[1]
    tm = PEER_TILE
    nblk = t // tm
    return pl.pallas_call(
        functools.partial(_peer_route_kernel, tm=tm),
        out_shape=(jax.ShapeDtypeStruct((nblk, PEER_SLOTS, tm), jnp.int32),
                   jax.ShapeDtypeStruct((nblk, PEER_SLOTS, tm), F32)),
        grid=(nblk,),
        in_specs=[
            pl.BlockSpec((2 * PEER_HEADS, tm, PEER_HALF), lambda i: (0, i, 0)),
            pl.BlockSpec((2 * PEER_HEADS, PEER_KEYS, PEER_HALF), lambda i: (0, 0, 0)),
        ],
        out_specs=(pl.BlockSpec((None, PEER_SLOTS, tm), lambda i: (i, 0, 0)),
                   pl.BlockSpec((None, PEER_SLOTS, tm), lambda i: (i, 0, 0))),
        scratch_shapes=[
            pltpu.VMEM((2 * PEER_HEADS, PEER_TOPK, tm), F32),
            pltpu.VMEM((2 * PEER_HEADS, PEER_TOPK, tm), jnp.int32),
            pltpu.VMEM((PEER_TOPK, tm), F32),
            pltpu.VMEM((PEER_TOPK, tm), jnp.int32),
        ],
        compiler_params=_cparams(("parallel",)),
        name="peer_route",
    )(q3, subkeys_bf16)


def _peer_apply_kernel(idx_ref, gate_ref, h_ref, x_ref, gt_ref, uv_hbm, o_ref, buf_ref, sem_ref, *, tb):
    lookahead = GATHER_SLOTS - 1
    n_groups = tb // TOKEN_GROUP

    def start_gather(tok, slot):
        for k in range(PEER_SLOTS):
            pltpu.make_async_copy(uv_hbm.at[idx_ref[tok, k]], buf_ref.at[slot, pl.ds(k, 1), :],
                                  sem_ref.at[slot]).start(priority=k % 2)

    def wait_gather(slot):
        pltpu.make_async_copy(uv_hbm.at[pl.ds(0, PEER_SLOTS), 0], buf_ref.at[slot], sem_ref.at[slot]).wait()

    for tok in range(lookahead):
        start_gather(tok, tok % GATHER_SLOTS)

    tok_lane = lax.broadcasted_iota(jnp.int32, (PEER_SLOTS, tb), 1)

    def group(base, is_last):
        h8 = h_ref[pl.ds(base, TOKEN_GROUP), :]
        rows = []
        for j in range(TOKEN_GROUP):
            tok = base + j
            slot = j % GATHER_SLOTS
            if not is_last or j + lookahead < TOKEN_GROUP:
                start_gather(tok + lookahead, (j + lookahead) % GATHER_SLOTS)
            wait_gather(slot)
            u = lax.bitcast_convert_type(buf_ref[slot] & jnp.uint32(0xFFFF0000), F32)
            act = jnp.sum(u * h8[j:j + 1, :], axis=1, keepdims=True)
            gate = jnp.sum(jnp.where(tok_lane == tok, gate_ref[...], 0.0), axis=1, keepdims=True)
            w = gate * _gelu(act)
            v = lax.bitcast_convert_type(buf_ref[slot] << 16, F32)
            rows.append(jnp.sum(w * v, axis=0, keepdims=True))
        out8 = jnp.concatenate(rows, axis=0)
        o_ref[pl.ds(base, TOKEN_GROUP), :] = x_ref[pl.ds(base, TOKEN_GROUP), :] + gt_ref[...] * out8

    def body(gi, carry):
        group(pl.multiple_of(gi * TOKEN_GROUP, TOKEN_GROUP), False)
        return carry

    lax.fori_loop(0, n_groups - 1, body, 0)
    group((n_groups - 1) * TOKEN_GROUP, True)


def _peer_apply(experts_tok, gates, h2d, x2d, gt, uv_table, rows_per_mod):
    t = x2d.shape[0]
    tb = PEER_TILE
    tiles_per_mod = rows_per_mod // tb
    return pl.pallas_call(
        functools.partial(_peer_apply_kernel, tb=tb),
        out_shape=jax.ShapeDtypeStruct((t, D_MODEL), F32),
        grid=(t // tb,),
        in_specs=[
            pl.BlockSpec((tb, PEER_SLOTS), lambda i: (i, 0), memory_space=pltpu.SMEM),
            pl.BlockSpec((None, PEER_SLOTS, tb), lambda i: (i, 0, 0)),
            pl.BlockSpec((tb, D_MODEL), lambda i: (i, 0)),
            pl.BlockSpec((tb, D_MODEL), lambda i: (i, 0)),
            pl.BlockSpec((None, 1, D_MODEL), lambda i: (i // tiles_per_mod, 0, 0)),
            pl.BlockSpec(memory_space=pl.ANY),
        ],
        out_specs=pl.BlockSpec((tb, D_MODEL), lambda i: (i, 0)),
        scratch_shapes=[
            pltpu.VMEM((GATHER_SLOTS, PEER_SLOTS, D_MODEL), jnp.uint32),
            pltpu.SemaphoreType.DMA((GATHER_SLOTS,)),
        ],
        compiler_params=_cparams(("arbitrary",), disable_bounds_checks=True),
        name="peer_apply",
    )(experts_tok, gates, h2d, x2d, gt, uv_table)


def _final_norm_kernel(x_ref, g_ref, o_ref):
    x = x_ref[...]
    o_ref[...] = (x * lax.rsqrt(jnp.mean(x * x, axis=-1, keepdims=True) + EPS)) * g_ref[...]


def _final_norm(x2d, g):
    t = x2d.shape[0]
    tm = min(2 * TOKEN_TILE, t)
    return pl.pallas_call(
        _final_norm_kernel,
        out_shape=jax.ShapeDtypeStruct((t, D_MODEL), F32),
        grid=(t // tm,),
        in_specs=[pl.BlockSpec((tm, D_MODEL), lambda i: (i, 0)), pl.BlockSpec((1, D_MODEL), lambda i: (0, 0))],
        out_specs=pl.BlockSpec((tm, D_MODEL), lambda i: (i, 0)),
        compiler_params=_cparams(("parallel",)),
        name="final_norm",
    )(x2d, g)


def _rope_tables(length):
    rows = length // GRID_W
    row = jnp.repeat(jnp.arange(rows, dtype=F32), GRID_W)
    col = jnp.tile(jnp.arange(GRID_W, dtype=F32), rows)
    inv = ROPE_BASE ** (-jnp.arange(ROPE_FREQS, dtype=F32) / ROPE_FREQS)
    ang = jnp.stack([row[:, None] * inv, col[:, None] * inv], axis=1)
    cos, sin = jnp.cos(ang), jnp.sin(ang)
    cos_h = jnp.broadcast_to(cos[:, :, None, :], (length, 2, 2, ROPE_FREQS)).reshape(length, HEAD_DIM)
    sin_h = jnp.stack([-sin, sin], axis=2).reshape(length, HEAD_DIM)
    reps = LANES // HEAD_DIM
    return jnp.tile(cos_h, (1, reps)), jnp.tile(sin_h, (1, reps))


def _pack_expert_tables(u_tab, v_tab):
    def bits(a):
        return lax.bitcast_convert_type(a.astype(BF16), jnp.uint16).astype(jnp.uint32)

    return ((bits(u_tab) << 16) | bits(v_tab))[:, None, :]


def _peer_layer(x2d, g, sc, sh, gt, rows_per_mod, wq, subkeys, uv_table):
    t = x2d.shape[0]
    q3, h2d = _peer_query(x2d, g, sc, sh, wq, rows_per_mod)
    experts, gates = _peer_route(q3, subkeys)
    experts_tok = jnp.transpose(experts, (0, 2, 1)).reshape(t, PEER_SLOTS)
    return _peer_apply(experts_tok, gates, h2d, x2d, gt, uv_table, rows_per_mod)


def kernel(x, c, ctx, c_ctx, w_mod, b_mod, g_mix, g_ffn, w_in, attn_sink, pool_w, pool_scale, sgu_w, sgu_b,
           w_br_attn, w_br_pool, w_br_sgu, w_out, peer_wq, peer_subkeys, peer_u, peer_v, g_final):
    b, l, d = x.shape
    n_ctx = ctx.shape[1]
    depth = w_mod.shape[0]
    assert d == D_MODEL and l % TOKEN_TILE == 0 and n_ctx % SGU_CHUNK == 0

    n_rows = -(-(b + 1) // SUBLANES) * SUBLANES
    cond = jnp.zeros((n_rows, d), F32).at[:b].set(c).at[b].set(c_ctx)
    mod = _adaln(cond, w_mod, b_mod).reshape(depth, n_rows, N_MOD, 1, d)

    rope = _rope_tables(l)
    x2d = x.reshape(b * l, d)
    xc2d = ctx.reshape(b * n_ctx, d)

    for layer in range(depth):
        last = layer == depth - 1
        m_lat = [mod[layer, :b, i] for i in range(N_MOD)]
        m_ctx = [mod[layer, b:b + 1, i] for i in range(N_MOD)]
        g1 = g_mix[layer].reshape(1, d)
        g2 = g_ffn[layer].reshape(1, d)

        w_in_b = w_in[layer].astype(BF16)
        qkv, rest = _in_proj(x2d, g1, m_lat[1], m_lat[0], w_in_b, l, rope, l)
        qkv_c, rest_c = _in_proj(xc2d, g1, m_ctx[1], m_ctx[0], w_in_b, b * n_ctx, None, n_ctx)

        y_attn = _attention(qkv.reshape(b, l, QKV_W), qkv_c.reshape(b, n_ctx, QKV_W), attn_sink[layer], True)

        poolw_bd = jnp.zeros((POOL_W, POOL_W), F32)
        for gi in range(len(POOL_SIZES)):
            sl = slice(gi * POOL_GROUP_W, (gi + 1) * POOL_GROUP_W)
            poolw_bd = poolw_bd.at[sl, sl].set(pool_w[layer, gi])
        poolw_bd = poolw_bd.astype(BF16)
        pscale = pool_scale[layer].reshape(1, POOL_W)
        sguw = sgu_w[layer].astype(BF16)
        sgub_full = jnp.repeat(sgu_b[layer].T, SGU_GROUP_W, axis=1)
        wbr = jnp.concatenate([w_br_attn[layer], w_br_pool[layer], w_br_sgu[layer]], axis=0).astype(BF16)
        wout = w_out[layer].astype(BF16)
        mix_w = (poolw_bd, pscale, sguw, sgub_full, wbr, wout)

        x2d = _mixer(rest, y_attn.reshape(b * l, ATTN_W), x2d, m_lat[2], *mix_w, l, l)
        if not last:
            y_attn_c = _attention(qkv_c.reshape(b, n_ctx, QKV_W), qkv_c.reshape(b, n_ctx, QKV_W),
                                  attn_sink[layer], False)
            xc2d = _mixer(rest_c, y_attn_c.reshape(b * n_ctx, ATTN_W), xc2d, m_ctx[2], *mix_w,
                          b * n_ctx, n_ctx)

        wq = peer_wq[layer].astype(BF16)
        subkeys = peer_subkeys[layer].reshape(2 * PEER_HEADS, PEER_KEYS, PEER_HALF).astype(BF16)
        uv_table = _pack_expert_tables(peer_u[layer], peer_v[layer])
        x2d = _peer_layer(x2d, g2, m_lat[4], m_lat[3], m_lat[5], l, wq, subkeys, uv_table)
        if not last:
            xc2d = _peer_layer(xc2d, g2, m_ctx[4], m_ctx[3], m_ctx[5], b * n_ctx, wq, subkeys, uv_table)

    return _final_norm(x2d, g_final.reshape(1, d)).reshape(b, l, d)
```

```python
import functools

import jax
import jax.numpy as jnp
from jax import lax
from jax.experimental import pallas as pl
from jax.experimental.pallas import tpu as pltpu

F32 = jnp.float32
BF16 = jnp.bfloat16

D_MODEL = 1024
EPS = 1e-6
N_MOD = 6
GRID_W = 64

N_HEADS = 8
N_KV_HEADS = 2
GQA_GROUP = N_HEADS // N_KV_HEADS
HEAD_DIM = 64
WINDOW = 128
ATTN_BLOCK = 128
ATTN_W = N_HEADS * HEAD_DIM
KV_W = N_KV_HEADS * HEAD_DIM
QKV_W = ATTN_W + 2 * KV_W
ROPE_BASE = 10000.0
ROPE_FREQS = HEAD_DIM // 4

POOL_SIZES = (2, 4, 8, 16)
POOL_GROUP_W = 64
POOL_W = len(POOL_SIZES) * POOL_GROUP_W
POOL_HALO = 8

SGU_CHUNK = 128
SGU_GROUPS = 4
SGU_W = 256
SGU_GROUP_W = SGU_W // SGU_GROUPS

N_BRANCH = 3
REST_W = POOL_W + 2 * SGU_W + N_BRANCH * D_MODEL
IN_W = QKV_W + REST_W
BR_W = ATTN_W + POOL_W + SGU_W

PEER_HEADS = 8
PEER_KEYS = 128
PEER_HALF = 128
PEER_TOPK = 16
PEER_SLOTS = PEER_HEADS * PEER_TOPK
PEER_QW = PEER_HEADS * 2 * PEER_HALF

LANES = 128
SUBLANES = 8
VMEM_LIMIT = 48 * 1024 * 1024

TOKEN_TILE = 256
PEER_TILE = 128
GATHER_SLOTS = 4
TOKEN_GROUP = 8
ROUTE_INTERLEAVE = 2
_NO_ROW = 1e9


def _cparams(sem, **kw):
    return pltpu.CompilerParams(dimension_semantics=sem, vmem_limit_bytes=VMEM_LIMIT, **kw)


def _dot(a, b):
    return jnp.dot(a, b, preferred_element_type=F32)


def _dot_nt(a, b):
    return lax.dot_general(a, b, (((1,), (1,)), ((), ())), preferred_element_type=F32)


def _split_bf16(a):
    hi = a.astype(BF16)
    lo = (a - hi.astype(F32)).astype(BF16)
    return hi, lo


def _adaln_kernel(c_ref, w_ref, b_ref, o_ref):
    c = c_ref[...]
    s = c * (1.0 / (1.0 + jnp.exp(-c)))
    s_hi, s_lo = _split_bf16(s)
    w_hi, w_lo = _split_bf16(w_ref[...])
    acc = _dot(s_hi, w_hi) + (_dot(s_hi, w_lo) + _dot(s_lo, w_hi))
    o_ref[...] = acc + b_ref[...]


def _adaln(cond, w_mod, b_mod):
    depth = w_mod.shape[0]
    r = cond.shape[0]
    n = w_mod.shape[2]
    tn = D_MODEL
    return pl.pallas_call(
        _adaln_kernel,
        out_shape=jax.ShapeDtypeStruct((depth, r, n), F32),
        grid=(depth, n // tn),
        in_specs=[
            pl.BlockSpec((r, D_MODEL), lambda l, j: (0, 0)),
            pl.BlockSpec((None, D_MODEL, tn), lambda l, j: (l, 0, j)),
            pl.BlockSpec((None, 1, tn), lambda l, j: (l, 0, j)),
        ],
        out_specs=pl.BlockSpec((None, r, tn), lambda l, j: (l, 0, j)),
        compiler_params=_cparams(("arbitrary", "arbitrary")),
        name="adaln",
    )(cond, w_mod, b_mod.reshape(depth, 1, n))


def _modulated_norm(x, g, sc, sh):
    y = x * lax.rsqrt(jnp.mean(x * x, axis=-1, keepdims=True) + EPS)
    return (y * g) * (1.0 + sc) + sh


def _rope_tile(x, cos, sin_signed):
    lane = lax.broadcasted_iota(jnp.int32, x.shape, 1)
    first_half = (lane % (2 * ROPE_FREQS)) < ROPE_FREQS
    partner = jnp.where(first_half, pltpu.roll(x, LANES - ROPE_FREQS, 1), pltpu.roll(x, ROPE_FREQS, 1))
    return x * cos + partner * sin_signed


def _in_proj_kernel(x_ref, g_ref, sc_ref, sh_ref, w_ref, *rest, rope):
    if rope:
        cos_ref, sin_ref, qkv_ref, rest_ref = rest
    else:
        qkv_ref, rest_ref = rest
    hb = _modulated_norm(x_ref[...], g_ref[...], sc_ref[...], sh_ref[...]).astype(BF16)
    for c in range(QKV_W // LANES):
        cols = slice(c * LANES, (c + 1) * LANES)
        acc = _dot(hb, w_ref[:, cols])
        if rope and c * LANES < ATTN_W + KV_W:
            acc = _rope_tile(acc, cos_ref[...], sin_ref[...])
        qkv_ref[:, cols] = acc
    chunk = 768
    for c in range(REST_W // chunk):
        rest_ref[:, c * chunk:(c + 1) * chunk] = _dot(hb, w_ref[:, QKV_W + c * chunk:QKV_W + (c + 1) * chunk])


def _in_proj(x2d, g, sc, sh, w_bf16, rows_per_mod, rope_tables, seq_len):
    t = x2d.shape[0]
    tm = min(TOKEN_TILE, t)
    tiles_per_mod = rows_per_mod // tm
    rope = rope_tables is not None
    in_specs = [
        pl.BlockSpec((tm, D_MODEL), lambda i: (i, 0)),
        pl.BlockSpec((1, D_MODEL), lambda i: (0, 0)),
        pl.BlockSpec((None, 1, D_MODEL), lambda i: (i // tiles_per_mod, 0, 0)),
        pl.BlockSpec((None, 1, D_MODEL), lambda i: (i // tiles_per_mod, 0, 0)),
        pl.BlockSpec((D_MODEL, IN_W), lambda i: (0, 0)),
    ]
    args = [x2d, g, sc, sh, w_bf16]
    if rope:
        tiles_per_seq = seq_len // tm
        in_specs += [pl.BlockSpec((tm, LANES), lambda i: (i % tiles_per_seq, 0))] * 2
        args += list(rope_tables)
    return pl.pallas_call(
        functools.partial(_in_proj_kernel, rope=rope),
        out_shape=(jax.ShapeDtypeStruct((t, QKV_W), F32), jax.ShapeDtypeStruct((t, REST_W), F32)),
        grid=(t // tm,),
        in_specs=in_specs,
        out_specs=(pl.BlockSpec((tm, QKV_W), lambda i: (i, 0)), pl.BlockSpec((tm, REST_W), lambda i: (i, 0))),
        compiler_params=_cparams(("parallel",)),
        name="in_proj",
    )(*args)


def _peer_query_kernel(x_ref, g_ref, sc_ref, sh_ref, w_ref, q_ref, h_ref):
    h = _modulated_norm(x_ref[...], g_ref[...], sc_ref[...], sh_ref[...])
    h_ref[...] = h
    hb = h.astype(BF16)
    for hp in range(2 * PEER_HEADS):
        q_ref[hp] = _dot(hb, w_ref[:, hp * PEER_HALF:(hp + 1) * PEER_HALF])


def _peer_query(x2d, g, sc, sh, wq_bf16, rows_per_mod):
    t = x2d.shape[0]
    tm = min(TOKEN_TILE, t)
    tiles_per_mod = rows_per_mod // tm
    return pl.pallas_call(
        _peer_query_kernel,
        out_shape=(jax.ShapeDtypeStruct((2 * PEER_HEADS, t, PEER_HALF), F32),
                   jax.ShapeDtypeStruct((t, D_MODEL), F32)),
        grid=(t // tm,),
        in_specs=[
            pl.BlockSpec((tm, D_MODEL), lambda i: (i, 0)),
            pl.BlockSpec((1, D_MODEL), lambda i: (0, 0)),
            pl.BlockSpec((None, 1, D_MODEL), lambda i: (i // tiles_per_mod, 0, 0)),
            pl.BlockSpec((None, 1, D_MODEL), lambda i: (i // tiles_per_mod, 0, 0)),
            pl.BlockSpec((D_MODEL, PEER_QW), lambda i: (0, 0)),
        ],
        out_specs=(pl.BlockSpec((2 * PEER_HEADS, tm, PEER_HALF), lambda i: (0, i, 0)),
                   pl.BlockSpec((tm, D_MODEL), lambda i: (i, 0))),
        compiler_params=_cparams(("parallel",)),
        name="peer_query",
    )(x2d, g, sc, sh, wq_bf16)


def _attn_kernel(sink_ref, q_ref, *rest, window, n_blocks):
    if window:
        kvp_ref, kvc_ref, kvn_ref, ctx_ref, o_ref = rest
    else:
        ctx_ref, o_ref = rest
    n = pl.program_id(1)
    q = q_ref[...] * (HEAD_DIM ** -0.5)
    sources = []
    if window:
        qi = lax.broadcasted_iota(jnp.int32, (ATTN_BLOCK, ATTN_BLOCK), 0)
        kj = lax.broadcasted_iota(jnp.int32, (ATTN_BLOCK, ATTN_BLOCK), 1)
        sources.append((kvp_ref[...], (kj >= qi) & (n > 0)))
        sources.append((kvc_ref[...], None))
        sources.append((kvn_ref[...], (kj <= qi) & (n < n_blocks - 1)))
    sources.append((ctx_ref[...], None))
    kb = [[kv[:, g * HEAD_DIM:(g + 1) * HEAD_DIM].astype(BF16) for g in range(N_KV_HEADS)] for kv, _ in sources]
    vb = [[kv[:, KV_W + g * HEAD_DIM:KV_W + (g + 1) * HEAD_DIM].astype(BF16) for g in range(N_KV_HEADS)]
          for kv, _ in sources]
    outs = []
    for h in range(N_HEADS):
        g = h // GQA_GROUP
        qh = q[:, h * HEAD_DIM:(h + 1) * HEAD_DIM].astype(BF16)
        sink = sink_ref[h]
        scores = []
        m = jnp.full((ATTN_BLOCK, 1), sink, F32)
        for si, (_, valid) in enumerate(sources):
            s = _dot_nt(qh, kb[si][g])
            if valid is not None:
                s = jnp.where(valid, s, -jnp.inf)
            scores.append(s)
            m = jnp.maximum(m, jnp.max(s, axis=-1, keepdims=True))
        denom = jnp.exp(sink - m)
        acc = jnp.zeros((ATTN_BLOCK, HEAD_DIM), F32)
        for si, s in enumerate(scores):
            p = jnp.exp(s - m)
            denom = denom + jnp.sum(p, axis=-1, keepdims=True)
            acc = acc + _dot(p.astype(BF16), vb[si][g])
        outs.append(acc / denom)
    o_ref[...] = jnp.concatenate(outs, axis=-1)


def _attention(qkv, kv_ctx, sink, window):
    b, l, _ = qkv.shape
    n_ctx = kv_ctx.shape[1]
    nb = l // ATTN_BLOCK
    kv_col = ATTN_W // (2 * KV_W)
    in_specs = [
        pl.BlockSpec(memory_space=pltpu.SMEM),
        pl.BlockSpec((None, ATTN_BLOCK, ATTN_W), lambda bi, n: (bi, n, 0)),
    ]
    args = [sink, qkv]
    if window:
        in_specs += [
            pl.BlockSpec((None, ATTN_BLOCK, 2 * KV_W), lambda bi, n: (bi, jnp.maximum(n - 1, 0), kv_col)),
            pl.BlockSpec((None, ATTN_BLOCK, 2 * KV_W), lambda bi, n: (bi, n, kv_col)),
            pl.BlockSpec((None, ATTN_BLOCK, 2 * KV_W), lambda bi, n: (bi, jnp.minimum(n + 1, nb - 1), kv_col)),
        ]
        args += [qkv, qkv, qkv]
    in_specs.append(pl.BlockSpec((None, n_ctx, 2 * KV_W), lambda bi, n: (bi, 0, kv_col)))
    args.append(kv_ctx)
    return pl.pallas_call(
        functools.partial(_attn_kernel, window=window, n_blocks=nb),
        out_shape=jax.ShapeDtypeStruct((b, l, ATTN_W), F32),
        grid=(b, nb),
        in_specs=in_specs,
        out_specs=pl.BlockSpec((None, ATTN_BLOCK, ATTN_W), lambda bi, n: (bi, n, 0)),
        compiler_params=_cparams(("parallel", "parallel")),
        name="attention_window" if window else "attention_context",
    )(*args)


def _gelu(x):
    return jax.nn.gelu(x, approximate=True)


def _sigmoid(x):
    return 1.0 / (1.0 + jnp.exp(-x))


def _mixer_kernel(rest_ref, hprev_ref, hnext_ref, ya_ref, x_ref, gt_ref, poolw_ref, pscale_ref, sguw_ref,
                  sgub_ref, wbr_ref, wout_ref, o_ref, ext_ref, *, tm, tiles_per_seq, seq_len):
    ti = pl.program_id(0) % tiles_per_seq
    z = rest_ref[:, 0:POOL_W]
    ext_ref[0:POOL_HALO, :] = jnp.where(ti > 0, hprev_ref[...], 0.0)
    ext_ref[POOL_HALO:POOL_HALO + tm, :] = z
    ext_ref[POOL_HALO + tm:2 * POOL_HALO + tm, :] = jnp.where(ti < tiles_per_seq - 1, hnext_ref[...], 0.0)
    pos = ti * tm + lax.broadcasted_iota(jnp.int32, (tm, LANES), 0)
    lane = lax.broadcasted_iota(jnp.int32, (tm, LANES), 1)

    def count(size):
        hi = jnp.minimum(pos + size // 2, seq_len)
        lo = jnp.maximum(pos - size // 2, 0)
        return (hi - lo).astype(F32)

    diffs = []
    for lt in range(POOL_W // LANES):
        cols = slice(lt * LANES, (lt + 1) * LANES)

        def shifted(d, cols=cols):
            return ext_ref[POOL_HALO + d:POOL_HALO + d + tm, cols]

        small, large = POOL_SIZES[2 * lt], POOL_SIZES[2 * lt + 1]
        s = shifted(-1) + shifted(0)
        width = 2
        sums = {}
        while width <= large:
            sums[width] = s
            half = width // 2
            if 2 * width <= large:
                for d in range(half, width):
                    s = s + shifted(-d - 1) + shifted(d)
            width *= 2
        mean = jnp.where(lane < POOL_GROUP_W, sums[small] / count(small), sums[large] / count(large))
        diffs.append(mean - z[:, cols])
    d = jnp.concatenate(diffs, axis=-1).astype(BF16)
    y_pool = _dot(d, poolw_ref[...]) * pscale_ref[...]

    u = _gelu(rest_ref[:, POOL_W:POOL_W + SGU_W])
    v = _gelu(rest_ref[:, POOL_W + SGU_W:POOL_W + 2 * SGU_W])
    vn = (v * lax.rsqrt(jnp.mean(v * v, axis=-1, keepdims=True) + EPS)).astype(BF16)
    group = lax.broadcasted_iota(jnp.int32, (SGU_CHUNK, SGU_W), 1) // SGU_GROUP_W
    mixed = []
    for c in range(tm // SGU_CHUNK):
        vc = vn[c * SGU_CHUNK:(c + 1) * SGU_CHUNK, :]
        mc = sgub_ref[...]
        for hg in range(SGU_GROUPS):
            mc = mc + jnp.where(group == hg, _dot(sguw_ref[hg], vc), 0.0)
        mixed.append(mc)
    y_sgu = u * jnp.concatenate(mixed, axis=0)

    g_off = POOL_W + 2 * SGU_W
    merged = _sigmoid(rest_ref[:, g_off:g_off + D_MODEL]) * _dot(ya_ref[...].astype(BF16), wbr_ref[0:ATTN_W, :])
    merged = merged + _sigmoid(rest_ref[:, g_off + D_MODEL:g_off + 2 * D_MODEL]) * _dot(
        y_pool.astype(BF16), wbr_ref[ATTN_W:ATTN_W + POOL_W, :])
    merged = merged + _sigmoid(rest_ref[:, g_off + 2 * D_MODEL:g_off + 3 * D_MODEL]) * _dot(
        y_sgu.astype(BF16), wbr_ref[ATTN_W + POOL_W:BR_W, :])
    o_ref[...] = x_ref[...] + gt_ref[...] * _dot(merged.astype(BF16), wout_ref[...])


def _mixer(rest, y_attn, x2d, gt, poolw_bd, pscale, sguw, sgub_full, wbr, wout, rows_per_mod, seq_len):
    t = x2d.shape[0]
    tm = min(TOKEN_TILE, seq_len)
    tiles_per_seq = seq_len // tm
    tiles_per_mod = rows_per_mod // tm
    halo_blocks = tm // POOL_HALO
    n_halo = t // POOL_HALO
    kern = functools.partial(_mixer_kernel, tm=tm, tiles_per_seq=tiles_per_seq, seq_len=seq_len)
    return pl.pallas_call(
        kern,
        out_shape=jax.ShapeDtypeStruct((t, D_MODEL), F32),
        grid=(t // tm,),
        in_specs=[
            pl.BlockSpec((tm, REST_W), lambda i: (i, 0)),
            pl.BlockSpec((POOL_HALO, POOL_W), lambda i: (jnp.maximum(i * halo_blocks - 1, 0), 0)),
            pl.BlockSpec((POOL_HALO, POOL_W), lambda i: (jnp.minimum((i + 1) * halo_blocks, n_halo - 1), 0)),
            pl.BlockSpec((tm, ATTN_W), lambda i: (i, 0)),
            pl.BlockSpec((tm, D_MODEL), lambda i: (i, 0)),
            pl.BlockSpec((None, 1, D_MODEL), lambda i: (i // tiles_per_mod, 0, 0)),
            pl.BlockSpec((POOL_W, POOL_W), lambda i: (0, 0)),
            pl.BlockSpec((1, POOL_W), lambda i: (0, 0)),
            pl.BlockSpec((SGU_GROUPS, SGU_CHUNK, SGU_CHUNK), lambda i: (0, 0, 0)),
            pl.BlockSpec((SGU_CHUNK, SGU_W), lambda i: (0, 0)),
            pl.BlockSpec((BR_W, D_MODEL), lambda i: (0, 0)),
            pl.BlockSpec((D_MODEL, D_MODEL), lambda i: (0, 0)),
        ],
        out_specs=pl.BlockSpec((tm, D_MODEL), lambda i: (i, 0)),
        scratch_shapes=[pltpu.VMEM((tm + 2 * POOL_HALO, POOL_W), F32)],
        compiler_params=_cparams(("parallel",)),
        name="mixer",
    )(rest, rest, rest, y_attn, x2d, gt, poolw_bd, pscale, sguw, sgub_full, wbr, wout)


def _top16_rows(problems, emit):
    scores = [p[0] for p in problems]
    for r in range(PEER_TOPK):
        for i, (_, ids) in enumerate(problems):
            s = scores[i]
            m = jnp.max(s, axis=0, keepdims=True)
            idx = jnp.min(jnp.where(s == m, ids, _NO_ROW), axis=0, keepdims=True)
            hit = ids == idx
            emit(i, r, m, hit, idx)
            scores[i] = jnp.where(hit, -jnp.inf, s)


def _pair_candidates(sv0, sv1, si0, si1, sub):
    subf = sub.astype(F32)
    sums, experts, flat = [], [], []
    for a in range(SUBLANES):
        n_b = PEER_TOPK // (a + 1)
        for b0 in range(0, n_b, SUBLANES):
            s = sv0[a:a + 1, :] + sv1[b0:b0 + SUBLANES, :]
            if n_b - b0 < SUBLANES:
                s = jnp.where(sub < n_b - b0, s, -jnp.inf)
            sums.append(s)
            experts.append(si0[a:a + 1, :] * PEER_KEYS + si1[b0:b0 + SUBLANES, :])
            flat.append(subf + float(a * PEER_TOPK + b0))
    assert PEER_TOPK // (SUBLANES + 1) == 1
    sums.append(sv0[SUBLANES:, :] + sv1[0:1, :])
    experts.append(si0[SUBLANES:, :] * PEER_KEYS + si1[0:1, :])
    flat.append((subf + float(SUBLANES)) * float(PEER_TOPK))
    return jnp.concatenate(sums, axis=0), jnp.concatenate(experts, axis=0), jnp.concatenate(flat, axis=0)


def _peer_route_kernel(q_ref, sk_ref, e_ref, g_ref, sv_ref, si_ref, ts_ref, te_ref, *, tm):
    key_id = lax.broadcasted_iota(jnp.int32, (PEER_KEYS, tm), 0).astype(F32)

    def sub_keys(it, carry):
        hps = [it * ROUTE_INTERLEAVE + u for u in range(ROUTE_INTERLEAVE)]
        problems = [(_dot_nt(sk_ref[hp], q_ref[hp].astype(BF16)), key_id) for hp in hps]

        def emit(i, r, m, hit, idx):
            sv_ref[hps[i], r:r + 1, :] = m
            si_ref[hps[i], r:r + 1, :] = idx.astype(jnp.int32)

        _top16_rows(problems, emit)
        return carry

    lax.fori_loop(0, 2 * PEER_HEADS // ROUTE_INTERLEAVE, sub_keys, 0)

    sub = lax.broadcasted_iota(jnp.int32, (SUBLANES, tm), 0)

    def heads(it, carry):
        hs = [it * ROUTE_INTERLEAVE + u for u in range(ROUTE_INTERLEAVE)]
        problems, experts = [], []
        for h in hs:
            cs, ce, cid = _pair_candidates(sv_ref[2 * h], sv_ref[2 * h + 1], si_ref[2 * h], si_ref[2 * h + 1], sub)
            problems.append((cs, cid))
            experts.append(ce)

        def emit(i, r, m, hit, idx):
            ts_ref[i, r:r + 1, :] = m
            te_ref[i, r:r + 1, :] = jnp.max(jnp.where(hit, experts[i], -1), axis=0, keepdims=True)

        _top16_rows(problems, emit)
        for i, h in enumerate(hs):
            row0 = pl.multiple_of(h * PEER_TOPK, PEER_TOPK)
            e_ref[pl.ds(row0, PEER_TOPK), :] = te_ref[i]
            ts = ts_ref[i]
            ex = jnp.exp(ts - jnp.max(ts, axis=0, keepdims=True))
            g_ref[pl.ds(row0, PEER_TOPK), :] = ex / jnp.sum(ex, axis=0, keepdims=True)
        return carry

    lax.fori_loop(0, PEER_HEADS // ROUTE_INTERLEAVE, heads, 0)


def _peer_route(q3, subkeys_bf16):
    t = q3.shape[1]
    tm = PEER_TILE
    nblk = t // tm
    return pl.pallas_call(
        functools.partial(_peer_route_kernel, tm=tm),
        out_shape=(jax.ShapeDtypeStruct((nblk, PEER_SLOTS, tm), jnp.int32),
                   jax.ShapeDtypeStruct((nblk, PEER_SLOTS, tm), F32)),
        grid=(nblk,),
        in_specs=[
            pl.BlockSpec((2 * PEER_HEADS, tm, PEER_HALF), lambda i: (0, i, 0)),
            pl.BlockSpec((2 * PEER_HEADS, PEER_KEYS, PEER_HALF), lambda i: (0, 0, 0)),
        ],
        out_specs=(pl.BlockSpec((None, PEER_SLOTS, tm), lambda i: (i, 0, 0)),
                   pl.BlockSpec((None, PEER_SLOTS, tm), lambda i: (i, 0, 0))),
        scratch_shapes=[
            pltpu.VMEM((2 * PEER_HEADS, PEER_TOPK, tm), F32),
            pltpu.VMEM((2 * PEER_HEADS, PEER_TOPK, tm), jnp.int32),
            pltpu.VMEM((ROUTE_INTERLEAVE, PEER_TOPK, tm), F32),
            pltpu.VMEM((ROUTE_INTERLEAVE, PEER_TOPK, tm), jnp.int32),
        ],
        compiler_params=_cparams(("parallel",)),
        name="peer_route",
    )(q3, subkeys_bf16)


def _peer_apply_kernel(idx_ref, gate_ref, h_ref, x_ref, gt_ref, uv_hbm, o_ref, buf_ref, sem_ref, *, tb):
    lookahead = GATHER_SLOTS - 1
    n_groups = tb // TOKEN_GROUP

    def start_gather(tok, slot):
        for k in range(PEER_SLOTS):
            pltpu.make_async_copy(uv_hbm.at[idx_ref[tok, k]], buf_ref.at[slot, pl.ds(k, 1), :],
                                  sem_ref.at[slot]).start(priority=k % 2)

    def wait_gather(slot):
        pltpu.make_async_copy(uv_hbm.at[pl.ds(0, PEER_SLOTS), 0], buf_ref.at[slot], sem_ref.at[slot]).wait()

    for tok in range(lookahead):
        start_gather(tok, tok % GATHER_SLOTS)

    tok_lane = lax.broadcasted_iota(jnp.int32, (PEER_SLOTS, tb), 1)

    def group(base, is_last):
        h8 = h_ref[pl.ds(base, TOKEN_GROUP), :]
        rows = []
        for j in range(TOKEN_GROUP):
            tok = base + j
            slot = j % GATHER_SLOTS
            if not is_last or j + lookahead < TOKEN_GROUP:
                start_gather(tok + lookahead, (j + lookahead) % GATHER_SLOTS)
            wait_gather(slot)
            u = lax.bitcast_convert_type(buf_ref[slot] & jnp.uint32(0xFFFF0000), F32)
            act = jnp.sum(u * h8[j:j + 1, :], axis=1, keepdims=True)
            gate = jnp.sum(jnp.where(tok_lane == tok, gate_ref[...], 0.0), axis=1, keepdims=True)
            w = gate * _gelu(act)
            v = lax.bitcast_convert_type(buf_ref[slot] << 16, F32)
            rows.append(jnp.sum(w * v, axis=0, keepdims=True))
        out8 = jnp.concatenate(rows, axis=0)
        o_ref[pl.ds(base, TOKEN_GROUP), :] = x_ref[pl.ds(base, TOKEN_GROUP), :] + gt_ref[...] * out8

    def body(gi, carry):
        group(pl.multiple_of(gi * TOKEN_GROUP, TOKEN_GROUP), False)
        return carry

    lax.fori_loop(0, n_groups - 1, body, 0)
    group((n_groups - 1) * TOKEN_GROUP, True)


def _peer_apply(experts_tok, gates, h2d, x2d, gt, uv_table, rows_per_mod):
    t = x2d.shape[0]
    tb = PEER_TILE
    tiles_per_mod = rows_per_mod // tb
    return pl.pallas_call(
        functools.partial(_peer_apply_kernel, tb=tb),
        out_shape=jax.ShapeDtypeStruct((t, D_MODEL), F32),
        grid=(t // tb,),
        in_specs=[
            pl.BlockSpec((tb, PEER_SLOTS), lambda i: (i, 0), memory_space=pltpu.SMEM),
            pl.BlockSpec((None, PEER_SLOTS, tb), lambda i: (i, 0, 0)),
            pl.BlockSpec((tb, D_MODEL), lambda i: (i, 0)),
            pl.BlockSpec((tb, D_MODEL), lambda i: (i, 0)),
            pl.BlockSpec((None, 1, D_MODEL), lambda i: (i // tiles_per_mod, 0, 0)),
            pl.BlockSpec(memory_space=pl.ANY),
        ],
        out_specs=pl.BlockSpec((tb, D_MODEL), lambda i: (i, 0)),
        scratch_shapes=[
            pltpu.VMEM((GATHER_SLOTS, PEER_SLOTS, D_MODEL), jnp.uint32),
            pltpu.SemaphoreType.DMA((GATHER_SLOTS,)),
        ],
        compiler_params=_cparams(("arbitrary",), disable_bounds_checks=True),
        name="peer_apply",
    )(experts_tok, gates, h2d, x2d, gt, uv_table)


def _final_norm_kernel(x_ref, g_ref, o_ref):
    x = x_ref[...]
    o_ref[...] = (x * lax.rsqrt(jnp.mean(x * x, axis=-1, keepdims=True) + EPS)) * g_ref[...]


def _final_norm(x2d, g):
    t = x2d.shape[0]
    tm = min(2 * TOKEN_TILE, t)
    return pl.pallas_call(
        _final_norm_kernel,
        out_shape=jax.ShapeDtypeStruct((t, D_MODEL), F32),
        grid=(t // tm,),
        in_specs=[pl.BlockSpec((tm, D_MODEL), lambda i: (i, 0)), pl.BlockSpec((1, D_MODEL), lambda i: (0, 0))],
        out_specs=pl.BlockSpec((tm, D_MODEL), lambda i: (i, 0)),
        compiler_params=_cparams(("parallel",)),
        name="final_norm",
    )(x2d, g)


def _rope_tables(length):
    rows = length // GRID_W
    row = jnp.repeat(jnp.arange(rows, dtype=F32), GRID_W)
    col = jnp.tile(jnp.arange(GRID_W, dtype=F32), rows)
    inv = ROPE_BASE ** (-jnp.arange(ROPE_FREQS, dtype=F32) / ROPE_FREQS)
    ang = jnp.stack([row[:, None] * inv, col[:, None] * inv], axis=1)
    cos, sin = jnp.cos(ang), jnp.sin(ang)
    cos_h = jnp.broadcast_to(cos[:, :, None, :], (length, 2, 2, ROPE_FREQS)).reshape(length, HEAD_DIM)
    sin_h = jnp.stack([-sin, sin], axis=2).reshape(length, HEAD_DIM)
    reps = LANES // HEAD_DIM
    return jnp.tile(cos_h, (1, reps)), jnp.tile(sin_h, (1, reps))


def _pack_expert_tables(u_tab, v_tab):
    def bits(a):
        return lax.bitcast_convert_type(a.astype(BF16), jnp.uint16).astype(jnp.uint32)

    return ((bits(u_tab) << 16) | bits(v_tab))[:, None, :]


def _peer_layer(x2d, g, sc, sh, gt, rows_per_mod, wq, subkeys, uv_table):
    t = x2d.shape[0]
    q3, h2d = _peer_query(x2d, g, sc, sh, wq, rows_per_mod)
    experts, gates = _peer_route(q3, subkeys)
    experts_tok = jnp.transpose(experts, (0, 2, 1)).reshape(t, PEER_SLOTS)
    return _peer_apply(experts_tok, gates, h2d, x2d, gt, uv_table, rows_per_mod)


def kernel(x, c, ctx, c_ctx, w_mod, b_mod, g_mix, g_ffn, w_in, attn_sink, pool_w, pool_scale, sgu_w, sgu_b,
           w_br_attn, w_br_pool, w_br_sgu, w_out, peer_wq, peer_subkeys, peer_u, peer_v, g_final):
    b, l, d = x.shape
    n_ctx = ctx.shape[1]
    depth = w_mod.shape[0]
    assert d == D_MODEL and l % TOKEN_TILE == 0 and n_ctx % SGU_CHUNK == 0

    n_rows = -(-(b + 1) // SUBLANES) * SUBLANES
    cond = jnp.zeros((n_rows, d), F32).at[:b].set(c).at[b].set(c_ctx)
    mod = _adaln(cond, w_mod, b_mod).reshape(depth, n_rows, N_MOD, 1, d)

    rope = _rope_tables(l)
    x2d = x.reshape(b * l, d)
    xc2d = ctx.reshape(b * n_ctx, d)

    for layer in range(depth):
        last = layer == depth - 1
        m_lat = [mod[layer, :b, i] for i in range(N_MOD)]
        m_ctx = [mod[layer, b:b + 1, i] for i in range(N_MOD)]
        g1 = g_mix[layer].reshape(1, d)
        g2 = g_ffn[layer].reshape(1, d)

        w_in_b = w_in[layer].astype(BF16)
        qkv, rest = _in_proj(x2d, g1, m_lat[1], m_lat[0], w_in_b, l, rope, l)
        qkv_c, rest_c = _in_proj(xc2d, g1, m_ctx[1], m_ctx[0], w_in_b, b * n_ctx, None, n_ctx)

        y_attn = _attention(qkv.reshape(b, l, QKV_W), qkv_c.reshape(b, n_ctx, QKV_W), attn_sink[layer], True)

        poolw_bd = jnp.zeros((POOL_W, POOL_W), F32)
        for gi in range(len(POOL_SIZES)):
            sl = slice(gi * POOL_GROUP_W, (gi + 1) * POOL_GROUP_W)
            poolw_bd = poolw_bd.at[sl, sl].set(pool_w[layer, gi])
        poolw_bd = poolw_bd.astype(BF16)
        pscale = pool_scale[layer].reshape(1, POOL_W)
        sguw = sgu_w[layer].astype(BF16)
        sgub_full = jnp.repeat(sgu_b[layer].T, SGU_GROUP_W, axis=1)
        wbr = jnp.concatenate([w_br_attn[layer], w_br_pool[layer], w_br_sgu[layer]], axis=0).astype(BF16)
        wout = w_out[layer].astype(BF16)
        mix_w = (poolw_bd, pscale, sguw, sgub_full, wbr, wout)

        x2d = _mixer(rest, y_attn.reshape(b * l, ATTN_W), x2d, m_lat[2], *mix_w, l, l)
        if not last:
            y_attn_c = _attention(qkv_c.reshape(b, n_ctx, QKV_W), qkv_c.reshape(b, n_ctx, QKV_W),
                                  attn_sink[layer], False)
            xc2d = _mixer(rest_c, y_attn_c.reshape(b * n_ctx, ATTN_W), xc2d, m_ctx[2], *mix_w,
                          b * n_ctx, n_ctx)

        wq = peer_wq[layer].astype(BF16)
        subkeys = peer_subkeys[layer].reshape(2 * PEER_HEADS, PEER_KEYS, PEER_HALF).astype(BF16)
        uv_table = _pack_expert_tables(peer_u[layer], peer_v[layer])
        x2d = _peer_layer(x2d, g2, m_lat[4], m_lat[3], m_lat[5], l, wq, subkeys, uv_table)
        if not last:
            xc2d = _peer_layer(xc2d, g2, m_ctx[4], m_ctx[3], m_ctx[5], b * n_ctx, wq, subkeys, uv_table)

    return _final_norm(x2d, g_final.reshape(1, d)).reshape(b, l, d)
```

```python
import functools

import jax
import jax.numpy as jnp
from jax import lax
from jax.experimental import pallas as pl
from jax.experimental.pallas import tpu as pltpu

F32 = jnp.float32
BF16 = jnp.bfloat16

D_MODEL = 1024
EPS = 1e-6
N_MOD = 6
GRID_W = 64

N_HEADS = 8
N_KV_HEADS = 2
GQA_GROUP = N_HEADS // N_KV_HEADS
HEAD_DIM = 64
WINDOW = 128
ATTN_BLOCK = 128
ATTN_W = N_HEADS * HEAD_DIM
KV_W = N_KV_HEADS * HEAD_DIM
QKV_W = ATTN_W + 2 * KV_W
ROPE_BASE = 10000.0
ROPE_FREQS = HEAD_DIM // 4

POOL_SIZES = (2, 4, 8, 16)
POOL_GROUP_W = 64
POOL_W = len(POOL_SIZES) * POOL_GROUP_W
POOL_HALO = 8

SGU_CHUNK = 128
SGU_GROUPS = 4
SGU_W = 256
SGU_GROUP_W = SGU_W // SGU_GROUPS

N_BRANCH = 3
REST_W = POOL_W + 2 * SGU_W + N_BRANCH * D_MODEL
IN_W = QKV_W + REST_W
BR_W = ATTN_W + POOL_W + SGU_W

PEER_HEADS = 8
PEER_KEYS = 128
PEER_HALF = 128
PEER_TOPK = 16
PEER_SLOTS = PEER_HEADS * PEER_TOPK
PEER_QW = PEER_HEADS * 2 * PEER_HALF

LANES = 128
SUBLANES = 8
VMEM_LIMIT = 48 * 1024 * 1024

TOKEN_TILE = 256
ROUTE_TILE = 128
APPLY_TILE = 256
GATHER_SLOTS = 8
TOKEN_GROUP = 8
ROUTE_INTERLEAVE = 2
_NO_ROW = 1e9


def _cparams(sem, **kw):
    return pltpu.CompilerParams(dimension_semantics=sem, vmem_limit_bytes=VMEM_LIMIT, **kw)


def _dot(a, b):
    return jnp.dot(a, b, preferred_element_type=F32)


def _dot_nt(a, b):
    return lax.dot_general(a, b, (((1,), (1,)), ((), ())), preferred_element_type=F32)


def _split_bf16(a):
    hi = a.astype(BF16)
    lo = (a - hi.astype(F32)).astype(BF16)
    return hi, lo


def _adaln_kernel(c_ref, w_ref, b_ref, o_ref):
    c = c_ref[...]
    s = c * (1.0 / (1.0 + jnp.exp(-c)))
    s_hi, s_lo = _split_bf16(s)
    w_hi, w_lo = _split_bf16(w_ref[...])
    acc = _dot(s_hi, w_hi) + (_dot(s_hi, w_lo) + _dot(s_lo, w_hi))
    o_ref[...] = acc + b_ref[...]


def _adaln(cond, w_mod, b_mod):
    depth = w_mod.shape[0]
    r = cond.shape[0]
    n = w_mod.shape[2]
    tn = D_MODEL
    return pl.pallas_call(
        _adaln_kernel,
        out_shape=jax.ShapeDtypeStruct((depth, r, n), F32),
        grid=(depth, n // tn),
        in_specs=[
            pl.BlockSpec((r, D_MODEL), lambda l, j: (0, 0)),
            pl.BlockSpec((None, D_MODEL, tn), lambda l, j: (l, 0, j)),
            pl.BlockSpec((None, 1, tn), lambda l, j: (l, 0, j)),
        ],
        out_specs=pl.BlockSpec((None, r, tn), lambda l, j: (l, 0, j)),
        compiler_params=_cparams(("arbitrary", "arbitrary")),
        name="adaln",
    )(cond, w_mod, b_mod.reshape(depth, 1, n))


def _modulated_norm(x, g, sc, sh):
    y = x * lax.rsqrt(jnp.mean(x * x, axis=-1, keepdims=True) + EPS)
    return (y * g) * (1.0 + sc) + sh


def _rope_tile(x, cos, sin_signed):
    lane = lax.broadcasted_iota(jnp.int32, x.shape, 1)
    first_half = (lane % (2 * ROPE_FREQS)) < ROPE_FREQS
    partner = jnp.where(first_half, pltpu.roll(x, LANES - ROPE_FREQS, 1), pltpu.roll(x, ROPE_FREQS, 1))
    return x * cos + partner * sin_signed


def _in_proj_kernel(x_ref, g_ref, sc_ref, sh_ref, w_ref, *rest, rope):
    if rope:
        cos_ref, sin_ref, qkv_ref, rest_ref = rest
    else:
        qkv_ref, rest_ref = rest
    hb = _modulated_norm(x_ref[...], g_ref[...], sc_ref[...], sh_ref[...]).astype(BF16)
    for c in range(QKV_W // LANES):
        cols = slice(c * LANES, (c + 1) * LANES)
        acc = _dot(hb, w_ref[:, cols])
        if rope and c * LANES < ATTN_W + KV_W:
            acc = _rope_tile(acc, cos_ref[...], sin_ref[...])
        qkv_ref[:, cols] = acc
    chunk = 768
    for c in range(REST_W // chunk):
        rest_ref[:, c * chunk:(c + 1) * chunk] = _dot(hb, w_ref[:, QKV_W + c * chunk:QKV_W + (c + 1) * chunk])


def _in_proj(x2d, g, sc, sh, w_bf16, rows_per_mod, rope_tables, seq_len):
    t = x2d.shape[0]
    tm = min(TOKEN_TILE, t)
    tiles_per_mod = rows_per_mod // tm
    rope = rope_tables is not None
    in_specs = [
        pl.BlockSpec((tm, D_MODEL), lambda i: (i, 0)),
        pl.BlockSpec((1, D_MODEL), lambda i: (0, 0)),
        pl.BlockSpec((None, 1, D_MODEL), lambda i: (i // tiles_per_mod, 0, 0)),
        pl.BlockSpec((None, 1, D_MODEL), lambda i: (i // tiles_per_mod, 0, 0)),
        pl.BlockSpec((D_MODEL, IN_W), lambda i: (0, 0)),
    ]
    args = [x2d, g, sc, sh, w_bf16]
    if rope:
        tiles_per_seq = seq_len // tm
        in_specs += [pl.BlockSpec((tm, LANES), lambda i: (i % tiles_per_seq, 0))] * 2
        args += list(rope_tables)
    return pl.pallas_call(
        functools.partial(_in_proj_kernel, rope=rope),
        out_shape=(jax.ShapeDtypeStruct((t, QKV_W), F32), jax.ShapeDtypeStruct((t, REST_W), F32)),
        grid=(t // tm,),
        in_specs=in_specs,
        out_specs=(pl.BlockSpec((tm, QKV_W), lambda i: (i, 0)), pl.BlockSpec((tm, REST_W), lambda i: (i, 0))),
        compiler_params=_cparams(("parallel",)),
        name="in_proj",
    )(*args)


def _peer_query_kernel(x_ref, g_ref, sc_ref, sh_ref, w_ref, q_ref, h_ref):
    h = _modulated_norm(x_ref[...], g_ref[...], sc_ref[...], sh_ref[...])
    h_ref[...] = h
    hb = h.astype(BF16)
    for hp in range(2 * PEER_HEADS):
        q_ref[hp] = _dot(hb, w_ref[:, hp * PEER_HALF:(hp + 1) * PEER_HALF])


def _peer_query(x2d, g, sc, sh, wq_bf16, rows_per_mod):
    t = x2d.shape[0]
    tm = min(TOKEN_TILE, t)
    tiles_per_mod = rows_per_mod // tm
    return pl.pallas_call(
        _peer_query_kernel,
        out_shape=(jax.ShapeDtypeStruct((2 * PEER_HEADS, t, PEER_HALF), F32),
                   jax.ShapeDtypeStruct((t, D_MODEL), F32)),
        grid=(t // tm,),
        in_specs=[
            pl.BlockSpec((tm, D_MODEL), lambda i: (i, 0)),
            pl.BlockSpec((1, D_MODEL), lambda i: (0, 0)),
            pl.BlockSpec((None, 1, D_MODEL), lambda i: (i // tiles_per_mod, 0, 0)),
            pl.BlockSpec((None, 1, D_MODEL), lambda i: (i // tiles_per_mod, 0, 0)),
            pl.BlockSpec((D_MODEL, PEER_QW), lambda i: (0, 0)),
        ],
        out_specs=(pl.BlockSpec((2 * PEER_HEADS, tm, PEER_HALF), lambda i: (0, i, 0)),
                   pl.BlockSpec((tm, D_MODEL), lambda i: (i, 0))),
        compiler_params=_cparams(("parallel",)),
        name="peer_query",
    )(x2d, g, sc, sh, wq_bf16)


def _attn_kernel(sink_ref, q_ref, *rest, window, n_blocks):
    if window:
        kvp_ref, kvc_ref, kvn_ref, ctx_ref, o_ref = rest
    else:
        ctx_ref, o_ref = rest
    n = pl.program_id(1)
    q = q_ref[...] * (HEAD_DIM ** -0.5)
    sources = []
    if window:
        qi = lax.broadcasted_iota(jnp.int32, (ATTN_BLOCK, ATTN_BLOCK), 0)
        kj = lax.broadcasted_iota(jnp.int32, (ATTN_BLOCK, ATTN_BLOCK), 1)
        sources.append((kvp_ref[...], (kj >= qi) & (n > 0)))
        sources.append((kvc_ref[...], None))
        sources.append((kvn_ref[...], (kj <= qi) & (n < n_blocks - 1)))
    sources.append((ctx_ref[...], None))
    kb = [[kv[:, g * HEAD_DIM:(g + 1) * HEAD_DIM].astype(BF16) for g in range(N_KV_HEADS)] for kv, _ in sources]
    vb = [[kv[:, KV_W + g * HEAD_DIM:KV_W + (g + 1) * HEAD_DIM].astype(BF16) for g in range(N_KV_HEADS)]
          for kv, _ in sources]
    outs = []
    for h in range(N_HEADS):
        g = h // GQA_GROUP
        qh = q[:, h * HEAD_DIM:(h + 1) * HEAD_DIM].astype(BF16)
        sink = sink_ref[h]
        scores = []
        m = jnp.full((ATTN_BLOCK, 1), sink, F32)
        for si, (_, valid) in enumerate(sources):
            s = _dot_nt(qh, kb[si][g])
            if valid is not None:
                s = jnp.where(valid, s, -jnp.inf)
            scores.append(s)
            m = jnp.maximum(m, jnp.max(s, axis=-1, keepdims=True))
        denom = jnp.exp(sink - m)
        acc = jnp.zeros((ATTN_BLOCK, HEAD_DIM), F32)
        for si, s in enumerate(scores):
            p = jnp.exp(s - m)
            denom = denom + jnp.sum(p, axis=-1, keepdims=True)
            acc = acc + _dot(p.astype(BF16), vb[si][g])
        outs.append(acc / denom)
    o_ref[...] = jnp.concatenate(outs, axis=-1)


def _attention(qkv, kv_ctx, sink, window):
    b, l, _ = qkv.shape
    n_ctx = kv_ctx.shape[1]
    nb = l // ATTN_BLOCK
    kv_col = ATTN_W // (2 * KV_W)
    in_specs = [
        pl.BlockSpec(memory_space=pltpu.SMEM),
        pl.BlockSpec((None, ATTN_BLOCK, ATTN_W), lambda bi, n: (bi, n, 0)),
    ]
    args = [sink, qkv]
    if window:
        in_specs += [
            pl.BlockSpec((None, ATTN_BLOCK, 2 * KV_W), lambda bi, n: (bi, jnp.maximum(n - 1, 0), kv_col)),
            pl.BlockSpec((None, ATTN_BLOCK, 2 * KV_W), lambda bi, n: (bi, n, kv_col)),
            pl.BlockSpec((None, ATTN_BLOCK, 2 * KV_W), lambda bi, n: (bi, jnp.minimum(n + 1, nb - 1), kv_col)),
        ]
        args += [qkv, qkv, qkv]
    in_specs.append(pl.BlockSpec((None, n_ctx, 2 * KV_W), lambda bi, n: (bi, 0, kv_col)))
    args.append(kv_ctx)
    return pl.pallas_call(
        functools.partial(_attn_kernel, window=window, n_blocks=nb),
        out_shape=jax.ShapeDtypeStruct((b, l, ATTN_W), F32),
        grid=(b, nb),
        in_specs=in_specs,
        out_specs=pl.BlockSpec((None, ATTN_BLOCK, ATTN_W), lambda bi, n: (bi, n, 0)),
        compiler_params=_cparams(("parallel", "parallel")),
        name="attention_window" if window else "attention_context",
    )(*args)


def _gelu(x):
    return jax.nn.gelu(x, approximate=True)


def _sigmoid(x):
    return 1.0 / (1.0 + jnp.exp(-x))


def _mixer_kernel(rest_ref, hprev_ref, hnext_ref, ya_ref, x_ref, gt_ref, poolw_ref, pscale_ref, sguw_ref,
                  sgub_ref, wbr_ref, wout_ref, o_ref, ext_ref, *, tm, tiles_per_seq, seq_len):
    ti = pl.program_id(0) % tiles_per_seq
    z = rest_ref[:, 0:POOL_W]
    ext_ref[0:POOL_HALO, :] = jnp.where(ti > 0, hprev_ref[...], 0.0)
    ext_ref[POOL_HALO:POOL_HALO + tm, :] = z
    ext_ref[POOL_HALO + tm:2 * POOL_HALO + tm, :] = jnp.where(ti < tiles_per_seq - 1, hnext_ref[...], 0.0)
    pos = ti * tm + lax.broadcasted_iota(jnp.int32, (tm, LANES), 0)
    lane = lax.broadcasted_iota(jnp.int32, (tm, LANES), 1)

    def count(size):
        hi = jnp.minimum(pos + size // 2, seq_len)
        lo = jnp.maximum(pos - size // 2, 0)
        return (hi - lo).astype(F32)

    diffs = []
    for lt in range(POOL_W // LANES):
        cols = slice(lt * LANES, (lt + 1) * LANES)

        def shifted(d, cols=cols):
            return ext_ref[POOL_HALO + d:POOL_HALO + d + tm, cols]

        small, large = POOL_SIZES[2 * lt], POOL_SIZES[2 * lt + 1]
        s = shifted(-1) + shifted(0)
        width = 2
        sums = {}
        while width <= large:
            sums[width] = s
            half = width // 2
            if 2 * width <= large:
                for d in range(half, width):
                    s = s + shifted(-d - 1) + shifted(d)
            width *= 2
        mean = jnp.where(lane < POOL_GROUP_W, sums[small] / count(small), sums[large] / count(large))
        diffs.append(mean - z[:, cols])
    d = jnp.concatenate(diffs, axis=-1).astype(BF16)
    y_pool = _dot(d, poolw_ref[...]) * pscale_ref[...]

    u = _gelu(rest_ref[:, POOL_W:POOL_W + SGU_W])
    v = _gelu(rest_ref[:, POOL_W + SGU_W:POOL_W + 2 * SGU_W])
    vn = (v * lax.rsqrt(jnp.mean(v * v, axis=-1, keepdims=True) + EPS)).astype(BF16)
    group = lax.broadcasted_iota(jnp.int32, (SGU_CHUNK, SGU_W), 1) // SGU_GROUP_W
    mixed = []
    for c in range(tm // SGU_CHUNK):
        vc = vn[c * SGU_CHUNK:(c + 1) * SGU_CHUNK, :]
        mc = sgub_ref[...]
        for hg in range(SGU_GROUPS):
            mc = mc + jnp.where(group == hg, _dot(sguw_ref[hg], vc), 0.0)
        mixed.append(mc)
    y_sgu = u * jnp.concatenate(mixed, axis=0)

    g_off = POOL_W + 2 * SGU_W
    merged = _sigmoid(rest_ref[:, g_off:g_off + D_MODEL]) * _dot(ya_ref[...].astype(BF16), wbr_ref[0:ATTN_W, :])
    merged = merged + _sigmoid(rest_ref[:, g_off + D_MODEL:g_off + 2 * D_MODEL]) * _dot(
        y_pool.astype(BF16), wbr_ref[ATTN_W:ATTN_W + POOL_W, :])
    merged = merged + _sigmoid(rest_ref[:, g_off + 2 * D_MODEL:g_off + 3 * D_MODEL]) * _dot(
        y_sgu.astype(BF16), wbr_ref[ATTN_W + POOL_W:BR_W, :])
    o_ref[...] = x_ref[...] + gt_ref[...] * _dot(merged.astype(BF16), wout_ref[...])


def _mixer(rest, y_attn, x2d, gt, poolw_bd, pscale, sguw, sgub_full, wbr, wout, rows_per_mod, seq_len):
    t = x2d.shape[0]
    tm = min(TOKEN_TILE, seq_len)
    tiles_per_seq = seq_len // tm
    tiles_per_mod = rows_per_mod // tm
    halo_blocks = tm // POOL_HALO
    n_halo = t // POOL_HALO
    kern = functools.partial(_mixer_kernel, tm=tm, tiles_per_seq=tiles_per_seq, seq_len=seq_len)
    return pl.pallas_call(
        kern,
        out_shape=jax.ShapeDtypeStruct((t, D_MODEL), F32),
        grid=(t // tm,),
        in_specs=[
            pl.BlockSpec((tm, REST_W), lambda i: (i, 0)),
            pl.BlockSpec((POOL_HALO, POOL_W), lambda i: (jnp.maximum(i * halo_blocks - 1, 0), 0)),
            pl.BlockSpec((POOL_HALO, POOL_W), lambda i: (jnp.minimum((i + 1) * halo_blocks, n_halo - 1), 0)),
            pl.BlockSpec((tm, ATTN_W), lambda i: (i, 0)),
            pl.BlockSpec((tm, D_MODEL), lambda i: (i, 0)),
            pl.BlockSpec((None, 1, D_MODEL), lambda i: (i // tiles_per_mod, 0, 0)),
            pl.BlockSpec((POOL_W, POOL_W), lambda i: (0, 0)),
            pl.BlockSpec((1, POOL_W), lambda i: (0, 0)),
            pl.BlockSpec((SGU_GROUPS, SGU_CHUNK, SGU_CHUNK), lambda i: (0, 0, 0)),
            pl.BlockSpec((SGU_CHUNK, SGU_W), lambda i: (0, 0)),
            pl.BlockSpec((BR_W, D_MODEL), lambda i: (0, 0)),
            pl.BlockSpec((D_MODEL, D_MODEL), lambda i: (0, 0)),
        ],
        out_specs=pl.BlockSpec((tm, D_MODEL), lambda i: (i, 0)),
        scratch_shapes=[pltpu.VMEM((tm + 2 * POOL_HALO, POOL_W), F32)],
        compiler_params=_cparams(("parallel",)),
        name="mixer",
    )(rest, rest, rest, y_attn, x2d, gt, poolw_bd, pscale, sguw, sgub_full, wbr, wout)


def _top16_rows(problems, emit):
    scores = [p[0] for p in problems]
    for r in range(PEER_TOPK):
        for i, (_, ids) in enumerate(problems):
            s = scores[i]
            m = jnp.max(s, axis=0, keepdims=True)
            idx = jnp.min(jnp.where(s == m, ids, _NO_ROW), axis=0, keepdims=True)
            hit = ids == idx
            emit(i, r, m, hit, idx)
            scores[i] = jnp.where(hit, -jnp.inf, s)


def _pair_candidates(sv0, sv1, si0, si1, sub):
    subf = sub.astype(F32)
    sums, experts, flat = [], [], []
    for a in range(SUBLANES):
        n_b = PEER_TOPK // (a + 1)
        for b0 in range(0, n_b, SUBLANES):
            s = sv0[a:a + 1, :] + sv1[b0:b0 + SUBLANES, :]
            if n_b - b0 < SUBLANES:
                s = jnp.where(sub < n_b - b0, s, -jnp.inf)
            sums.append(s)
            experts.append(si0[a:a + 1, :] * PEER_KEYS + si1[b0:b0 + SUBLANES, :])
            flat.append(subf + float(a * PEER_TOPK + b0))
    assert PEER_TOPK // (SUBLANES + 1) == 1
    sums.append(sv0[SUBLANES:, :] + sv1[0:1, :])
    experts.append(si0[SUBLANES:, :] * PEER_KEYS + si1[0:1, :])
    flat.append((subf + float(SUBLANES)) * float(PEER_TOPK))
    return jnp.concatenate(sums, axis=0), jnp.concatenate(experts, axis=0), jnp.concatenate(flat, axis=0)


def _peer_route_kernel(q_ref, sk_ref, e_ref, g_ref, sv_ref, si_ref, ts_ref, te_ref, *, tm):
    key_id = lax.broadcasted_iota(jnp.int32, (PEER_KEYS, tm), 0).astype(F32)

    def sub_keys(it, carry):
        hps = [it * ROUTE_INTERLEAVE + u for u in range(ROUTE_INTERLEAVE)]
        problems = [(_dot_nt(sk_ref[hp], q_ref[hp].astype(BF16)), key_id) for hp in hps]

        def emit(i, r, m, hit, idx):
            sv_ref[hps[i], r:r + 1, :] = m
            si_ref[hps[i], r:r + 1, :] = idx.astype(jnp.int32)

        _top16_rows(problems, emit)
        return carry

    lax.fori_loop(0, 2 * PEER_HEADS // ROUTE_INTERLEAVE, sub_keys, 0)

    sub = lax.broadcasted_iota(jnp.int32, (SUBLANES, tm), 0)

    def heads(it, carry):
        hs = [it * ROUTE_INTERLEAVE + u for u in range(ROUTE_INTERLEAVE)]
        problems, experts = [], []
        for h in hs:
            cs, ce, cid = _pair_candidates(sv_ref[2 * h], sv_ref[2 * h + 1], si_ref[2 * h], si_ref[2 * h + 1], sub)
            problems.append((cs, cid))
            experts.append(ce)

        def emit(i, r, m, hit, idx):
            ts_ref[i, r:r + 1, :] = m
            te_ref[i, r:r + 1, :] = jnp.max(jnp.where(hit, experts[i], -1), axis=0, keepdims=True)

        _top16_rows(problems, emit)
        for i, h in enumerate(hs):
            row0 = pl.multiple_of(h * PEER_TOPK, PEER_TOPK)
            e_ref[pl.ds(row0, PEER_TOPK), :] = te_ref[i]
            ts = ts_ref[i]
            ex = jnp.exp(ts - jnp.max(ts, axis=0, keepdims=True))
            g_ref[pl.ds(row0, PEER_TOPK), :] = ex / jnp.sum(ex, axis=0, keepdims=True)
        return carry

    lax.fori_loop(0, PEER_HEADS // ROUTE_INTERLEAVE, heads, 0)


def _peer_route(q3, subkeys_bf16):
    t = q3.shape[1]
    tm = ROUTE_TILE
    nblk = t // tm
    return pl.pallas_call(
        functools.partial(_peer_route_kernel, tm=tm),
        out_shape=(jax.ShapeDtypeStruct((nblk, PEER_SLOTS, tm), jnp.int32),
                   jax.ShapeDtypeStruct((nblk, PEER_SLOTS, tm), F32)),
        grid=(nblk,),
        in_specs=[
            pl.BlockSpec((2 * PEER_HEADS, tm, PEER_HALF), lambda i: (0, i, 0)),
            pl.BlockSpec((2 * PEER_HEADS, PEER_KEYS, PEER_HALF), lambda i: (0, 0, 0)),
        ],
        out_specs=(pl.BlockSpec((None, PEER_SLOTS, tm), lambda i: (i, 0, 0)),
                   pl.BlockSpec((None, PEER_SLOTS, tm), lambda i: (i, 0, 0))),
        scratch_shapes=[
            pltpu.VMEM((2 * PEER_HEADS, PEER_TOPK, tm), F32),
            pltpu.VMEM((2 * PEER_HEADS, PEER_TOPK, tm), jnp.int32),
            pltpu.VMEM((ROUTE_INTERLEAVE, PEER_TOPK, tm), F32),
            pltpu.VMEM((ROUTE_INTERLEAVE, PEER_TOPK, tm), jnp.int32),
        ],
        compiler_params=_cparams(("parallel",)),
        name="peer_route",
    )(q3, subkeys_bf16)


def _peer_apply_kernel(idx_ref, gate_ref, h_ref, x_ref, gt_ref, uv_hbm, o_ref, buf_ref, sem_ref, *, tb):
    lookahead = GATHER_SLOTS - 1
    n_groups = tb // TOKEN_GROUP
    n_tiles = D_MODEL // LANES

    def start_gather(tok, slot):
        for k in range(PEER_SLOTS):
            pltpu.make_async_copy(uv_hbm.at[idx_ref[tok, k]], buf_ref.at[slot, k],
                                  sem_ref.at[slot]).start(priority=k % 2)

    def wait_gather(slot):
        pltpu.make_async_copy(uv_hbm.at[pl.ds(0, PEER_SLOTS)], buf_ref.at[slot], sem_ref.at[slot]).wait()

    for tok in range(lookahead):
        start_gather(tok, tok % GATHER_SLOTS)

    tok_lane = lax.broadcasted_iota(jnp.int32, (PEER_SLOTS, ROUTE_TILE), 1)
    sub = lax.broadcasted_iota(jnp.int32, (SUBLANES, LANES), 0)

    def fold(x0, x1, d):
        t0 = x0 + pltpu.roll(x0, SUBLANES - d, 0)
        t1 = x1 + pltpu.roll(x1, d, 0)
        return jnp.where((sub & d) == 0, t0, t1)

    def group(base, is_last):
        h8 = h_ref[pl.ds(base, TOKEN_GROUP), :]
        rows = []
        for j in range(TOKEN_GROUP):
            tok = base + j
            slot = j % GATHER_SLOTS
            if not is_last or j + lookahead < TOKEN_GROUP:
                start_gather(tok + lookahead, (j + lookahead) % GATHER_SLOTS)
            wait_gather(slot)
            h_tile = jnp.concatenate([h8[j:j + 1, s * LANES:(s + 1) * LANES] for s in range(n_tiles)], axis=0)
            partial = []
            for g in range(PEER_SLOTS // SUBLANES):
                p = [lax.bitcast_convert_type(buf_ref[slot, g * SUBLANES + c] & jnp.uint32(0xFFFF0000), F32) * h_tile
                     for c in range(SUBLANES)]
                y = [fold(p[c], p[c + 4], 4) for c in range(4)]
                z = [fold(y[c], y[c + 2], 2) for c in range(2)]
                partial.append(fold(z[0], z[1], 1))
            act = jnp.sum(jnp.concatenate(partial, axis=0), axis=1, keepdims=True)
            gates = gate_ref[tok // ROUTE_TILE]
            gate = jnp.sum(jnp.where(tok_lane == tok % ROUTE_TILE, gates, 0.0), axis=1, keepdims=True)
            w = gate * _gelu(act)
            accs = [jnp.zeros((SUBLANES, LANES), F32) for _ in range(4)]
            for k in range(PEER_SLOTS):
                v = lax.bitcast_convert_type(buf_ref[slot, k] << 16, F32)
                accs[k % 4] = accs[k % 4] + jnp.broadcast_to(w[k:k + 1, :], (SUBLANES, LANES)) * v
            acc = (accs[0] + accs[1]) + (accs[2] + accs[3])
            rows.append(jnp.concatenate([acc[s:s + 1, :] for s in range(n_tiles)], axis=1))
        out8 = jnp.concatenate(rows, axis=0)
        o_ref[pl.ds(base, TOKEN_GROUP), :] = x_ref[pl.ds(base, TOKEN_GROUP), :] + gt_ref[...] * out8

    def body(gi, carry):
        group(pl.multiple_of(gi * TOKEN_GROUP, TOKEN_GROUP), False)
        return carry

    lax.fori_loop(0, n_groups - 1, body, 0)
    group((n_groups - 1) * TOKEN_GROUP, True)


def _peer_apply(experts_tok, gates, h2d, x2d, gt, uv_table, rows_per_mod):
    t = x2d.shape[0]
    tb = min(APPLY_TILE, rows_per_mod)
    tiles_per_mod = rows_per_mod // tb
    return pl.pallas_call(
        functools.partial(_peer_apply_kernel, tb=tb),
        out_shape=jax.ShapeDtypeStruct((t, D_MODEL), F32),
        grid=(t // tb,),
        in_specs=[
            pl.BlockSpec((tb, PEER_SLOTS), lambda i: (i, 0), memory_space=pltpu.SMEM),
            pl.BlockSpec((tb // ROUTE_TILE, PEER_SLOTS, ROUTE_TILE), lambda i: (i, 0, 0)),
            pl.BlockSpec((tb, D_MODEL), lambda i: (i, 0)),
            pl.BlockSpec((tb, D_MODEL), lambda i: (i, 0)),
            pl.BlockSpec((None, 1, D_MODEL), lambda i: (i // tiles_per_mod, 0, 0)),
            pl.BlockSpec(memory_space=pl.ANY),
        ],
        out_specs=pl.BlockSpec((tb, D_MODEL), lambda i: (i, 0)),
        scratch_shapes=[
            pltpu.VMEM((GATHER_SLOTS, PEER_SLOTS, SUBLANES, LANES), jnp.uint32),
            pltpu.SemaphoreType.DMA((GATHER_SLOTS,)),
        ],
        compiler_params=_cparams(("arbitrary",), disable_bounds_checks=True),
        name="peer_apply",
    )(experts_tok, gates, h2d, x2d, gt, uv_table)


def _final_norm_kernel(x_ref, g_ref, o_ref):
    x = x_ref[...]
    o_ref[...] = (x * lax.rsqrt(jnp.mean(x * x, axis=-1, keepdims=True) + EPS)) * g_ref[...]


def _final_norm(x2d, g):
    t = x2d.shape[0]
    tm = min(2 * TOKEN_TILE, t)
    return pl.pallas_call(
        _final_norm_kernel,
        out_shape=jax.ShapeDtypeStruct((t, D_MODEL), F32),
        grid=(t // tm,),
        in_specs=[pl.BlockSpec((tm, D_MODEL), lambda i: (i, 0)), pl.BlockSpec((1, D_MODEL), lambda i: (0, 0))],
        out_specs=pl.BlockSpec((tm, D_MODEL), lambda i: (i, 0)),
        compiler_params=_cparams(("parallel",)),
        name="final_norm",
    )(x2d, g)


def _rope_tables(length):
    rows = length // GRID_W
    row = jnp.repeat(jnp.arange(rows, dtype=F32), GRID_W)
    col = jnp.tile(jnp.arange(GRID_W, dtype=F32), rows)
    inv = ROPE_BASE ** (-jnp.arange(ROPE_FREQS, dtype=F32) / ROPE_FREQS)
    ang = jnp.stack([row[:, None] * inv, col[:, None] * inv], axis=1)
    cos, sin = jnp.cos(ang), jnp.sin(ang)
    cos_h = jnp.broadcast_to(cos[:, :, None, :], (length, 2, 2, ROPE_FREQS)).reshape(length, HEAD_DIM)
    sin_h = jnp.stack([-sin, sin], axis=2).reshape(length, HEAD_DIM)
    reps = LANES // HEAD_DIM
    return jnp.tile(cos_h, (1, reps)), jnp.tile(sin_h, (1, reps))


def _pack_expert_tables(u_tab, v_tab):
    def bits(a):
        return lax.bitcast_convert_type(a.astype(BF16), jnp.uint16).astype(jnp.uint32)

    return ((bits(u_tab) << 16) | bits(v_tab)).reshape(-1, D_MODEL // LANES, LANES)


def _peer_layer(x2d, g, sc, sh, gt, rows_per_mod, wq, subkeys, uv_table):
    t = x2d.shape[0]
    q3, h2d = _peer_query(x2d, g, sc, sh, wq, rows_per_mod)
    experts, gates = _peer_route(q3, subkeys)
    experts_tok = jnp.transpose(experts, (0, 2, 1)).reshape(t, PEER_SLOTS)
    return _peer_apply(experts_tok, gates, h2d, x2d, gt, uv_table, rows_per_mod)


def kernel(x, c, ctx, c_ctx, w_mod, b_mod, g_mix, g_ffn, w_in, attn_sink, pool_w, pool_scale, sgu_w, sgu_b,
           w_br_attn, w_br_pool, w_br_sgu, w_out, peer_wq, peer_subkeys, peer_u, peer_v, g_final):
    b, l, d = x.shape
    n_ctx = ctx.shape[1]
    depth = w_mod.shape[0]
    assert d == D_MODEL and l % TOKEN_TILE == 0 and n_ctx % SGU_CHUNK == 0

    n_rows = -(-(b + 1) // SUBLANES) * SUBLANES
    cond = jnp.zeros((n_rows, d), F32).at[:b].set(c).at[b].set(c_ctx)
    mod = _adaln(cond, w_mod, b_mod).reshape(depth, n_rows, N_MOD, 1, d)

    rope = _rope_tables(l)
    x2d = x.reshape(b * l, d)
    xc2d = ctx.reshape(b * n_ctx, d)

    for layer in range(depth):
        last = layer == depth - 1
        m_lat = [mod[layer, :b, i] for i in range(N_MOD)]
        m_ctx = [mod[layer, b:b + 1, i] for i in range(N_MOD)]
        g1 = g_mix[layer].reshape(1, d)
        g2 = g_ffn[layer].reshape(1, d)

        w_in_b = w_in[layer].astype(BF16)
        qkv, rest = _in_proj(x2d, g1, m_lat[1], m_lat[0], w_in_b, l, rope, l)
        qkv_c, rest_c = _in_proj(xc2d, g1, m_ctx[1], m_ctx[0], w_in_b, b * n_ctx, None, n_ctx)

        y_attn = _attention(qkv.reshape(b, l, QKV_W), qkv_c.reshape(b, n_ctx, QKV_W), attn_sink[layer], True)

        poolw_bd = jnp.zeros((POOL_W, POOL_W), F32)
        for gi in range(len(POOL_SIZES)):
            sl = slice(gi * POOL_GROUP_W, (gi + 1) * POOL_GROUP_W)
            poolw_bd = poolw_bd.at[sl, sl].set(pool_w[layer, gi])
        poolw_bd = poolw_bd.astype(BF16)
        pscale = pool_scale[layer].reshape(1, POOL_W)
        sguw = sgu_w[layer].astype(BF16)
        sgub_full = jnp.repeat(sgu_b[layer].T, SGU_GROUP_W, axis=1)
        wbr = jnp.concatenate([w_br_attn[layer], w_br_pool[layer], w_br_sgu[layer]], axis=0).astype(BF16)
        wout = w_out[layer].astype(BF16)
        mix_w = (poolw_bd, pscale, sguw, sgub_full, wbr, wout)

        x2d = _mixer(rest, y_attn.reshape(b * l, ATTN_W), x2d, m_lat[2], *mix_w, l, l)
        if not last:
            y_attn_c = _attention(qkv_c.reshape(b, n_ctx, QKV_W), qkv_c.reshape(b, n_ctx, QKV_W),
                                  attn_sink[layer], False)
            xc2d = _mixer(rest_c, y_attn_c.reshape(b * n_ctx, ATTN_W), xc2d, m_ctx[2], *mix_w,
                          b * n_ctx, n_ctx)

        wq = peer_wq[layer].astype(BF16)
        subkeys = peer_subkeys[layer].reshape(2 * PEER_HEADS, PEER_KEYS, PEER_HALF).astype(BF16)
        uv_table = _pack_expert_tables(peer_u[layer], peer_v[layer])
        x2d = _peer_layer(x2d, g2, m_lat[4], m_lat[3], m_lat[5], l, wq, subkeys, uv_table)
        if not last:
            xc2d = _peer_layer(xc2d, g2, m_ctx[4], m_ctx[3], m_ctx[5], b * n_ctx, wq, subkeys, uv_table)

    return _final_norm(x2d, g_final.reshape(1, d)).reshape(b, l, d)
```

```python
import functools

import jax
import jax.numpy as jnp
from jax import lax
from jax.experimental import pallas as pl
from jax.experimental.pallas import tpu as pltpu

F32 = jnp.float32
BF16 = jnp.bfloat16

D_MODEL = 1024
EPS = 1e-6
N_MOD = 6
GRID_W = 64

N_HEADS = 8
N_KV_HEADS = 2
GQA_GROUP = N_HEADS // N_KV_HEADS
HEAD_DIM = 64
WINDOW = 128
ATTN_BLOCK = 128
ATTN_W = N_HEADS * HEAD_DIM
KV_W = N_KV_HEADS * HEAD_DIM
QKV_W = ATTN_W + 2 * KV_W
ROPE_BASE = 10000.0
ROPE_FREQS = HEAD_DIM // 4

POOL_SIZES = (2, 4, 8, 16)
POOL_GROUP_W = 64
POOL_W = len(POOL_SIZES) * POOL_GROUP_W
POOL_HALO = 8

SGU_CHUNK = 128
SGU_GROUPS = 4
SGU_W = 256
SGU_GROUP_W = SGU_W // SGU_GROUPS

N_BRANCH = 3
REST_W = POOL_W + 2 * SGU_W + N_BRANCH * D_MODEL
IN_W = QKV_W + REST_W
BR_W = ATTN_W + POOL_W + SGU_W

PEER_HEADS = 8
PEER_KEYS = 128
PEER_HALF = 128
PEER_TOPK = 16
PEER_SLOTS = PEER_HEADS * PEER_TOPK
PEER_QW = PEER_HEADS * 2 * PEER_HALF

LANES = 128
SUBLANES = 8
VMEM_LIMIT = 48 * 1024 * 1024

TOKEN_TILE = 256
ROUTE_TILE = 128
APPLY_TILE = 256
GATHER_SLOTS = 8
TOKEN_GROUP = 8
ISSUE_PER_GROUP = 3
ROUTE_INTERLEAVE = 2
_NO_ROW = 1e9


def _cparams(sem, **kw):
    return pltpu.CompilerParams(dimension_semantics=sem, vmem_limit_bytes=VMEM_LIMIT, **kw)


def _dot(a, b):
    return jnp.dot(a, b, preferred_element_type=F32)


def _dot_nt(a, b):
    return lax.dot_general(a, b, (((1,), (1,)), ((), ())), preferred_element_type=F32)


def _split_bf16(a):
    hi = a.astype(BF16)
    lo = (a - hi.astype(F32)).astype(BF16)
    return hi, lo


def _adaln_kernel(c_ref, w_ref, b_ref, o_ref):
    c = c_ref[...]
    s = c * (1.0 / (1.0 + jnp.exp(-c)))
    s_hi, s_lo = _split_bf16(s)
    w_hi, w_lo = _split_bf16(w_ref[...])
    acc = _dot(s_hi, w_hi) + (_dot(s_hi, w_lo) + _dot(s_lo, w_hi))
    o_ref[...] = acc + b_ref[...]


def _adaln(cond, w_mod, b_mod):
    depth = w_mod.shape[0]
    r = cond.shape[0]
    n = w_mod.shape[2]
    tn = D_MODEL
    return pl.pallas_call(
        _adaln_kernel,
        out_shape=jax.ShapeDtypeStruct((depth, r, n), F32),
        grid=(depth, n // tn),
        in_specs=[
            pl.BlockSpec((r, D_MODEL), lambda l, j: (0, 0)),
            pl.BlockSpec((None, D_MODEL, tn), lambda l, j: (l, 0, j)),
            pl.BlockSpec((None, 1, tn), lambda l, j: (l, 0, j)),
        ],
        out_specs=pl.BlockSpec((None, r, tn), lambda l, j: (l, 0, j)),
        compiler_params=_cparams(("arbitrary", "arbitrary")),
        name="adaln",
    )(cond, w_mod, b_mod.reshape(depth, 1, n))


def _modulated_norm(x, g, sc, sh):
    y = x * lax.rsqrt(jnp.mean(x * x, axis=-1, keepdims=True) + EPS)
    return (y * g) * (1.0 + sc) + sh


def _rope_tile(x, cos, sin_signed):
    lane = lax.broadcasted_iota(jnp.int32, x.shape, 1)
    first_half = (lane % (2 * ROPE_FREQS)) < ROPE_FREQS
    partner = jnp.where(first_half, pltpu.roll(x, LANES - ROPE_FREQS, 1), pltpu.roll(x, ROPE_FREQS, 1))
    return x * cos + partner * sin_signed


def _in_proj_kernel(x_ref, g_ref, sc_ref, sh_ref, w_ref, *rest, rope):
    if rope:
        cos_ref, sin_ref, qkv_ref, rest_ref = rest
    else:
        qkv_ref, rest_ref = rest
    hb = _modulated_norm(x_ref[...], g_ref[...], sc_ref[...], sh_ref[...]).astype(BF16)
    for c in range(QKV_W // LANES):
        cols = slice(c * LANES, (c + 1) * LANES)
        acc = _dot(hb, w_ref[:, cols])
        if rope and c * LANES < ATTN_W + KV_W:
            acc = _rope_tile(acc, cos_ref[...], sin_ref[...])
        qkv_ref[:, cols] = acc
    chunk = 768
    for c in range(REST_W // chunk):
        rest_ref[:, c * chunk:(c + 1) * chunk] = _dot(hb, w_ref[:, QKV_W + c * chunk:QKV_W + (c + 1) * chunk])


def _in_proj(x2d, g, sc, sh, w_bf16, rows_per_mod, rope_tables, seq_len):
    t = x2d.shape[0]
    tm = min(TOKEN_TILE, t)
    tiles_per_mod = rows_per_mod // tm
    rope = rope_tables is not None
    in_specs = [
        pl.BlockSpec((tm, D_MODEL), lambda i: (i, 0)),
        pl.BlockSpec((1, D_MODEL), lambda i: (0, 0)),
        pl.BlockSpec((None, 1, D_MODEL), lambda i: (i // tiles_per_mod, 0, 0)),
        pl.BlockSpec((None, 1, D_MODEL), lambda i: (i // tiles_per_mod, 0, 0)),
        pl.BlockSpec((D_MODEL, IN_W), lambda i: (0, 0)),
    ]
    args = [x2d, g, sc, sh, w_bf16]
    if rope:
        tiles_per_seq = seq_len // tm
        in_specs += [pl.BlockSpec((tm, LANES), lambda i: (i % tiles_per_seq, 0))] * 2
        args += list(rope_tables)
    return pl.pallas_call(
        functools.partial(_in_proj_kernel, rope=rope),
        out_shape=(jax.ShapeDtypeStruct((t, QKV_W), F32), jax.ShapeDtypeStruct((t, REST_W), F32)),
        grid=(t // tm,),
        in_specs=in_specs,
        out_specs=(pl.BlockSpec((tm, QKV_W), lambda i: (i, 0)), pl.BlockSpec((tm, REST_W), lambda i: (i, 0))),
        compiler_params=_cparams(("parallel",)),
        name="in_proj",
    )(*args)


def _peer_query_kernel(x_ref, g_ref, sc_ref, sh_ref, w_ref, q_ref, h_ref):
    h = _modulated_norm(x_ref[...], g_ref[...], sc_ref[...], sh_ref[...])
    h_ref[...] = h
    hb = h.astype(BF16)
    for hp in range(2 * PEER_HEADS):
        q_ref[hp] = _dot(hb, w_ref[:, hp * PEER_HALF:(hp + 1) * PEER_HALF])


def _peer_query(x2d, g, sc, sh, wq_bf16, rows_per_mod):
    t = x2d.shape[0]
    tm = min(TOKEN_TILE, t)
    tiles_per_mod = rows_per_mod // tm
    return pl.pallas_call(
        _peer_query_kernel,
        out_shape=(jax.ShapeDtypeStruct((2 * PEER_HEADS, t, PEER_HALF), F32),
                   jax.ShapeDtypeStruct((t, D_MODEL), F32)),
        grid=(t // tm,),
        in_specs=[
            pl.BlockSpec((tm, D_MODEL), lambda i: (i, 0)),
            pl.BlockSpec((1, D_MODEL), lambda i: (0, 0)),
            pl.BlockSpec((None, 1, D_MODEL), lambda i: (i // tiles_per_mod, 0, 0)),
            pl.BlockSpec((None, 1, D_MODEL), lambda i: (i // tiles_per_mod, 0, 0)),
            pl.BlockSpec((D_MODEL, PEER_QW), lambda i: (0, 0)),
        ],
        out_specs=(pl.BlockSpec((2 * PEER_HEADS, tm, PEER_HALF), lambda i: (0, i, 0)),
                   pl.BlockSpec((tm, D_MODEL), lambda i: (i, 0))),
        compiler_params=_cparams(("parallel",)),
        name="peer_query",
    )(x2d, g, sc, sh, wq_bf16)


def _attn_kernel(sink_ref, q_ref, *rest, window, n_blocks):
    if window:
        kvp_ref, kvc_ref, kvn_ref, ctx_ref, o_ref = rest
    else:
        ctx_ref, o_ref = rest
    n = pl.program_id(1)
    q = q_ref[...] * (HEAD_DIM ** -0.5)
    sources = []
    if window:
        qi = lax.broadcasted_iota(jnp.int32, (ATTN_BLOCK, ATTN_BLOCK), 0)
        kj = lax.broadcasted_iota(jnp.int32, (ATTN_BLOCK, ATTN_BLOCK), 1)
        sources.append((kvp_ref[...], (kj >= qi) & (n > 0)))
        sources.append((kvc_ref[...], None))
        sources.append((kvn_ref[...], (kj <= qi) & (n < n_blocks - 1)))
    sources.append((ctx_ref[...], None))
    kb = [[kv[:, g * HEAD_DIM:(g + 1) * HEAD_DIM].astype(BF16) for g in range(N_KV_HEADS)] for kv, _ in sources]
    vb = [[kv[:, KV_W + g * HEAD_DIM:KV_W + (g + 1) * HEAD_DIM].astype(BF16) for g in range(N_KV_HEADS)]
          for kv, _ in sources]
    outs = []
    for h in range(N_HEADS):
        g = h // GQA_GROUP
        qh = q[:, h * HEAD_DIM:(h + 1) * HEAD_DIM].astype(BF16)
        sink = sink_ref[h]
        scores = []
        m = jnp.full((ATTN_BLOCK, 1), sink, F32)
        for si, (_, valid) in enumerate(sources):
            s = _dot_nt(qh, kb[si][g])
            if valid is not None:
                s = jnp.where(valid, s, -jnp.inf)
            scores.append(s)
            m = jnp.maximum(m, jnp.max(s, axis=-1, keepdims=True))
        denom = jnp.exp(sink - m)
        acc = jnp.zeros((ATTN_BLOCK, HEAD_DIM), F32)
        for si, s in enumerate(scores):
            p = jnp.exp(s - m)
            denom = denom + jnp.sum(p, axis=-1, keepdims=True)
            acc = acc + _dot(p.astype(BF16), vb[si][g])
        outs.append(acc / denom)
    o_ref[...] = jnp.concatenate(outs, axis=-1)


def _attention(qkv, kv_ctx, sink, window):
    b, l, _ = qkv.shape
    n_ctx = kv_ctx.shape[1]
    nb = l // ATTN_BLOCK
    kv_col = ATTN_W // (2 * KV_W)
    in_specs = [
        pl.BlockSpec(memory_space=pltpu.SMEM),
        pl.BlockSpec((None, ATTN_BLOCK, ATTN_W), lambda bi, n: (bi, n, 0)),
    ]
    args = [sink, qkv]
    if window:
        in_specs += [
            pl.BlockSpec((None, ATTN_BLOCK, 2 * KV_W), lambda bi, n: (bi, jnp.maximum(n - 1, 0), kv_col)),
            pl.BlockSpec((None, ATTN_BLOCK, 2 * KV_W), lambda bi, n: (bi, n, kv_col)),
            pl.BlockSpec((None, ATTN_BLOCK, 2 * KV_W), lambda bi, n: (bi, jnp.minimum(n + 1, nb - 1), kv_col)),
        ]
        args += [qkv, qkv, qkv]
    in_specs.append(pl.BlockSpec((None, n_ctx, 2 * KV_W), lambda bi, n: (bi, 0, kv_col)))
    args.append(kv_ctx)
    return pl.pallas_call(
        functools.partial(_attn_kernel, window=window, n_blocks=nb),
        out_shape=jax.ShapeDtypeStruct((b, l, ATTN_W), F32),
        grid=(b, nb),
        in_specs=in_specs,
        out_specs=pl.BlockSpec((None, ATTN_BLOCK, ATTN_W), lambda bi, n: (bi, n, 0)),
        compiler_params=_cparams(("parallel", "parallel")),
        name="attention_window" if window else "attention_context",
    )(*args)


def _gelu(x):
    return jax.nn.gelu(x, approximate=True)


def _sigmoid(x):
    return 1.0 / (1.0 + jnp.exp(-x))


def _mixer_kernel(rest_ref, hprev_ref, hnext_ref, ya_ref, x_ref, gt_ref, poolw_ref, pscale_ref, sguw_ref,
                  sgub_ref, wbr_ref, wout_ref, o_ref, ext_ref, *, tm, tiles_per_seq, seq_len):
    ti = pl.program_id(0) % tiles_per_seq
    z = rest_ref[:, 0:POOL_W]
    ext_ref[0:POOL_HALO, :] = jnp.where(ti > 0, hprev_ref[...], 0.0)
    ext_ref[POOL_HALO:POOL_HALO + tm, :] = z
    ext_ref[POOL_HALO + tm:2 * POOL_HALO + tm, :] = jnp.where(ti < tiles_per_seq - 1, hnext_ref[...], 0.0)
    pos = ti * tm + lax.broadcasted_iota(jnp.int32, (tm, LANES), 0)
    lane = lax.broadcasted_iota(jnp.int32, (tm, LANES), 1)

    def count(size):
        hi = jnp.minimum(pos + size // 2, seq_len)
        lo = jnp.maximum(pos - size // 2, 0)
        return (hi - lo).astype(F32)

    diffs = []
    for lt in range(POOL_W // LANES):
        cols = slice(lt * LANES, (lt + 1) * LANES)

        def shifted(d, cols=cols):
            return ext_ref[POOL_HALO + d:POOL_HALO + d + tm, cols]

        small, large = POOL_SIZES[2 * lt], POOL_SIZES[2 * lt + 1]
        s = shifted(-1) + shifted(0)
        width = 2
        sums = {}
        while width <= large:
            sums[width] = s
            half = width // 2
            if 2 * width <= large:
                for d in range(half, width):
                    s = s + shifted(-d - 1) + shifted(d)
            width *= 2
        mean = jnp.where(lane < POOL_GROUP_W, sums[small] / count(small), sums[large] / count(large))
        diffs.append(mean - z[:, cols])
    d = jnp.concatenate(diffs, axis=-1).astype(BF16)
    y_pool = _dot(d, poolw_ref[...]) * pscale_ref[...]

    u = _gelu(rest_ref[:, POOL_W:POOL_W + SGU_W])
    v = _gelu(rest_ref[:, POOL_W + SGU_W:POOL_W + 2 * SGU_W])
    vn = (v * lax.rsqrt(jnp.mean(v * v, axis=-1, keepdims=True) + EPS)).astype(BF16)
    group = lax.broadcasted_iota(jnp.int32, (SGU_CHUNK, SGU_W), 1) // SGU_GROUP_W
    mixed = []
    for c in range(tm // SGU_CHUNK):
        vc = vn[c * SGU_CHUNK:(c + 1) * SGU_CHUNK, :]
        mc = sgub_ref[...]
        for hg in range(SGU_GROUPS):
            mc = mc + jnp.where(group == hg, _dot(sguw_ref[hg], vc), 0.0)
        mixed.append(mc)
    y_sgu = u * jnp.concatenate(mixed, axis=0)

    g_off = POOL_W + 2 * SGU_W
    merged = _sigmoid(rest_ref[:, g_off:g_off + D_MODEL]) * _dot(ya_ref[...].astype(BF16), wbr_ref[0:ATTN_W, :])
    merged = merged + _sigmoid(rest_ref[:, g_off + D_MODEL:g_off + 2 * D_MODEL]) * _dot(
        y_pool.astype(BF16), wbr_ref[ATTN_W:ATTN_W + POOL_W, :])
    merged = merged + _sigmoid(rest_ref[:, g_off + 2 * D_MODEL:g_off + 3 * D_MODEL]) * _dot(
        y_sgu.astype(BF16), wbr_ref[ATTN_W + POOL_W:BR_W, :])
    o_ref[...] = x_ref[...] + gt_ref[...] * _dot(merged.astype(BF16), wout_ref[...])


def _mixer(rest, y_attn, x2d, gt, poolw_bd, pscale, sguw, sgub_full, wbr, wout, rows_per_mod, seq_len):
    t = x2d.shape[0]
    tm = min(TOKEN_TILE, seq_len)
    tiles_per_seq = seq_len // tm
    tiles_per_mod = rows_per_mod // tm
    halo_blocks = tm // POOL_HALO
    n_halo = t // POOL_HALO
    kern = functools.partial(_mixer_kernel, tm=tm, tiles_per_seq=tiles_per_seq, seq_len=seq_len)
    return pl.pallas_call(
        kern,
        out_shape=jax.ShapeDtypeStruct((t, D_MODEL), F32),
        grid=(t // tm,),
        in_specs=[
            pl.BlockSpec((tm, REST_W), lambda i: (i, 0)),
            pl.BlockSpec((POOL_HALO, POOL_W), lambda i: (jnp.maximum(i * halo_blocks - 1, 0), 0)),
            pl.BlockSpec((POOL_HALO, POOL_W), lambda i: (jnp.minimum((i + 1) * halo_blocks, n_halo - 1), 0)),
            pl.BlockSpec((tm, ATTN_W), lambda i: (i, 0)),
            pl.BlockSpec((tm, D_MODEL), lambda i: (i, 0)),
            pl.BlockSpec((None, 1, D_MODEL), lambda i: (i // tiles_per_mod, 0, 0)),
            pl.BlockSpec((POOL_W, POOL_W), lambda i: (0, 0)),
            pl.BlockSpec((1, POOL_W), lambda i: (0, 0)),
            pl.BlockSpec((SGU_GROUPS, SGU_CHUNK, SGU_CHUNK), lambda i: (0, 0, 0)),
            pl.BlockSpec((SGU_CHUNK, SGU_W), lambda i: (0, 0)),
            pl.BlockSpec((BR_W, D_MODEL), lambda i: (0, 0)),
            pl.BlockSpec((D_MODEL, D_MODEL), lambda i: (0, 0)),
        ],
        out_specs=pl.BlockSpec((tm, D_MODEL), lambda i: (i, 0)),
        scratch_shapes=[pltpu.VMEM((tm + 2 * POOL_HALO, POOL_W), F32)],
        compiler_params=_cparams(("parallel",)),
        name="mixer",
    )(rest, rest, rest, y_attn, x2d, gt, poolw_bd, pscale, sguw, sgub_full, wbr, wout)


def _top16_rows(problems, emit):
    scores = [p[0] for p in problems]
    for r in range(PEER_TOPK):
        for i, (_, ids) in enumerate(problems):
            s = scores[i]
            m = jnp.max(s, axis=0, keepdims=True)
            idx = jnp.min(jnp.where(s == m, ids, _NO_ROW), axis=0, keepdims=True)
            hit = ids == idx
            emit(i, r, m, hit, idx)
            scores[i] = jnp.where(hit, -jnp.inf, s)


def _pair_candidates(sv0, sv1, si0, si1, sub):
    subf = sub.astype(F32)
    sums, experts, flat = [], [], []
    for a in range(SUBLANES):
        n_b = PEER_TOPK // (a + 1)
        for b0 in range(0, n_b, SUBLANES):
            s = sv0[a:a + 1, :] + sv1[b0:b0 + SUBLANES, :]
            if n_b - b0 < SUBLANES:
                s = jnp.where(sub < n_b - b0, s, -jnp.inf)
            sums.append(s)
            experts.append(si0[a:a + 1, :] * PEER_KEYS + si1[b0:b0 + SUBLANES, :])
            flat.append(subf + float(a * PEER_TOPK + b0))
    assert PEER_TOPK // (SUBLANES + 1) == 1
    sums.append(sv0[SUBLANES:, :] + sv1[0:1, :])
    experts.append(si0[SUBLANES:, :] * PEER_KEYS + si1[0:1, :])
    flat.append((subf + float(SUBLANES)) * float(PEER_TOPK))
    return jnp.concatenate(sums, axis=0), jnp.concatenate(experts, axis=0), jnp.concatenate(flat, axis=0)


def _peer_route_kernel(q_ref, sk_ref, e_ref, g_ref, sv_ref, si_ref, ts_ref, te_ref, *, tm):
    key_id = lax.broadcasted_iota(jnp.int32, (PEER_KEYS, tm), 0).astype(F32)

    def sub_keys(it, carry):
        hps = [it * ROUTE_INTERLEAVE + u for u in range(ROUTE_INTERLEAVE)]
        problems = [(_dot_nt(sk_ref[hp], q_ref[hp].astype(BF16)), key_id) for hp in hps]

        def emit(i, r, m, hit, idx):
            sv_ref[hps[i], r:r + 1, :] = m
            si_ref[hps[i], r:r + 1, :] = idx.astype(jnp.int32)

        _top16_rows(problems, emit)
        return carry

    lax.fori_loop(0, 2 * PEER_HEADS // ROUTE_INTERLEAVE, sub_keys, 0)

    sub = lax.broadcasted_iota(jnp.int32, (SUBLANES, tm), 0)

    def heads(it, carry):
        hs = [it * ROUTE_INTERLEAVE + u for u in range(ROUTE_INTERLEAVE)]
        problems, experts = [], []
        for h in hs:
            cs, ce, cid = _pair_candidates(sv_ref[2 * h], sv_ref[2 * h + 1], si_ref[2 * h], si_ref[2 * h + 1], sub)
            problems.append((cs, cid))
            experts.append(ce)

        def emit(i, r, m, hit, idx):
            ts_ref[i, r:r + 1, :] = m
            te_ref[i, r:r + 1, :] = jnp.max(jnp.where(hit, experts[i], -1), axis=0, keepdims=True)

        _top16_rows(problems, emit)
        for i, h in enumerate(hs):
            row0 = pl.multiple_of(h * PEER_TOPK, PEER_TOPK)
            e_ref[pl.ds(row0, PEER_TOPK), :] = te_ref[i]
            ts = ts_ref[i]
            ex = jnp.exp(ts - jnp.max(ts, axis=0, keepdims=True))
            g_ref[pl.ds(row0, PEER_TOPK), :] = ex / jnp.sum(ex, axis=0, keepdims=True)
        return carry

    lax.fori_loop(0, PEER_HEADS // ROUTE_INTERLEAVE, heads, 0)


def _peer_route(q3, subkeys_bf16):
    t = q3.shape[1]
    tm = ROUTE_TILE
    nblk = t // tm
    return pl.pallas_call(
        functools.partial(_peer_route_kernel, tm=tm),
        out_shape=(jax.ShapeDtypeStruct((nblk, PEER_SLOTS, tm), jnp.int32),
                   jax.ShapeDtypeStruct((nblk, PEER_SLOTS, tm), F32)),
        grid=(nblk,),
        in_specs=[
            pl.BlockSpec((2 * PEER_HEADS, tm, PEER_HALF), lambda i: (0, i, 0)),
            pl.BlockSpec((2 * PEER_HEADS, PEER_KEYS, PEER_HALF), lambda i: (0, 0, 0)),
        ],
        out_specs=(pl.BlockSpec((None, PEER_SLOTS, tm), lambda i: (i, 0, 0)),
                   pl.BlockSpec((None, PEER_SLOTS, tm), lambda i: (i, 0, 0))),
        scratch_shapes=[
            pltpu.VMEM((2 * PEER_HEADS, PEER_TOPK, tm), F32),
            pltpu.VMEM((2 * PEER_HEADS, PEER_TOPK, tm), jnp.int32),
            pltpu.VMEM((ROUTE_INTERLEAVE, PEER_TOPK, tm), F32),
            pltpu.VMEM((ROUTE_INTERLEAVE, PEER_TOPK, tm), jnp.int32),
        ],
        compiler_params=_cparams(("parallel",)),
        name="peer_route",
    )(q3, subkeys_bf16)


def _peer_apply_kernel(idx_ref, gate_ref, h_ref, x_ref, gt_ref, uv_hbm, o_ref, buf_ref, sem_ref, part_ref,
                       acc_ref, *, tb):
    lookahead = GATHER_SLOTS - 1
    n_groups = tb // TOKEN_GROUP
    n_tiles = D_MODEL // LANES
    n_slot_groups = PEER_SLOTS // SUBLANES

    def start_rows(tok, slot, k0, n):
        for k in range(k0, k0 + n):
            pltpu.make_async_copy(uv_hbm.at[idx_ref[tok, k]], buf_ref.at[slot, k],
                                  sem_ref.at[slot]).start(priority=k % 2)

    def wait_gather(slot):
        pltpu.make_async_copy(uv_hbm.at[pl.ds(0, PEER_SLOTS)], buf_ref.at[slot], sem_ref.at[slot]).wait()

    for tok in range(lookahead):
        start_rows(tok, tok % GATHER_SLOTS, 0, PEER_SLOTS)

    tok_lane = lax.broadcasted_iota(jnp.int32, (PEER_SLOTS, ROUTE_TILE), 1)
    sub = lax.broadcasted_iota(jnp.int32, (SUBLANES, LANES), 0)

    def fold(x0, x1, d):
        t0 = x0 + pltpu.roll(x0, SUBLANES - d, 0)
        t1 = x1 + pltpu.roll(x1, d, 0)
        return jnp.where((sub & d) == 0, t0, t1)

    def group(base, is_last):
        h8 = h_ref[pl.ds(base, TOKEN_GROUP), :]
        rows = []
        for j in range(TOKEN_GROUP):
            tok = base + j
            slot = j % GATHER_SLOTS
            prefetch = not is_last or j + lookahead < TOKEN_GROUP

            def start_ahead(k0, n, tok=tok, j=j, prefetch=prefetch):
                if prefetch:
                    start_rows(tok + lookahead, (j + lookahead) % GATHER_SLOTS, k0, n)

            wait_gather(slot)
            h_tile = jnp.concatenate([h8[j:j + 1, s * LANES:(s + 1) * LANES] for s in range(n_tiles)], axis=0)
            for g in range(n_slot_groups):
                p = [lax.bitcast_convert_type(buf_ref[slot, g * SUBLANES + c] & jnp.uint32(0xFFFF0000), F32) * h_tile
                     for c in range(SUBLANES)]
                y = [fold(p[c], p[c + 4], 4) for c in range(4)]
                z = [fold(y[c], y[c + 2], 2) for c in range(2)]
                part_ref[g * SUBLANES:(g + 1) * SUBLANES, :] = fold(z[0], z[1], 1)
                start_ahead(g * ISSUE_PER_GROUP, ISSUE_PER_GROUP)
            acc_ref[...] = jnp.zeros(acc_ref.shape, F32)
            start_ahead(n_slot_groups * ISSUE_PER_GROUP, PEER_SLOTS - 2 * n_slot_groups * ISSUE_PER_GROUP)
            act = jnp.sum(part_ref[...], axis=1, keepdims=True)
            gates = gate_ref[tok // ROUTE_TILE]
            gate = jnp.sum(jnp.where(tok_lane == tok % ROUTE_TILE, gates, 0.0), axis=1, keepdims=True)
            w = gate * _gelu(act)
            for g in range(n_slot_groups):
                terms = [jnp.broadcast_to(w[k:k + 1, :], (SUBLANES, LANES))
                         * lax.bitcast_convert_type(buf_ref[slot, k] << 16, F32)
                         for k in range(g * SUBLANES, (g + 1) * SUBLANES)]
                s4 = [terms[i] + terms[i + 4] for i in range(4)]
                acc_ref[g % 2] = acc_ref[g % 2] + ((s4[0] + s4[1]) + (s4[2] + s4[3]))
                start_ahead(PEER_SLOTS - (n_slot_groups - g) * ISSUE_PER_GROUP, ISSUE_PER_GROUP)
            acc = acc_ref[0] + acc_ref[1]
            rows.append(jnp.concatenate([acc[s:s + 1, :] for s in range(n_tiles)], axis=1))
        out8 = jnp.concatenate(rows, axis=0)
        o_ref[pl.ds(base, TOKEN_GROUP), :] = x_ref[pl.ds(base, TOKEN_GROUP), :] + gt_ref[...] * out8

    def body(gi, carry):
        group(pl.multiple_of(gi * TOKEN_GROUP, TOKEN_GROUP), False)
        return carry

    lax.fori_loop(0, n_groups - 1, body, 0)
    group((n_groups - 1) * TOKEN_GROUP, True)


def _peer_apply(experts_tok, gates, h2d, x2d, gt, uv_table, rows_per_mod):
    t = x2d.shape[0]
    tb = min(APPLY_TILE, rows_per_mod)
    tiles_per_mod = rows_per_mod // tb
    return pl.pallas_call(
        functools.partial(_peer_apply_kernel, tb=tb),
        out_shape=jax.ShapeDtypeStruct((t, D_MODEL), F32),
        grid=(t // tb,),
        in_specs=[
            pl.BlockSpec((tb, PEER_SLOTS), lambda i: (i, 0), memory_space=pltpu.SMEM),
            pl.BlockSpec((tb // ROUTE_TILE, PEER_SLOTS, ROUTE_TILE), lambda i: (i, 0, 0)),
            pl.BlockSpec((tb, D_MODEL), lambda i: (i, 0)),
            pl.BlockSpec((tb, D_MODEL), lambda i: (i, 0)),
            pl.BlockSpec((None, 1, D_MODEL), lambda i: (i // tiles_per_mod, 0, 0)),
            pl.BlockSpec(memory_space=pl.ANY),
        ],
        out_specs=pl.BlockSpec((tb, D_MODEL), lambda i: (i, 0)),
        scratch_shapes=[
            pltpu.VMEM((GATHER_SLOTS, PEER_SLOTS, SUBLANES, LANES), jnp.uint32),
            pltpu.SemaphoreType.DMA((GATHER_SLOTS,)),
            pltpu.VMEM((PEER_SLOTS, LANES), F32),
            pltpu.VMEM((2, SUBLANES, LANES), F32),
        ],
        compiler_params=_cparams(("arbitrary",), disable_bounds_checks=True),
        name="peer_apply",
    )(experts_tok, gates, h2d, x2d, gt, uv_table)


def _final_norm_kernel(x_ref, g_ref, o_ref):
    x = x_ref[...]
    o_ref[...] = (x * lax.rsqrt(jnp.mean(x * x, axis=-1, keepdims=True) + EPS)) * g_ref[...]


def _final_norm(x2d, g):
    t = x2d.shape[0]
    tm = min(2 * TOKEN_TILE, t)
    return pl.pallas_call(
        _final_norm_kernel,
        out_shape=jax.ShapeDtypeStruct((t, D_MODEL), F32),
        grid=(t // tm,),
        in_specs=[pl.BlockSpec((tm, D_MODEL), lambda i: (i, 0)), pl.BlockSpec((1, D_MODEL), lambda i: (0, 0))],
        out_specs=pl.BlockSpec((tm, D_MODEL), lambda i: (i, 0)),
        compiler_params=_cparams(("parallel",)),
        name="final_norm",
    )(x2d, g)


def _rope_tables(length):
    rows = length // GRID_W
    row = jnp.repeat(jnp.arange(rows, dtype=F32), GRID_W)
    col = jnp.tile(jnp.arange(GRID_W, dtype=F32), rows)
    inv = ROPE_BASE ** (-jnp.arange(ROPE_FREQS, dtype=F32) / ROPE_FREQS)
    ang = jnp.stack([row[:, None] * inv, col[:, None] * inv], axis=1)
    cos, sin = jnp.cos(ang), jnp.sin(ang)
    cos_h = jnp.broadcast_to(cos[:, :, None, :], (length, 2, 2, ROPE_FREQS)).reshape(length, HEAD_DIM)
    sin_h = jnp.stack([-sin, sin], axis=2).reshape(length, HEAD_DIM)
    reps = LANES // HEAD_DIM
    return jnp.tile(cos_h, (1, reps)), jnp.tile(sin_h, (1, reps))


def _pack_expert_tables(u_tab, v_tab):
    def bits(a):
        return lax.bitcast_convert_type(a.astype(BF16), jnp.uint16).astype(jnp.uint32)

    return ((bits(u_tab) << 16) | bits(v_tab)).reshape(-1, D_MODEL // LANES, LANES)


def _peer_layer(x2d, g, sc, sh, gt, rows_per_mod, wq, subkeys, uv_table):
    t = x2d.shape[0]
    q3, h2d = _peer_query(x2d, g, sc, sh, wq, rows_per_mod)
    experts, gates = _peer_route(q3, subkeys)
    experts_tok = jnp.transpose(experts, (0, 2, 1)).reshape(t, PEER_SLOTS)
    return _peer_apply(experts_tok, gates, h2d, x2d, gt, uv_table, rows_per_mod)


def kernel(x, c, ctx, c_ctx, w_mod, b_mod, g_mix, g_ffn, w_in, attn_sink, pool_w, pool_scale, sgu_w, sgu_b,
           w_br_attn, w_br_pool, w_br_sgu, w_out, peer_wq, peer_subkeys, peer_u, peer_v, g_final):
    b, l, d = x.shape
    n_ctx = ctx.shape[1]
    depth = w_mod.shape[0]
    assert d == D_MODEL and l % TOKEN_TILE == 0 and n_ctx % SGU_CHUNK == 0

    n_rows = -(-(b + 1) // SUBLANES) * SUBLANES
    cond = jnp.zeros((n_rows, d), F32).at[:b].set(c).at[b].set(c_ctx)
    mod = _adaln(cond, w_mod, b_mod).reshape(depth, n_rows, N_MOD, 1, d)

    rope = _rope_tables(l)
    x2d = x.reshape(b * l, d)
    xc2d = ctx.reshape(b * n_ctx, d)

    for layer in range(depth):
        last = layer == depth - 1
        m_lat = [mod[layer, :b, i] for i in range(N_MOD)]
        m_ctx = [mod[layer, b:b + 1, i] for i in range(N_MOD)]
        g1 = g_mix[layer].reshape(1, d)
        g2 = g_ffn[layer].reshape(1, d)

        w_in_b = w_in[layer].astype(BF16)
        qkv, rest = _in_proj(x2d, g1, m_lat[1], m_lat[0], w_in_b, l, rope, l)
        qkv_c, rest_c = _in_proj(xc2d, g1, m_ctx[1], m_ctx[0], w_in_b, b * n_ctx, None, n_ctx)

        y_attn = _attention(qkv.reshape(b, l, QKV_W), qkv_c.reshape(b, n_ctx, QKV_W), attn_sink[layer], True)

        poolw_bd = jnp.zeros((POOL_W, POOL_W), F32)
        for gi in range(len(POOL_SIZES)):
            sl = slice(gi * POOL_GROUP_W, (gi + 1) * POOL_GROUP_W)
            poolw_bd = poolw_bd.at[sl, sl].set(pool_w[layer, gi])
        poolw_bd = poolw_bd.astype(BF16)
        pscale = pool_scale[layer].reshape(1, POOL_W)
        sguw = sgu_w[layer].astype(BF16)
        sgub_full = jnp.repeat(sgu_b[layer].T, SGU_GROUP_W, axis=1)
        wbr = jnp.concatenate([w_br_attn[layer], w_br_pool[layer], w_br_sgu[layer]], axis=0).astype(BF16)
        wout = w_out[layer].astype(BF16)
        mix_w = (poolw_bd, pscale, sguw, sgub_full, wbr, wout)

        x2d = _mixer(rest, y_attn.reshape(b * l, ATTN_W), x2d, m_lat[2], *mix_w, l, l)
        if not last:
            y_attn_c = _attention(qkv_c.reshape(b, n_ctx, QKV_W), qkv_c.reshape(b, n_ctx, QKV_W),
                                  attn_sink[layer], False)
            xc2d = _mixer(rest_c, y_attn_c.reshape(b * n_ctx, ATTN_W), xc2d, m_ctx[2], *mix_w,
                          b * n_ctx, n_ctx)

        wq = peer_wq[layer].astype(BF16)
        subkeys = peer_subkeys[layer].reshape(2 * PEER_HEADS, PEER_KEYS, PEER_HALF).astype(BF16)
        uv_table = _pack_expert_tables(peer_u[layer], peer_v[layer])
        x2d = _peer_layer(x2d, g2, m_lat[4], m_lat[3], m_lat[5], l, wq, subkeys, uv_table)
        if not last:
            xc2d = _peer_layer(xc2d, g2, m_ctx[4], m_ctx[3], m_ctx[5], b * n_ctx, wq, subkeys, uv_table)

    return _final_norm(x2d, g_final.reshape(1, d)).reshape(b, l, d)
```

```python
import functools

import jax
import jax.numpy as jnp
from jax import lax
from jax.experimental import pallas as pl
from jax.experimental.pallas import tpu as pltpu

F32 = jnp.float32
BF16 = jnp.bfloat16

D_MODEL = 1024
EPS = 1e-6
N_MOD = 6
GRID_W = 64

N_HEADS = 8
N_KV_HEADS = 2
GQA_GROUP = N_HEADS // N_KV_HEADS
HEAD_DIM = 64
WINDOW = 128
ATTN_BLOCK = 128
ATTN_W = N_HEADS * HEAD_DIM
KV_W = N_KV_HEADS * HEAD_DIM
QKV_W = ATTN_W + 2 * KV_W
ROPE_BASE = 10000.0
ROPE_FREQS = HEAD_DIM // 4

POOL_SIZES = (2, 4, 8, 16)
POOL_GROUP_W = 64
POOL_W = len(POOL_SIZES) * POOL_GROUP_W
POOL_HALO = 8

SGU_CHUNK = 128
SGU_GROUPS = 4
SGU_W = 256
SGU_GROUP_W = SGU_W // SGU_GROUPS

N_BRANCH = 3
REST_W = POOL_W + 2 * SGU_W + N_BRANCH * D_MODEL
IN_W = QKV_W + REST_W
BR_W = ATTN_W + POOL_W + SGU_W

PEER_HEADS = 8
PEER_KEYS = 128
PEER_HALF = 128
PEER_TOPK = 16
PEER_SLOTS = PEER_HEADS * PEER_TOPK
PEER_QW = PEER_HEADS * 2 * PEER_HALF

LANES = 128
SUBLANES = 8
VMEM_LIMIT = 48 * 1024 * 1024

TOKEN_TILE = 256
ROUTE_TILE = 128
APPLY_TILE = 256
GATHER_SLOTS = 8
TOKEN_GROUP = 8
ISSUE_PER_GROUP = 3
ROUTE_INTERLEAVE = 2
_NO_ROW = 1e9


def _cparams(sem, **kw):
    return pltpu.CompilerParams(dimension_semantics=sem, vmem_limit_bytes=VMEM_LIMIT, **kw)


def _dot(a, b):
    return jnp.dot(a, b, preferred_element_type=F32)


def _dot_nt(a, b):
    return lax.dot_general(a, b, (((1,), (1,)), ((), ())), preferred_element_type=F32)


def _split_bf16(a):
    hi = a.astype(BF16)
    lo = (a - hi.astype(F32)).astype(BF16)
    return hi, lo


def _adaln_kernel(c_ref, w_ref, b_ref, o_ref):
    c = c_ref[...]
    s = c * (1.0 / (1.0 + jnp.exp(-c)))
    s_hi, s_lo = _split_bf16(s)
    w_hi, w_lo = _split_bf16(w_ref[...])
    acc = _dot(s_hi, w_hi) + (_dot(s_hi, w_lo) + _dot(s_lo, w_hi))
    o_ref[...] = acc + b_ref[...]


def _adaln(cond, w_mod, b_mod):
    depth = w_mod.shape[0]
    r = cond.shape[0]
    n = w_mod.shape[2]
    tn = D_MODEL
    return pl.pallas_call(
        _adaln_kernel,
        out_shape=jax.ShapeDtypeStruct((depth, r, n), F32),
        grid=(depth, n // tn),
        in_specs=[
            pl.BlockSpec((r, D_MODEL), lambda l, j: (0, 0)),
            pl.BlockSpec((None, D_MODEL, tn), lambda l, j: (l, 0, j)),
            pl.BlockSpec((None, 1, tn), lambda l, j: (l, 0, j)),
        ],
        out_specs=pl.BlockSpec((None, r, tn), lambda l, j: (l, 0, j)),
        compiler_params=_cparams(("arbitrary", "arbitrary")),
        name="adaln",
    )(cond, w_mod, b_mod.reshape(depth, 1, n))


def _modulated_norm(x, g, sc, sh):
    y = x * lax.rsqrt(jnp.mean(x * x, axis=-1, keepdims=True) + EPS)
    return (y * g) * (1.0 + sc) + sh


def _rope_tile(x, cos, sin_signed):
    lane = lax.broadcasted_iota(jnp.int32, x.shape, 1)
    first_half = (lane % (2 * ROPE_FREQS)) < ROPE_FREQS
    partner = jnp.where(first_half, pltpu.roll(x, LANES - ROPE_FREQS, 1), pltpu.roll(x, ROPE_FREQS, 1))
    return x * cos + partner * sin_signed


def _in_proj_kernel(x_ref, g_ref, sc_ref, sh_ref, w_ref, *rest, rope):
    if rope:
        cos_ref, sin_ref, qkv_ref, rest_ref = rest
    else:
        qkv_ref, rest_ref = rest
    hb = _modulated_norm(x_ref[...], g_ref[...], sc_ref[...], sh_ref[...]).astype(BF16)
    for c in range(QKV_W // LANES):
        cols = slice(c * LANES, (c + 1) * LANES)
        acc = _dot(hb, w_ref[:, cols])
        if rope and c * LANES < ATTN_W + KV_W:
            acc = _rope_tile(acc, cos_ref[...], sin_ref[...])
        qkv_ref[:, cols] = acc
    chunk = 768
    for c in range(REST_W // chunk):
        rest_ref[:, c * chunk:(c + 1) * chunk] = _dot(hb, w_ref[:, QKV_W + c * chunk:QKV_W + (c + 1) * chunk])


def _in_proj(x2d, g, sc, sh, w_bf16, rows_per_mod, rope_tables, seq_len):
    t = x2d.shape[0]
    tm = min(TOKEN_TILE, t)
    tiles_per_mod = rows_per_mod // tm
    rope = rope_tables is not None
    in_specs = [
        pl.BlockSpec((tm, D_MODEL), lambda i: (i, 0)),
        pl.BlockSpec((1, D_MODEL), lambda i: (0, 0)),
        pl.BlockSpec((None, 1, D_MODEL), lambda i: (i // tiles_per_mod, 0, 0)),
        pl.BlockSpec((None, 1, D_MODEL), lambda i: (i // tiles_per_mod, 0, 0)),
        pl.BlockSpec((D_MODEL, IN_W), lambda i: (0, 0)),
    ]
    args = [x2d, g, sc, sh, w_bf16]
    if rope:
        tiles_per_seq = seq_len // tm
        in_specs += [pl.BlockSpec((tm, LANES), lambda i: (i % tiles_per_seq, 0))] * 2
        args += list(rope_tables)
    return pl.pallas_call(
        functools.partial(_in_proj_kernel, rope=rope),
        out_shape=(jax.ShapeDtypeStruct((t, QKV_W), F32), jax.ShapeDtypeStruct((t, REST_W), F32)),
        grid=(t // tm,),
        in_specs=in_specs,
        out_specs=(pl.BlockSpec((tm, QKV_W), lambda i: (i, 0)), pl.BlockSpec((tm, REST_W), lambda i: (i, 0))),
        compiler_params=_cparams(("parallel",)),
        name="in_proj",
    )(*args)


def _peer_query_kernel(x_ref, g_ref, sc_ref, sh_ref, w_ref, q_ref, h_ref):
    h = _modulated_norm(x_ref[...], g_ref[...], sc_ref[...], sh_ref[...])
    h_ref[...] = h
    hb = h.astype(BF16)
    for hp in range(2 * PEER_HEADS):
        q_ref[hp] = _dot(hb, w_ref[:, hp * PEER_HALF:(hp + 1) * PEER_HALF])


def _peer_query(x2d, g, sc, sh, wq_bf16, rows_per_mod):
    t = x2d.shape[0]
    tm = min(TOKEN_TILE, t)
    tiles_per_mod = rows_per_mod // tm
    return pl.pallas_call(
        _peer_query_kernel,
        out_shape=(jax.ShapeDtypeStruct((2 * PEER_HEADS, t, PEER_HALF), F32),
                   jax.ShapeDtypeStruct((t, D_MODEL), F32)),
        grid=(t // tm,),
        in_specs=[
            pl.BlockSpec((tm, D_MODEL), lambda i: (i, 0)),
            pl.BlockSpec((1, D_MODEL), lambda i: (0, 0)),
            pl.BlockSpec((None, 1, D_MODEL), lambda i: (i // tiles_per_mod, 0, 0)),
            pl.BlockSpec((None, 1, D_MODEL), lambda i: (i // tiles_per_mod, 0, 0)),
            pl.BlockSpec((D_MODEL, PEER_QW), lambda i: (0, 0)),
        ],
        out_specs=(pl.BlockSpec((2 * PEER_HEADS, tm, PEER_HALF), lambda i: (0, i, 0)),
                   pl.BlockSpec((tm, D_MODEL), lambda i: (i, 0))),
        compiler_params=_cparams(("parallel",)),
        name="peer_query",
    )(x2d, g, sc, sh, wq_bf16)


def _attn_kernel(sink_ref, q_ref, *rest, window, n_blocks):
    if window:
        kvp_ref, kvc_ref, kvn_ref, ctx_ref, o_ref = rest
    else:
        ctx_ref, o_ref = rest
    n = pl.program_id(1)
    q = q_ref[...] * (HEAD_DIM ** -0.5)
    sources = []
    if window:
        qi = lax.broadcasted_iota(jnp.int32, (GQA_GROUP * ATTN_BLOCK, ATTN_BLOCK), 0) % ATTN_BLOCK
        kj = lax.broadcasted_iota(jnp.int32, (GQA_GROUP * ATTN_BLOCK, ATTN_BLOCK), 1)
        sources.append((kvp_ref[...], (kj >= qi) & (n > 0)))
        sources.append((kvc_ref[...], None))
        sources.append((kvn_ref[...], (kj <= qi) & (n < n_blocks - 1)))
    for c in range(ctx_ref.shape[0] // ATTN_BLOCK):
        sources.append((ctx_ref[c * ATTN_BLOCK:(c + 1) * ATTN_BLOCK, :], None))
    kb = [[kv[:, g * HEAD_DIM:(g + 1) * HEAD_DIM].astype(BF16) for g in range(N_KV_HEADS)] for kv, _ in sources]
    ones = jnp.ones((ATTN_BLOCK, HEAD_DIM), BF16)
    vb = [[jnp.concatenate([kv[:, KV_W + g * HEAD_DIM:KV_W + (g + 1) * HEAD_DIM].astype(BF16), ones], axis=1)
           for g in range(N_KV_HEADS)] for kv, _ in sources]
    outs = []
    for g in range(N_KV_HEADS):
        heads = range(g * GQA_GROUP, (g + 1) * GQA_GROUP)
        qg = jnp.concatenate([q[:, h * HEAD_DIM:(h + 1) * HEAD_DIM] for h in heads], axis=0).astype(BF16)
        sink = jnp.concatenate([jnp.full((ATTN_BLOCK, 1), sink_ref[h], F32) for h in heads], axis=0)
        scores = []
        for si, (_, valid) in enumerate(sources):
            s = _dot_nt(qg, kb[si][g])
            scores.append(s if valid is None else jnp.where(valid, s, -jnp.inf))
        tile_max = functools.reduce(jnp.maximum, scores)
        m = jnp.maximum(jnp.max(tile_max, axis=-1, keepdims=True), sink)
        acc = jnp.zeros((GQA_GROUP * ATTN_BLOCK, 2 * HEAD_DIM), F32)
        for si, s in enumerate(scores):
            acc = acc + _dot(jnp.exp(s - m).astype(BF16), vb[si][g])
        og = acc[:, :HEAD_DIM] / (acc[:, HEAD_DIM:] + jnp.exp(sink - m))
        outs += [og[i * ATTN_BLOCK:(i + 1) * ATTN_BLOCK, :] for i in range(GQA_GROUP)]
    o_ref[...] = jnp.concatenate(outs, axis=-1)


def _attention(qkv, kv_ctx, sink, window):
    b, l, _ = qkv.shape
    n_ctx = kv_ctx.shape[1]
    nb = l // ATTN_BLOCK
    kv_col = ATTN_W // (2 * KV_W)
    in_specs = [
        pl.BlockSpec(memory_space=pltpu.SMEM),
        pl.BlockSpec((None, ATTN_BLOCK, ATTN_W), lambda bi, n: (bi, n, 0)),
    ]
    args = [sink, qkv]
    if window:
        in_specs += [
            pl.BlockSpec((None, ATTN_BLOCK, 2 * KV_W), lambda bi, n: (bi, jnp.maximum(n - 1, 0), kv_col)),
            pl.BlockSpec((None, ATTN_BLOCK, 2 * KV_W), lambda bi, n: (bi, n, kv_col)),
            pl.BlockSpec((None, ATTN_BLOCK, 2 * KV_W), lambda bi, n: (bi, jnp.minimum(n + 1, nb - 1), kv_col)),
        ]
        args += [qkv, qkv, qkv]
    in_specs.append(pl.BlockSpec((None, n_ctx, 2 * KV_W), lambda bi, n: (bi, 0, kv_col)))
    args.append(kv_ctx)
    return pl.pallas_call(
        functools.partial(_attn_kernel, window=window, n_blocks=nb),
        out_shape=jax.ShapeDtypeStruct((b, l, ATTN_W), F32),
        grid=(b, nb),
        in_specs=in_specs,
        out_specs=pl.BlockSpec((None, ATTN_BLOCK, ATTN_W), lambda bi, n: (bi, n, 0)),
        compiler_params=_cparams(("parallel", "parallel")),
        name="attention_window" if window else "attention_context",
    )(*args)


def _gelu(x):
    return jax.nn.gelu(x, approximate=True)


def _sigmoid(x):
    return 1.0 / (1.0 + jnp.exp(-x))


def _mixer_kernel(rest_ref, hprev_ref, hnext_ref, ya_ref, x_ref, gt_ref, poolw_ref, pscale_ref, sguw_ref,
                  sgub_ref, wbr_ref, wout_ref, o_ref, ext_ref, *, tm, tiles_per_seq, seq_len):
    ti = pl.program_id(0) % tiles_per_seq
    z = rest_ref[:, 0:POOL_W]
    ext_ref[0:POOL_HALO, :] = jnp.where(ti > 0, hprev_ref[...], 0.0)
    ext_ref[POOL_HALO:POOL_HALO + tm, :] = z
    ext_ref[POOL_HALO + tm:2 * POOL_HALO + tm, :] = jnp.where(ti < tiles_per_seq - 1, hnext_ref[...], 0.0)
    pos = ti * tm + lax.broadcasted_iota(jnp.int32, (tm, LANES), 0)
    lane = lax.broadcasted_iota(jnp.int32, (tm, LANES), 1)

    def count(size):
        hi = jnp.minimum(pos + size // 2, seq_len)
        lo = jnp.maximum(pos - size // 2, 0)
        return (hi - lo).astype(F32)

    diffs = []
    for lt in range(POOL_W // LANES):
        cols = slice(lt * LANES, (lt + 1) * LANES)

        def shifted(d, cols=cols):
            return ext_ref[POOL_HALO + d:POOL_HALO + d + tm, cols]

        small, large = POOL_SIZES[2 * lt], POOL_SIZES[2 * lt + 1]
        s = shifted(-1) + shifted(0)
        width = 2
        sums = {}
        while width <= large:
            sums[width] = s
            half = width // 2
            if 2 * width <= large:
                for d in range(half, width):
                    s = s + shifted(-d - 1) + shifted(d)
            width *= 2
        mean = jnp.where(lane < POOL_GROUP_W, sums[small] / count(small), sums[large] / count(large))
        diffs.append(mean - z[:, cols])
    d = jnp.concatenate(diffs, axis=-1).astype(BF16)
    y_pool = _dot(d, poolw_ref[...]) * pscale_ref[...]

    u = _gelu(rest_ref[:, POOL_W:POOL_W + SGU_W])
    v = _gelu(rest_ref[:, POOL_W + SGU_W:POOL_W + 2 * SGU_W])
    vn = (v * lax.rsqrt(jnp.mean(v * v, axis=-1, keepdims=True) + EPS)).astype(BF16)
    group = lax.broadcasted_iota(jnp.int32, (SGU_CHUNK, SGU_W), 1) // SGU_GROUP_W
    mixed = []
    for c in range(tm // SGU_CHUNK):
        vc = vn[c * SGU_CHUNK:(c + 1) * SGU_CHUNK, :]
        mc = sgub_ref[...]
        for hg in range(SGU_GROUPS):
            mc = mc + jnp.where(group == hg, _dot(sguw_ref[hg], vc), 0.0)
        mixed.append(mc)
    y_sgu = u * jnp.concatenate(mixed, axis=0)

    g_off = POOL_W + 2 * SGU_W
    merged = _sigmoid(rest_ref[:, g_off:g_off + D_MODEL]) * _dot(ya_ref[...].astype(BF16), wbr_ref[0:ATTN_W, :])
    merged = merged + _sigmoid(rest_ref[:, g_off + D_MODEL:g_off + 2 * D_MODEL]) * _dot(
        y_pool.astype(BF16), wbr_ref[ATTN_W:ATTN_W + POOL_W, :])
    merged = merged + _sigmoid(rest_ref[:, g_off + 2 * D_MODEL:g_off + 3 * D_MODEL]) * _dot(
        y_sgu.astype(BF16), wbr_ref[ATTN_W + POOL_W:BR_W, :])
    o_ref[...] = x_ref[...] + gt_ref[...] * _dot(merged.astype(BF16), wout_ref[...])


def _mixer(rest, y_attn, x2d, gt, poolw_bd, pscale, sguw, sgub_full, wbr, wout, rows_per_mod, seq_len):
    t = x2d.shape[0]
    tm = min(TOKEN_TILE, seq_len)
    tiles_per_seq = seq_len // tm
    tiles_per_mod = rows_per_mod // tm
    halo_blocks = tm // POOL_HALO
    n_halo = t // POOL_HALO
    kern = functools.partial(_mixer_kernel, tm=tm, tiles_per_seq=tiles_per_seq, seq_len=seq_len)
    return pl.pallas_call(
        kern,
        out_shape=jax.ShapeDtypeStruct((t, D_MODEL), F32),
        grid=(t // tm,),
        in_specs=[
            pl.BlockSpec((tm, REST_W), lambda i: (i, 0)),
            pl.BlockSpec((POOL_HALO, POOL_W), lambda i: (jnp.maximum(i * halo_blocks - 1, 0), 0)),
            pl.BlockSpec((POOL_HALO, POOL_W), lambda i: (jnp.minimum((i + 1) * halo_blocks, n_halo - 1), 0)),
            pl.BlockSpec((tm, ATTN_W), lambda i: (i, 0)),
            pl.BlockSpec((tm, D_MODEL), lambda i: (i, 0)),
            pl.BlockSpec((None, 1, D_MODEL), lambda i: (i // tiles_per_mod, 0, 0)),
            pl.BlockSpec((POOL_W, POOL_W), lambda i: (0, 0)),
            pl.BlockSpec((1, POOL_W), lambda i: (0, 0)),
            pl.BlockSpec((SGU_GROUPS, SGU_CHUNK, SGU_CHUNK), lambda i: (0, 0, 0)),
            pl.BlockSpec((SGU_CHUNK, SGU_W), lambda i: (0, 0)),
            pl.BlockSpec((BR_W, D_MODEL), lambda i: (0, 0)),
            pl.BlockSpec((D_MODEL, D_MODEL), lambda i: (0, 0)),
        ],
        out_specs=pl.BlockSpec((tm, D_MODEL), lambda i: (i, 0)),
        scratch_shapes=[pltpu.VMEM((tm + 2 * POOL_HALO, POOL_W), F32)],
        compiler_params=_cparams(("parallel",)),
        name="mixer",
    )(rest, rest, rest, y_attn, x2d, gt, poolw_bd, pscale, sguw, sgub_full, wbr, wout)


def _top16_rows(problems, emit):
    scores = [p[0] for p in problems]
    for r in range(PEER_TOPK):
        for i, (_, ids) in enumerate(problems):
            s = scores[i]
            m = jnp.max(s, axis=0, keepdims=True)
            idx = jnp.min(jnp.where(s == m, ids, _NO_ROW), axis=0, keepdims=True)
            hit = ids == idx
            emit(i, r, m, hit, idx)
            scores[i] = jnp.where(hit, -jnp.inf, s)


def _pair_candidates(sv0, sv1, si0, si1, sub):
    subf = sub.astype(F32)
    sums, experts, flat = [], [], []
    for a in range(SUBLANES):
        n_b = PEER_TOPK // (a + 1)
        for b0 in range(0, n_b, SUBLANES):
            s = sv0[a:a + 1, :] + sv1[b0:b0 + SUBLANES, :]
            if n_b - b0 < SUBLANES:
                s = jnp.where(sub < n_b - b0, s, -jnp.inf)
            sums.append(s)
            experts.append(si0[a:a + 1, :] * PEER_KEYS + si1[b0:b0 + SUBLANES, :])
            flat.append(subf + float(a * PEER_TOPK + b0))
    assert PEER_TOPK // (SUBLANES + 1) == 1
    sums.append(sv0[SUBLANES:, :] + sv1[0:1, :])
    experts.append(si0[SUBLANES:, :] * PEER_KEYS + si1[0:1, :])
    flat.append((subf + float(SUBLANES)) * float(PEER_TOPK))
    return jnp.concatenate(sums, axis=0), jnp.concatenate(experts, axis=0), jnp.concatenate(flat, axis=0)


def _peer_route_kernel(q_ref, sk_ref, e_ref, g_ref, sv_ref, si_ref, ts_ref, te_ref, *, tm):
    key_id = lax.broadcasted_iota(jnp.int32, (PEER_KEYS, tm), 0).astype(F32)

    def sub_keys(it, carry):
        hps = [it * ROUTE_INTERLEAVE + u for u in range(ROUTE_INTERLEAVE)]
        problems = [(_dot_nt(sk_ref[hp], q_ref[hp].astype(BF16)), key_id) for hp in hps]

        def emit(i, r, m, hit, idx):
            sv_ref[hps[i], r:r + 1, :] = m
            si_ref[hps[i], r:r + 1, :] = idx.astype(jnp.int32)

        _top16_rows(problems, emit)
        return carry

    lax.fori_loop(0, 2 * PEER_HEADS // ROUTE_INTERLEAVE, sub_keys, 0)

    sub = lax.broadcasted_iota(jnp.int32, (SUBLANES, tm), 0)

    def heads(it, carry):
        hs = [it * ROUTE_INTERLEAVE + u for u in range(ROUTE_INTERLEAVE)]
        problems, experts = [], []
        for h in hs:
            cs, ce, cid = _pair_candidates(sv_ref[2 * h], sv_ref[2 * h + 1], si_ref[2 * h], si_ref[2 * h + 1], sub)
            problems.append((cs, cid))
            experts.append(ce)

        def emit(i, r, m, hit, idx):
            ts_ref[i, r:r + 1, :] = m
            te_ref[i, r:r + 1, :] = jnp.max(jnp.where(hit, experts[i], -1), axis=0, keepdims=True)

        _top16_rows(problems, emit)
        for i, h in enumerate(hs):
            row0 = pl.multiple_of(h * PEER_TOPK, PEER_TOPK)
            e_ref[pl.ds(row0, PEER_TOPK), :] = te_ref[i]
            ts = ts_ref[i]
            ex = jnp.exp(ts - jnp.max(ts, axis=0, keepdims=True))
            g_ref[pl.ds(row0, PEER_TOPK), :] = ex / jnp.sum(ex, axis=0, keepdims=True)
        return carry

    lax.fori_loop(0, PEER_HEADS // ROUTE_INTERLEAVE, heads, 0)


def _peer_route(q3, subkeys_bf16):
    t = q3.shape[1]
    tm = ROUTE_TILE
    nblk = t // tm
    return pl.pallas_call(
        functools.partial(_peer_route_kernel, tm=tm),
        out_shape=(jax.ShapeDtypeStruct((nblk, PEER_SLOTS, tm), jnp.int32),
                   jax.ShapeDtypeStruct((nblk, PEER_SLOTS, tm), F32)),
        grid=(nblk,),
        in_specs=[
            pl.BlockSpec((2 * PEER_HEADS, tm, PEER_HALF), lambda i: (0, i, 0)),
            pl.BlockSpec((2 * PEER_HEADS, PEER_KEYS, PEER_HALF), lambda i: (0, 0, 0)),
        ],
        out_specs=(pl.BlockSpec((None, PEER_SLOTS, tm), lambda i: (i, 0, 0)),
                   pl.BlockSpec((None, PEER_SLOTS, tm), lambda i: (i, 0, 0))),
        scratch_shapes=[
            pltpu.VMEM((2 * PEER_HEADS, PEER_TOPK, tm), F32),
            pltpu.VMEM((2 * PEER_HEADS, PEER_TOPK, tm), jnp.int32),
            pltpu.VMEM((ROUTE_INTERLEAVE, PEER_TOPK, tm), F32),
            pltpu.VMEM((ROUTE_INTERLEAVE, PEER_TOPK, tm), jnp.int32),
        ],
        compiler_params=_cparams(("parallel",)),
        name="peer_route",
    )(q3, subkeys_bf16)


def _peer_apply_kernel(idx_ref, gate_ref, h_ref, x_ref, gt_ref, uv_hbm, o_ref, buf_ref, sem_ref, part_ref,
                       acc_ref, *, tb):
    lookahead = GATHER_SLOTS - 1
    n_groups = tb // TOKEN_GROUP
    n_tiles = D_MODEL // LANES
    n_slot_groups = PEER_SLOTS // SUBLANES

    def start_rows(tok, slot, k0, n):
        for k in range(k0, k0 + n):
            pltpu.make_async_copy(uv_hbm.at[idx_ref[tok, k]], buf_ref.at[slot, k],
                                  sem_ref.at[slot]).start(priority=k % 2)

    def wait_gather(slot):
        pltpu.make_async_copy(uv_hbm.at[pl.ds(0, PEER_SLOTS)], buf_ref.at[slot], sem_ref.at[slot]).wait()

    for tok in range(lookahead):
        start_rows(tok, tok % GATHER_SLOTS, 0, PEER_SLOTS)

    tok_lane = lax.broadcasted_iota(jnp.int32, (PEER_SLOTS, ROUTE_TILE), 1)
    sub = lax.broadcasted_iota(jnp.int32, (SUBLANES, LANES), 0)

    def fold(x0, x1, d):
        t0 = x0 + pltpu.roll(x0, SUBLANES - d, 0)
        t1 = x1 + pltpu.roll(x1, d, 0)
        return jnp.where((sub & d) == 0, t0, t1)

    def group(base, is_last):
        h8 = h_ref[pl.ds(base, TOKEN_GROUP), :]
        rows = []
        for j in range(TOKEN_GROUP):
            tok = base + j
            slot = j % GATHER_SLOTS
            prefetch = not is_last or j + lookahead < TOKEN_GROUP

            def start_ahead(k0, n, tok=tok, j=j, prefetch=prefetch):
                if prefetch:
                    start_rows(tok + lookahead, (j + lookahead) % GATHER_SLOTS, k0, n)

            wait_gather(slot)
            h_tile = jnp.concatenate([h8[j:j + 1, s * LANES:(s + 1) * LANES] for s in range(n_tiles)], axis=0)
            for g in range(n_slot_groups):
                p = [lax.bitcast_convert_type(buf_ref[slot, g * SUBLANES + c] & jnp.uint32(0xFFFF0000), F32) * h_tile
                     for c in range(SUBLANES)]
                y = [fold(p[c], p[c + 4], 4) for c in range(4)]
                z = [fold(y[c], y[c + 2], 2) for c in range(2)]
                part_ref[g * SUBLANES:(g + 1) * SUBLANES, :] = fold(z[0], z[1], 1)
                start_ahead(g * ISSUE_PER_GROUP, ISSUE_PER_GROUP)
            acc_ref[...] = jnp.zeros(acc_ref.shape, F32)
            start_ahead(n_slot_groups * ISSUE_PER_GROUP, PEER_SLOTS - 2 * n_slot_groups * ISSUE_PER_GROUP)
            act = jnp.sum(part_ref[...], axis=1, keepdims=True)
            gates = gate_ref[tok // ROUTE_TILE]
            gate = jnp.sum(jnp.where(tok_lane == tok % ROUTE_TILE, gates, 0.0), axis=1, keepdims=True)
            w = gate * _gelu(act)
            for g in range(n_slot_groups):
                terms = [jnp.broadcast_to(w[k:k + 1, :], (SUBLANES, LANES))
                         * lax.bitcast_convert_type(buf_ref[slot, k] << 16, F32)
                         for k in range(g * SUBLANES, (g + 1) * SUBLANES)]
                s4 = [terms[i] + terms[i + 4] for i in range(4)]
                acc_ref[g % 2] = acc_ref[g % 2] + ((s4[0] + s4[1]) + (s4[2] + s4[3]))
                start_ahead(PEER_SLOTS - (n_slot_groups - g) * ISSUE_PER_GROUP, ISSUE_PER_GROUP)
            acc = acc_ref[0] + acc_ref[1]
            rows.append(jnp.concatenate([acc[s:s + 1, :] for s in range(n_tiles)], axis=1))
        out8 = jnp.concatenate(rows, axis=0)
        o_ref[pl.ds(base, TOKEN_GROUP), :] = x_ref[pl.ds(base, TOKEN_GROUP), :] + gt_ref[...] * out8

    def body(gi, carry):
        group(pl.multiple_of(gi * TOKEN_GROUP, TOKEN_GROUP), False)
        return carry

    lax.fori_loop(0, n_groups - 1, body, 0)
    group((n_groups - 1) * TOKEN_GROUP, True)


def _peer_apply(experts_tok, gates, h2d, x2d, gt, uv_table, rows_per_mod):
    t = x2d.shape[0]
    tb = min(APPLY_TILE, rows_per_mod)
    tiles_per_mod = rows_per_mod // tb
    return pl.pallas_call(
        functools.partial(_peer_apply_kernel, tb=tb),
        out_shape=jax.ShapeDtypeStruct((t, D_MODEL), F32),
        grid=(t // tb,),
        in_specs=[
            pl.BlockSpec((tb, PEER_SLOTS), lambda i: (i, 0), memory_space=pltpu.SMEM),
            pl.BlockSpec((tb // ROUTE_TILE, PEER_SLOTS, ROUTE_TILE), lambda i: (i, 0, 0)),
            pl.BlockSpec((tb, D_MODEL), lambda i: (i, 0)),
            pl.BlockSpec((tb, D_MODEL), lambda i: (i, 0)),
            pl.BlockSpec((None, 1, D_MODEL), lambda i: (i // tiles_per_mod, 0, 0)),
            pl.BlockSpec(memory_space=pl.ANY),
        ],
        out_specs=pl.BlockSpec((tb, D_MODEL), lambda i: (i, 0)),
        scratch_shapes=[
            pltpu.VMEM((GATHER_SLOTS, PEER_SLOTS, SUBLANES, LANES), jnp.uint32),
            pltpu.SemaphoreType.DMA((GATHER_SLOTS,)),
            pltpu.VMEM((PEER_SLOTS, LANES), F32),
            pltpu.VMEM((2, SUBLANES, LANES), F32),
        ],
        compiler_params=_cparams(("arbitrary",), disable_bounds_checks=True),
        name="peer_apply",
    )(experts_tok, gates, h2d, x2d, gt, uv_table)


def _final_norm_kernel(x_ref, g_ref, o_ref):
    x = x_ref[...]
    o_ref[...] = (x * lax.rsqrt(jnp.mean(x * x, axis=-1, keepdims=True) + EPS)) * g_ref[...]


def _final_norm(x2d, g):
    t = x2d.shape[0]
    tm = min(2 * TOKEN_TILE, t)
    return pl.pallas_call(
        _final_norm_kernel,
        out_shape=jax.ShapeDtypeStruct((t, D_MODEL), F32),
        grid=(t // tm,),
        in_specs=[pl.BlockSpec((tm, D_MODEL), lambda i: (i, 0)), pl.BlockSpec((1, D_MODEL), lambda i: (0, 0))],
        out_specs=pl.BlockSpec((tm, D_MODEL), lambda i: (i, 0)),
        compiler_params=_cparams(("parallel",)),
        name="final_norm",
    )(x2d, g)


def _rope_tables(length):
    rows = length // GRID_W
    row = jnp.repeat(jnp.arange(rows, dtype=F32), GRID_W)
    col = jnp.tile(jnp.arange(GRID_W, dtype=F32), rows)
    inv = ROPE_BASE ** (-jnp.arange(ROPE_FREQS, dtype=F32) / ROPE_FREQS)
    ang = jnp.stack([row[:, None] * inv, col[:, None] * inv], axis=1)
    cos, sin = jnp.cos(ang), jnp.sin(ang)
    cos_h = jnp.broadcast_to(cos[:, :, None, :], (length, 2, 2, ROPE_FREQS)).reshape(length, HEAD_DIM)
    sin_h = jnp.stack([-sin, sin], axis=2).reshape(length, HEAD_DIM)
    reps = LANES // HEAD_DIM
    return jnp.tile(cos_h, (1, reps)), jnp.tile(sin_h, (1, reps))


def _pack_expert_tables(u_tab, v_tab):
    def bits(a):
        return lax.bitcast_convert_type(a.astype(BF16), jnp.uint16).astype(jnp.uint32)

    return ((bits(u_tab) << 16) | bits(v_tab)).reshape(-1, D_MODEL // LANES, LANES)


def _peer_layer(x2d, g, sc, sh, gt, rows_per_mod, wq, subkeys, uv_table):
    t = x2d.shape[0]
    q3, h2d = _peer_query(x2d, g, sc, sh, wq, rows_per_mod)
    experts, gates = _peer_route(q3, subkeys)
    experts_tok = jnp.transpose(experts, (0, 2, 1)).reshape(t, PEER_SLOTS)
    return _peer_apply(experts_tok, gates, h2d, x2d, gt, uv_table, rows_per_mod)


def kernel(x, c, ctx, c_ctx, w_mod, b_mod, g_mix, g_ffn, w_in, attn_sink, pool_w, pool_scale, sgu_w, sgu_b,
           w_br_attn, w_br_pool, w_br_sgu, w_out, peer_wq, peer_subkeys, peer_u, peer_v, g_final):
    b, l, d = x.shape
    n_ctx = ctx.shape[1]
    depth = w_mod.shape[0]
    assert d == D_MODEL and l % TOKEN_TILE == 0 and n_ctx % SGU_CHUNK == 0

    n_rows = -(-(b + 1) // SUBLANES) * SUBLANES
    cond = jnp.zeros((n_rows, d), F32).at[:b].set(c).at[b].set(c_ctx)
    mod = _adaln(cond, w_mod, b_mod).reshape(depth, n_rows, N_MOD, 1, d)

    rope = _rope_tables(l)
    x2d = x.reshape(b * l, d)
    xc2d = ctx.reshape(b * n_ctx, d)

    for layer in range(depth):
        last = layer == depth - 1
        m_lat = [mod[layer, :b, i] for i in range(N_MOD)]
        m_ctx = [mod[layer, b:b + 1, i] for i in range(N_MOD)]
        g1 = g_mix[layer].reshape(1, d)
        g2 = g_ffn[layer].reshape(1, d)

        w_in_b = w_in[layer].astype(BF16)
        qkv, rest = _in_proj(x2d, g1, m_lat[1], m_lat[0], w_in_b, l, rope, l)
        qkv_c, rest_c = _in_proj(xc2d, g1, m_ctx[1], m_ctx[0], w_in_b, b * n_ctx, None, n_ctx)

        y_attn = _attention(qkv.reshape(b, l, QKV_W), qkv_c.reshape(b, n_ctx, QKV_W), attn_sink[layer], True)

        poolw_bd = jnp.zeros((POOL_W, POOL_W), F32)
        for gi in range(len(POOL_SIZES)):
            sl = slice(gi * POOL_GROUP_W, (gi + 1) * POOL_GROUP_W)
            poolw_bd = poolw_bd.at[sl, sl].set(pool_w[layer, gi])
        poolw_bd = poolw_bd.astype(BF16)
        pscale = pool_scale[layer].reshape(1, POOL_W)
        sguw = sgu_w[layer].astype(BF16)
        sgub_full = jnp.repeat(sgu_b[layer].T, SGU_GROUP_W, axis=1)
        wbr = jnp.concatenate([w_br_attn[layer], w_br_pool[layer], w_br_sgu[layer]], axis=0).astype(BF16)
        wout = w_out[layer].astype(BF16)
        mix_w = (poolw_bd, pscale, sguw, sgub_full, wbr, wout)

        x2d = _mixer(rest, y_attn.reshape(b * l, ATTN_W), x2d, m_lat[2], *mix_w, l, l)
        if not last:
            y_attn_c = _attention(qkv_c.reshape(b, n_ctx, QKV_W), qkv_c.reshape(b, n_ctx, QKV_W),
                                  attn_sink[layer], False)
            xc2d = _mixer(rest_c, y_attn_c.reshape(b * n_ctx, ATTN_W), xc2d, m_ctx[2], *mix_w,
                          b * n_ctx, n_ctx)

        wq = peer_wq[layer].astype(BF16)
        subkeys = peer_subkeys[layer].reshape(2 * PEER_HEADS, PEER_KEYS, PEER_HALF).astype(BF16)
        uv_table = _pack_expert_tables(peer_u[layer], peer_v[layer])
        x2d = _peer_layer(x2d, g2, m_lat[4], m_lat[3], m_lat[5], l, wq, subkeys, uv_table)
        if not last:
            xc2d = _peer_layer(xc2d, g2, m_ctx[4], m_ctx[3], m_ctx[5], b * n_ctx, wq, subkeys, uv_table)

    return _final_norm(x2d, g_final.reshape(1, d)).reshape(b, l, d)
```

```python
import functools

import jax
import jax.numpy as jnp
from jax import lax
from jax.experimental import pallas as pl
from jax.experimental.pallas import tpu as pltpu

F32 = jnp.float32
BF16 = jnp.bfloat16

D_MODEL = 1024
EPS = 1e-6
N_MOD = 6
GRID_W = 64

N_HEADS = 8
N_KV_HEADS = 2
GQA_GROUP = N_HEADS // N_KV_HEADS
HEAD_DIM = 64
WINDOW = 128
ATTN_BLOCK = 128
ATTN_W = N_HEADS * HEAD_DIM
KV_W = N_KV_HEADS * HEAD_DIM
QKV_W = ATTN_W + 2 * KV_W
ROPE_BASE = 10000.0
ROPE_FREQS = HEAD_DIM // 4

POOL_SIZES = (2, 4, 8, 16)
POOL_GROUP_W = 64
POOL_W = len(POOL_SIZES) * POOL_GROUP_W
POOL_HALO = 8

SGU_CHUNK = 128
SGU_GROUPS = 4
SGU_W = 256
SGU_GROUP_W = SGU_W // SGU_GROUPS

N_BRANCH = 3
REST_W = POOL_W + 2 * SGU_W + N_BRANCH * D_MODEL
IN_W = QKV_W + REST_W
BR_W = ATTN_W + POOL_W + SGU_W

PEER_HEADS = 8
PEER_KEYS = 128
PEER_HALF = 128
PEER_TOPK = 16
PEER_SLOTS = PEER_HEADS * PEER_TOPK
PEER_QW = PEER_HEADS * 2 * PEER_HALF

LANES = 128
SUBLANES = 8
VMEM_LIMIT = 48 * 1024 * 1024

TOKEN_TILE = 256
ROUTE_TILE = 128
APPLY_TILE = 256
GATHER_SLOTS = 8
TOKEN_GROUP = 8
ISSUE_PER_GROUP = 3
ROUTE_INTERLEAVE = 2
_NO_ROW = 1e9


def _cparams(sem, **kw):
    return pltpu.CompilerParams(dimension_semantics=sem, vmem_limit_bytes=VMEM_LIMIT, **kw)


def _dot(a, b):
    return jnp.dot(a, b, preferred_element_type=F32)


def _dot_nt(a, b):
    return lax.dot_general(a, b, (((1,), (1,)), ((), ())), preferred_element_type=F32)


def _split_bf16(a):
    hi = a.astype(BF16)
    lo = (a - hi.astype(F32)).astype(BF16)
    return hi, lo


def _adaln_kernel(c_ref, w_ref, b_ref, o_ref):
    c = c_ref[...]
    s = c * (1.0 / (1.0 + jnp.exp(-c)))
    s_hi, s_lo = _split_bf16(s)
    w_hi, w_lo = _split_bf16(w_ref[...])
    acc = _dot(s_hi, w_hi) + (_dot(s_hi, w_lo) + _dot(s_lo, w_hi))
    o_ref[...] = acc + b_ref[...]


def _adaln(cond, w_mod, b_mod):
    depth = w_mod.shape[0]
    r = cond.shape[0]
    n = w_mod.shape[2]
    tn = D_MODEL
    return pl.pallas_call(
        _adaln_kernel,
        out_shape=jax.ShapeDtypeStruct((depth, r, n), F32),
        grid=(depth, n // tn),
        in_specs=[
            pl.BlockSpec((r, D_MODEL), lambda l, j: (0, 0)),
            pl.BlockSpec((None, D_MODEL, tn), lambda l, j: (l, 0, j)),
            pl.BlockSpec((None, 1, tn), lambda l, j: (l, 0, j)),
        ],
        out_specs=pl.BlockSpec((None, r, tn), lambda l, j: (l, 0, j)),
        compiler_params=_cparams(("arbitrary", "arbitrary")),
        name="adaln",
    )(cond, w_mod, b_mod.reshape(depth, 1, n))


def _modulated_norm(x, g, sc, sh):
    y = x * lax.rsqrt(jnp.mean(x * x, axis=-1, keepdims=True) + EPS)
    return (y * g) * (1.0 + sc) + sh


def _rope_tile(x, cos, sin_signed):
    lane = lax.broadcasted_iota(jnp.int32, x.shape, 1)
    first_half = (lane % (2 * ROPE_FREQS)) < ROPE_FREQS
    partner = jnp.where(first_half, pltpu.roll(x, LANES - ROPE_FREQS, 1), pltpu.roll(x, ROPE_FREQS, 1))
    return x * cos + partner * sin_signed


def _in_proj_kernel(x_ref, g_ref, sc_ref, sh_ref, w_ref, *rest, rope):
    if rope:
        cos_ref, sin_ref, qkv_ref, rest_ref = rest
    else:
        qkv_ref, rest_ref = rest
    hb = _modulated_norm(x_ref[...], g_ref[...], sc_ref[...], sh_ref[...]).astype(BF16)
    for c in range(QKV_W // LANES):
        cols = slice(c * LANES, (c + 1) * LANES)
        acc = _dot(hb, w_ref[:, cols])
        if rope and c * LANES < ATTN_W + KV_W:
            acc = _rope_tile(acc, cos_ref[...], sin_ref[...])
        qkv_ref[:, cols] = acc
    chunk = 768
    for c in range(REST_W // chunk):
        rest_ref[:, c * chunk:(c + 1) * chunk] = _dot(hb, w_ref[:, QKV_W + c * chunk:QKV_W + (c + 1) * chunk])


def _in_proj(x2d, g, sc, sh, w_bf16, rows_per_mod, rope_tables, seq_len):
    t = x2d.shape[0]
    tm = min(TOKEN_TILE, t)
    tiles_per_mod = rows_per_mod // tm
    rope = rope_tables is not None
    in_specs = [
        pl.BlockSpec((tm, D_MODEL), lambda i: (i, 0)),
        pl.BlockSpec((1, D_MODEL), lambda i: (0, 0)),
        pl.BlockSpec((None, 1, D_MODEL), lambda i: (i // tiles_per_mod, 0, 0)),
        pl.BlockSpec((None, 1, D_MODEL), lambda i: (i // tiles_per_mod, 0, 0)),
        pl.BlockSpec((D_MODEL, IN_W), lambda i: (0, 0)),
    ]
    args = [x2d, g, sc, sh, w_bf16]
    if rope:
        tiles_per_seq = seq_len // tm
        in_specs += [pl.BlockSpec((tm, LANES), lambda i: (i % tiles_per_seq, 0))] * 2
        args += list(rope_tables)
    return pl.pallas_call(
        functools.partial(_in_proj_kernel, rope=rope),
        out_shape=(jax.ShapeDtypeStruct((t, QKV_W), F32), jax.ShapeDtypeStruct((t, REST_W), F32)),
        grid=(t // tm,),
        in_specs=in_specs,
        out_specs=(pl.BlockSpec((tm, QKV_W), lambda i: (i, 0)), pl.BlockSpec((tm, REST_W), lambda i: (i, 0))),
        compiler_params=_cparams(("parallel",)),
        name="in_proj",
    )(*args)


def _peer_query_kernel(x_ref, g_ref, sc_ref, sh_ref, w_ref, q_ref, h_ref):
    h = _modulated_norm(x_ref[...], g_ref[...], sc_ref[...], sh_ref[...])
    h_ref[...] = h
    hb = h.astype(BF16)
    for hp in range(2 * PEER_HEADS):
        q_ref[hp] = _dot(hb, w_ref[:, hp * PEER_HALF:(hp + 1) * PEER_HALF])


def _peer_query(x2d, g, sc, sh, wq_bf16, rows_per_mod):
    t = x2d.shape[0]
    tm = min(TOKEN_TILE, t)
    tiles_per_mod = rows_per_mod // tm
    return pl.pallas_call(
        _peer_query_kernel,
        out_shape=(jax.ShapeDtypeStruct((2 * PEER_HEADS, t, PEER_HALF), F32),
                   jax.ShapeDtypeStruct((t, D_MODEL), F32)),
        grid=(t // tm,),
        in_specs=[
            pl.BlockSpec((tm, D_MODEL), lambda i: (i, 0)),
            pl.BlockSpec((1, D_MODEL), lambda i: (0, 0)),
            pl.BlockSpec((None, 1, D_MODEL), lambda i: (i // tiles_per_mod, 0, 0)),
            pl.BlockSpec((None, 1, D_MODEL), lambda i: (i // tiles_per_mod, 0, 0)),
            pl.BlockSpec((D_MODEL, PEER_QW), lambda i: (0, 0)),
        ],
        out_specs=(pl.BlockSpec((2 * PEER_HEADS, tm, PEER_HALF), lambda i: (0, i, 0)),
                   pl.BlockSpec((tm, D_MODEL), lambda i: (i, 0))),
        compiler_params=_cparams(("parallel",)),
        name="peer_query",
    )(x2d, g, sc, sh, wq_bf16)


def _attn_kernel(sink_ref, q_ref, *rest, window, n_blocks):
    if window:
        kvp_ref, kvc_ref, kvn_ref, ctx_ref, o_ref = rest
    else:
        ctx_ref, o_ref = rest
    n = pl.program_id(1)
    q = q_ref[...] * (HEAD_DIM ** -0.5)
    sources = []
    if window:
        qi = lax.broadcasted_iota(jnp.int32, (GQA_GROUP * ATTN_BLOCK, ATTN_BLOCK), 0) % ATTN_BLOCK
        kj = lax.broadcasted_iota(jnp.int32, (GQA_GROUP * ATTN_BLOCK, ATTN_BLOCK), 1)
        sources.append((kvp_ref[...], (kj >= qi) & (n > 0)))
        sources.append((kvc_ref[...], None))
        sources.append((kvn_ref[...], (kj <= qi) & (n < n_blocks - 1)))
    for c in range(ctx_ref.shape[0] // ATTN_BLOCK):
        sources.append((ctx_ref[c * ATTN_BLOCK:(c + 1) * ATTN_BLOCK, :], None))
    kb = [[kv[:, g * HEAD_DIM:(g + 1) * HEAD_DIM].astype(BF16) for g in range(N_KV_HEADS)] for kv, _ in sources]
    ones = jnp.ones((ATTN_BLOCK, HEAD_DIM), BF16)
    vb = [[jnp.concatenate([kv[:, KV_W + g * HEAD_DIM:KV_W + (g + 1) * HEAD_DIM].astype(BF16), ones], axis=1)
           for g in range(N_KV_HEADS)] for kv, _ in sources]
    outs = []
    for g in range(N_KV_HEADS):
        heads = range(g * GQA_GROUP, (g + 1) * GQA_GROUP)
        qg = jnp.concatenate([q[:, h * HEAD_DIM:(h + 1) * HEAD_DIM] for h in heads], axis=0).astype(BF16)
        sink = jnp.concatenate([jnp.full((ATTN_BLOCK, 1), sink_ref[h], F32) for h in heads], axis=0)
        scores = []
        for si, (_, valid) in enumerate(sources):
            s = _dot_nt(qg, kb[si][g])
            scores.append(s if valid is None else jnp.where(valid, s, -jnp.inf))
        tile_max = functools.reduce(jnp.maximum, scores)
        m = jnp.maximum(jnp.max(tile_max, axis=-1, keepdims=True), sink)
        acc = jnp.zeros((GQA_GROUP * ATTN_BLOCK, 2 * HEAD_DIM), F32)
        for si, s in enumerate(scores):
            acc = acc + _dot(jnp.exp(s - m).astype(BF16), vb[si][g])
        og = acc[:, :HEAD_DIM] / (acc[:, HEAD_DIM:] + jnp.exp(sink - m))
        outs += [og[i * ATTN_BLOCK:(i + 1) * ATTN_BLOCK, :] for i in range(GQA_GROUP)]
    o_ref[...] = jnp.concatenate(outs, axis=-1)


def _attention(qkv, kv_ctx, sink, window):
    b, l, _ = qkv.shape
    n_ctx = kv_ctx.shape[1]
    nb = l // ATTN_BLOCK
    kv_col = ATTN_W // (2 * KV_W)
    in_specs = [
        pl.BlockSpec(memory_space=pltpu.SMEM),
        pl.BlockSpec((None, ATTN_BLOCK, ATTN_W), lambda bi, n: (bi, n, 0)),
    ]
    args = [sink, qkv]
    if window:
        in_specs += [
            pl.BlockSpec((None, ATTN_BLOCK, 2 * KV_W), lambda bi, n: (bi, jnp.maximum(n - 1, 0), kv_col)),
            pl.BlockSpec((None, ATTN_BLOCK, 2 * KV_W), lambda bi, n: (bi, n, kv_col)),
            pl.BlockSpec((None, ATTN_BLOCK, 2 * KV_W), lambda bi, n: (bi, jnp.minimum(n + 1, nb - 1), kv_col)),
        ]
        args += [qkv, qkv, qkv]
    in_specs.append(pl.BlockSpec((None, n_ctx, 2 * KV_W), lambda bi, n: (bi, 0, kv_col)))
    args.append(kv_ctx)
    return pl.pallas_call(
        functools.partial(_attn_kernel, window=window, n_blocks=nb),
        out_shape=jax.ShapeDtypeStruct((b, l, ATTN_W), F32),
        grid=(b, nb),
        in_specs=in_specs,
        out_specs=pl.BlockSpec((None, ATTN_BLOCK, ATTN_W), lambda bi, n: (bi, n, 0)),
        compiler_params=_cparams(("parallel", "parallel")),
        name="attention_window" if window else "attention_context",
    )(*args)


def _gelu(x):
    return jax.nn.gelu(x, approximate=True)


def _sigmoid(x):
    return 1.0 / (1.0 + jnp.exp(-x))


def _mixer_kernel(rest_ref, hprev_ref, hnext_ref, ya_ref, x_ref, gt_ref, poolw_ref, pscale_ref, sguw_ref,
                  sgub_ref, wbr_ref, wout_ref, o_ref, ext_ref, *, tm, tiles_per_seq, seq_len):
    ti = pl.program_id(0) % tiles_per_seq
    z = rest_ref[:, 0:POOL_W]
    ext_ref[0:POOL_HALO, :] = jnp.where(ti > 0, hprev_ref[...], 0.0)
    ext_ref[POOL_HALO:POOL_HALO + tm, :] = z
    ext_ref[POOL_HALO + tm:2 * POOL_HALO + tm, :] = jnp.where(ti < tiles_per_seq - 1, hnext_ref[...], 0.0)
    pos = ti * tm + lax.broadcasted_iota(jnp.int32, (tm, LANES), 0)
    lane = lax.broadcasted_iota(jnp.int32, (tm, LANES), 1)

    def count(size):
        hi = jnp.minimum(pos + size // 2, seq_len)
        lo = jnp.maximum(pos - size // 2, 0)
        return (hi - lo).astype(F32)

    diffs = []
    for lt in range(POOL_W // LANES):
        cols = slice(lt * LANES, (lt + 1) * LANES)

        def shifted(d, cols=cols):
            return ext_ref[POOL_HALO + d:POOL_HALO + d + tm, cols]

        small, large = POOL_SIZES[2 * lt], POOL_SIZES[2 * lt + 1]
        s = shifted(-1) + shifted(0)
        width = 2
        sums = {}
        while width <= large:
            sums[width] = s
            half = width // 2
            if 2 * width <= large:
                for d in range(half, width):
                    s = s + shifted(-d - 1) + shifted(d)
            width *= 2
        mean = jnp.where(lane < POOL_GROUP_W, sums[small] / count(small), sums[large] / count(large))
        diffs.append(mean - z[:, cols])
    d = jnp.concatenate(diffs, axis=-1).astype(BF16)
    y_pool = _dot(d, poolw_ref[...]) * pscale_ref[...]

    u = _gelu(rest_ref[:, POOL_W:POOL_W + SGU_W])
    v = _gelu(rest_ref[:, POOL_W + SGU_W:POOL_W + 2 * SGU_W])
    vn = (v * lax.rsqrt(jnp.mean(v * v, axis=-1, keepdims=True) + EPS)).astype(BF16)
    group = lax.broadcasted_iota(jnp.int32, (SGU_CHUNK, SGU_W), 1) // SGU_GROUP_W
    mixed = []
    for c in range(tm // SGU_CHUNK):
        vc = vn[c * SGU_CHUNK:(c + 1) * SGU_CHUNK, :]
        mc = sgub_ref[...]
        for hg in range(SGU_GROUPS):
            mc = mc + jnp.where(group == hg, _dot(sguw_ref[hg], vc), 0.0)
        mixed.append(mc)
    y_sgu = u * jnp.concatenate(mixed, axis=0)

    g_off = POOL_W + 2 * SGU_W
    merged = _sigmoid(rest_ref[:, g_off:g_off + D_MODEL]) * _dot(ya_ref[...].astype(BF16), wbr_ref[0:ATTN_W, :])
    merged = merged + _sigmoid(rest_ref[:, g_off + D_MODEL:g_off + 2 * D_MODEL]) * _dot(
        y_pool.astype(BF16), wbr_ref[ATTN_W:ATTN_W + POOL_W, :])
    merged = merged + _sigmoid(rest_ref[:, g_off + 2 * D_MODEL:g_off + 3 * D_MODEL]) * _dot(
        y_sgu.astype(BF16), wbr_ref[ATTN_W + POOL_W:BR_W, :])
    o_ref[...] = x_ref[...] + gt_ref[...] * _dot(merged.astype(BF16), wout_ref[...])


def _mixer(rest, y_attn, x2d, gt, poolw_bd, pscale, sguw, sgub_full, wbr, wout, rows_per_mod, seq_len):
    t = x2d.shape[0]
    tm = min(TOKEN_TILE, seq_len)
    tiles_per_seq = seq_len // tm
    tiles_per_mod = rows_per_mod // tm
    halo_blocks = tm // POOL_HALO
    n_halo = t // POOL_HALO
    kern = functools.partial(_mixer_kernel, tm=tm, tiles_per_seq=tiles_per_seq, seq_len=seq_len)
    return pl.pallas_call(
        kern,
        out_shape=jax.ShapeDtypeStruct((t, D_MODEL), F32),
        grid=(t // tm,),
        in_specs=[
            pl.BlockSpec((tm, REST_W), lambda i: (i, 0)),
            pl.BlockSpec((POOL_HALO, POOL_W), lambda i: (jnp.maximum(i * halo_blocks - 1, 0), 0)),
            pl.BlockSpec((POOL_HALO, POOL_W), lambda i: (jnp.minimum((i + 1) * halo_blocks, n_halo - 1), 0)),
            pl.BlockSpec((tm, ATTN_W), lambda i: (i, 0)),
            pl.BlockSpec((tm, D_MODEL), lambda i: (i, 0)),
            pl.BlockSpec((None, 1, D_MODEL), lambda i: (i // tiles_per_mod, 0, 0)),
            pl.BlockSpec((POOL_W, POOL_W), lambda i: (0, 0)),
            pl.BlockSpec((1, POOL_W), lambda i: (0, 0)),
            pl.BlockSpec((SGU_GROUPS, SGU_CHUNK, SGU_CHUNK), lambda i: (0, 0, 0)),
            pl.BlockSpec((SGU_CHUNK, SGU_W), lambda i: (0, 0)),
            pl.BlockSpec((BR_W, D_MODEL), lambda i: (0, 0)),
            pl.BlockSpec((D_MODEL, D_MODEL), lambda i: (0, 0)),
        ],
        out_specs=pl.BlockSpec((tm, D_MODEL), lambda i: (i, 0)),
        scratch_shapes=[pltpu.VMEM((tm + 2 * POOL_HALO, POOL_W), F32)],
        compiler_params=_cparams(("parallel",)),
        name="mixer",
    )(rest, rest, rest, y_attn, x2d, gt, poolw_bd, pscale, sguw, sgub_full, wbr, wout)


def _top16_rows(problems, emit):
    scores = [p[0] for p in problems]
    for r in range(PEER_TOPK):
        for i, (_, ids) in enumerate(problems):
            s = scores[i]
            m = jnp.max(s, axis=0, keepdims=True)
            idx = jnp.min(jnp.where(s == m, ids, _NO_ROW), axis=0, keepdims=True)
            hit = ids == idx
            emit(i, r, m, hit, idx)
            scores[i] = jnp.where(hit, -jnp.inf, s)


def _pair_candidates(sv0, sv1, si0, si1, sub):
    subf = sub.astype(F32)
    sums, experts, flat = [], [], []
    for a in range(SUBLANES):
        n_b = PEER_TOPK // (a + 1)
        for b0 in range(0, n_b, SUBLANES):
            s = sv0[a:a + 1, :] + sv1[b0:b0 + SUBLANES, :]
            if n_b - b0 < SUBLANES:
                s = jnp.where(sub < n_b - b0, s, -jnp.inf)
            sums.append(s)
            experts.append(si0[a:a + 1, :] * PEER_KEYS + si1[b0:b0 + SUBLANES, :])
            flat.append(subf + float(a * PEER_TOPK + b0))
    assert PEER_TOPK // (SUBLANES + 1) == 1
    sums.append(sv0[SUBLANES:, :] + sv1[0:1, :])
    experts.append(si0[SUBLANES:, :] * PEER_KEYS + si1[0:1, :])
    flat.append((subf + float(SUBLANES)) * float(PEER_TOPK))
    return jnp.concatenate(sums, axis=0), jnp.concatenate(experts, axis=0), jnp.concatenate(flat, axis=0)


def _route_sub_keys(q_ref, sk_ref, sv_ref, si_ref, tm):
    key_id = lax.broadcasted_iota(jnp.int32, (PEER_KEYS, tm), 0).astype(F32)

    def sub_keys(it, carry):
        hps = [it * ROUTE_INTERLEAVE + u for u in range(ROUTE_INTERLEAVE)]
        problems = [(_dot_nt(sk_ref[hp], q_ref[hp].astype(BF16)), key_id) for hp in hps]

        def emit(i, r, m, hit, idx):
            sv_ref[hps[i], r:r + 1, :] = m
            si_ref[hps[i], r:r + 1, :] = idx.astype(jnp.int32)

        _top16_rows(problems, emit)
        return carry

    lax.fori_loop(0, 2 * PEER_HEADS // ROUTE_INTERLEAVE, sub_keys, 0)


def _route_pairs(sv_ref, si_ref, e_ref, g_ref, ts_ref, te_ref, tm):
    sub = lax.broadcasted_iota(jnp.int32, (SUBLANES, tm), 0)

    def heads(it, carry):
        hs = [it * ROUTE_INTERLEAVE + u for u in range(ROUTE_INTERLEAVE)]
        problems, experts = [], []
        for h in hs:
            cs, ce, cid = _pair_candidates(sv_ref[2 * h], sv_ref[2 * h + 1], si_ref[2 * h], si_ref[2 * h + 1], sub)
            problems.append((cs, cid))
            experts.append(ce)

        def emit(i, r, m, hit, idx):
            ts_ref[i, r:r + 1, :] = m
            te_ref[i, r:r + 1, :] = jnp.max(jnp.where(hit, experts[i], -1), axis=0, keepdims=True)

        _top16_rows(problems, emit)
        for i, h in enumerate(hs):
            row0 = pl.multiple_of(h * PEER_TOPK, PEER_TOPK)
            e_ref[pl.ds(row0, PEER_TOPK), :] = te_ref[i]
            ts = ts_ref[i]
            ex = jnp.exp(ts - jnp.max(ts, axis=0, keepdims=True))
            g_ref[pl.ds(row0, PEER_TOPK), :] = ex / jnp.sum(ex, axis=0, keepdims=True)
        return carry

    lax.fori_loop(0, PEER_HEADS // ROUTE_INTERLEAVE, heads, 0)


def _peer_route_kernel(q_ref, sk_ref, e_ref, g_ref, sv_ref, si_ref, ts_ref, te_ref, *, tm):
    _route_sub_keys(q_ref, sk_ref, sv_ref, si_ref, tm)
    _route_pairs(sv_ref, si_ref, e_ref, g_ref, ts_ref, te_ref, tm)


def _peer_pairs_kernel(sv_ref, si_ref, e_ref, g_ref, ts_ref, te_ref, *, tm):
    _route_pairs(sv_ref, si_ref, e_ref, g_ref, ts_ref, te_ref, tm)


def _route_out(nblk, tm):
    shape = (jax.ShapeDtypeStruct((nblk, PEER_SLOTS, tm), jnp.int32), jax.ShapeDtypeStruct((nblk, PEER_SLOTS, tm), F32))
    specs = (pl.BlockSpec((None, PEER_SLOTS, tm), lambda i: (i, 0, 0)),
             pl.BlockSpec((None, PEER_SLOTS, tm), lambda i: (i, 0, 0)))
    scratch = [pltpu.VMEM((ROUTE_INTERLEAVE, PEER_TOPK, tm), F32), pltpu.VMEM((ROUTE_INTERLEAVE, PEER_TOPK, tm), jnp.int32)]
    return shape, specs, scratch


def _peer_route(q3, subkeys_bf16, n_tokens):
    tm = ROUTE_TILE
    nblk = n_tokens // tm
    out_shape, out_specs, out_scratch = _route_out(nblk, tm)
    return pl.pallas_call(
        functools.partial(_peer_route_kernel, tm=tm),
        out_shape=out_shape,
        grid=(nblk,),
        in_specs=[
            pl.BlockSpec((2 * PEER_HEADS, tm, PEER_HALF), lambda i: (0, i, 0)),
            pl.BlockSpec((2 * PEER_HEADS, PEER_KEYS, PEER_HALF), lambda i: (0, 0, 0)),
        ],
        out_specs=out_specs,
        scratch_shapes=[
            pltpu.VMEM((2 * PEER_HEADS, PEER_TOPK, tm), F32),
            pltpu.VMEM((2 * PEER_HEADS, PEER_TOPK, tm), jnp.int32),
        ] + out_scratch,
        compiler_params=_cparams(("parallel",)),
        name="peer_route",
    )(q3, subkeys_bf16)


def _peer_pairs(sv, si):
    nblk, _, _, tm = sv.shape
    out_shape, out_specs, out_scratch = _route_out(nblk, tm)
    list_spec = pl.BlockSpec((None, 2 * PEER_HEADS, PEER_TOPK, tm), lambda i: (i, 0, 0, 0))
    return pl.pallas_call(
        functools.partial(_peer_pairs_kernel, tm=tm),
        out_shape=out_shape,
        grid=(nblk,),
        in_specs=[list_spec, list_spec],
        out_specs=out_specs,
        scratch_shapes=out_scratch,
        compiler_params=_cparams(("parallel",)),
        name="peer_pairs",
    )(sv, si)


def _peer_apply_kernel(*refs, tb, side):
    if side:
        (idx_ref, gate_ref, h_ref, x_ref, gt_ref, uv_hbm, q_ref, sk_ref, o_ref, sv_ref, si_ref,
         buf_ref, sem_ref, part_ref, acc_ref, s_ref) = refs
    else:
        idx_ref, gate_ref, h_ref, x_ref, gt_ref, uv_hbm, o_ref, buf_ref, sem_ref, part_ref, acc_ref = refs
    lookahead = GATHER_SLOTS - 1
    n_groups = tb // TOKEN_GROUP
    n_tiles = D_MODEL // LANES
    n_slot_groups = PEER_SLOTS // SUBLANES

    def start_rows(tok, slot, k0, n):
        for k in range(k0, k0 + n):
            pltpu.make_async_copy(uv_hbm.at[idx_ref[tok, k]], buf_ref.at[slot, k],
                                  sem_ref.at[slot]).start(priority=k % 2)

    def wait_gather(slot):
        pltpu.make_async_copy(uv_hbm.at[pl.ds(0, PEER_SLOTS)], buf_ref.at[slot], sem_ref.at[slot]).wait()

    for tok in range(lookahead):
        start_rows(tok, tok % GATHER_SLOTS, 0, PEER_SLOTS)

    tok_lane = lax.broadcasted_iota(jnp.int32, (PEER_SLOTS, ROUTE_TILE), 1)
    sub = lax.broadcasted_iota(jnp.int32, (SUBLANES, LANES), 0)

    def fold(x0, x1, d):
        t0 = x0 + pltpu.roll(x0, SUBLANES - d, 0)
        t1 = x1 + pltpu.roll(x1, d, 0)
        return jnp.where((sub & d) == 0, t0, t1)

    key_id = lax.broadcasted_iota(jnp.int32, (PEER_KEYS, ROUTE_TILE), 0).astype(F32)
    groups_per_list_pair = PEER_TOPK // TOKEN_GROUP
    groups_per_tile = groups_per_list_pair * PEER_HEADS
    if side:
        assert n_groups * TOKEN_GROUP == tb and tb % ROUTE_TILE == 0 and groups_per_tile * TOKEN_GROUP == ROUTE_TILE

    def group(gi, is_last):
        base = gi * TOKEN_GROUP if is_last else pl.multiple_of(gi * TOKEN_GROUP, TOKEN_GROUP)
        h8 = h_ref[pl.ds(base, TOKEN_GROUP), :]
        rows = []
        if side:
            tile = gi // groups_per_tile
            hp0 = 2 * ((gi // groups_per_list_pair) % PEER_HEADS)
            r0 = (gi % groups_per_list_pair) * TOKEN_GROUP

            def load_scores():
                rows0 = tile * ROUTE_TILE if is_last else pl.multiple_of(tile * ROUTE_TILE, ROUTE_TILE)
                for pi in range(2):
                    s_ref[pi] = _dot_nt(sk_ref[hp0 + pi], q_ref[hp0 + pi, pl.ds(rows0, ROUTE_TILE), :].astype(BF16))

            if is_last:
                if r0 == 0:
                    load_scores()
            else:
                pl.when(r0 == 0)(load_scores)
        for j in range(TOKEN_GROUP):
            tok = base + j
            slot = j % GATHER_SLOTS
            prefetch = not is_last or j + lookahead < TOKEN_GROUP

            def start_ahead(k0, n, tok=tok, j=j, prefetch=prefetch):
                if prefetch:
                    start_rows(tok + lookahead, (j + lookahead) % GATHER_SLOTS, k0, n)

            wait_gather(slot)
            h_tile = jnp.concatenate([h8[j:j + 1, s * LANES:(s + 1) * LANES] for s in range(n_tiles)], axis=0)
            for g in range(n_slot_groups):
                p = [lax.bitcast_convert_type(buf_ref[slot, g * SUBLANES + c] & jnp.uint32(0xFFFF0000), F32) * h_tile
                     for c in range(SUBLANES)]
                y = [fold(p[c], p[c + 4], 4) for c in range(4)]
                z = [fold(y[c], y[c + 2], 2) for c in range(2)]
                part_ref[g * SUBLANES:(g + 1) * SUBLANES, :] = fold(z[0], z[1], 1)
                start_ahead(g * ISSUE_PER_GROUP, ISSUE_PER_GROUP)
            acc_ref[...] = jnp.zeros(acc_ref.shape, F32)
            start_ahead(n_slot_groups * ISSUE_PER_GROUP, PEER_SLOTS - 2 * n_slot_groups * ISSUE_PER_GROUP)
            if side:
                for pi in range(2):
                    sc = s_ref[pi]
                    m = jnp.max(sc, axis=0, keepdims=True)
                    idx = jnp.min(jnp.where(sc == m, key_id, _NO_ROW), axis=0, keepdims=True)
                    sv_ref[tile, hp0 + pi, r0 + j] = m
                    si_ref[tile, hp0 + pi, r0 + j] = idx.astype(jnp.int32)
                    s_ref[pi] = jnp.where(key_id == idx, -jnp.inf, sc)
            act = jnp.sum(part_ref[...], axis=1, keepdims=True)
            gates = gate_ref[tok // ROUTE_TILE]
            gate = jnp.sum(jnp.where(tok_lane == tok % ROUTE_TILE, gates, 0.0), axis=1, keepdims=True)
            w = gate * _gelu(act)
            for g in range(n_slot_groups):
                terms = [jnp.broadcast_to(w[k:k + 1, :], (SUBLANES, LANES))
                         * lax.bitcast_convert_type(buf_ref[slot, k] << 16, F32)
                         for k in range(g * SUBLANES, (g + 1) * SUBLANES)]
                s4 = [terms[i] + terms[i + 4] for i in range(4)]
                acc_ref[g % 2] = acc_ref[g % 2] + ((s4[0] + s4[1]) + (s4[2] + s4[3]))
                start_ahead(PEER_SLOTS - (n_slot_groups - g) * ISSUE_PER_GROUP, ISSUE_PER_GROUP)
            acc = acc_ref[0] + acc_ref[1]
            rows.append(jnp.concatenate([acc[s:s + 1, :] for s in range(n_tiles)], axis=1))
        out8 = jnp.concatenate(rows, axis=0)
        o_ref[pl.ds(base, TOKEN_GROUP), :] = x_ref[pl.ds(base, TOKEN_GROUP), :] + gt_ref[...] * out8

    def body(gi, carry):
        group(gi, False)
        return carry

    lax.fori_loop(0, n_groups - 1, body, 0)
    group(n_groups - 1, True)


def _peer_apply(experts_tok, gates, h2d, x2d, gt, uv_table, n_tokens, row0, rows_per_mod, side=None):
    tb = min(APPLY_TILE, n_tokens)
    tiles_per_mod = rows_per_mod // tb
    tile0 = row0 // tb
    in_specs = [
        pl.BlockSpec((tb, PEER_SLOTS), lambda i: (i, 0), memory_space=pltpu.SMEM),
        pl.BlockSpec((tb // ROUTE_TILE, PEER_SLOTS, ROUTE_TILE), lambda i: (i, 0, 0)),
        pl.BlockSpec((tb, D_MODEL), lambda i: (tile0 + i, 0)),
        pl.BlockSpec((tb, D_MODEL), lambda i: (tile0 + i, 0)),
        pl.BlockSpec((None, 1, D_MODEL), lambda i: ((tile0 + i) // tiles_per_mod, 0, 0)),
        pl.BlockSpec(memory_space=pl.ANY),
    ]
    args = [experts_tok, gates, h2d, x2d, gt, uv_table]
    out_shape = [jax.ShapeDtypeStruct((n_tokens, D_MODEL), F32)]
    out_specs = [pl.BlockSpec((tb, D_MODEL), lambda i: (i, 0))]
    scratch = [
        pltpu.VMEM((GATHER_SLOTS, PEER_SLOTS, SUBLANES, LANES), jnp.uint32),
        pltpu.SemaphoreType.DMA((GATHER_SLOTS,)),
        pltpu.VMEM((PEER_SLOTS, LANES), F32),
        pltpu.VMEM((2, SUBLANES, LANES), F32),
    ]
    if side is not None:
        q3, subkeys, side_row0 = side
        side_tile0 = side_row0 // tb
        in_specs += [
            pl.BlockSpec((2 * PEER_HEADS, tb, PEER_HALF), lambda i: (0, side_tile0 + i, 0)),
            pl.BlockSpec((2 * PEER_HEADS, PEER_KEYS, PEER_HALF), lambda i: (0, 0, 0)),
        ]
        args += [q3, subkeys]
        list_shape = (n_tokens // ROUTE_TILE, 2 * PEER_HEADS, PEER_TOPK, 1, ROUTE_TILE)
        list_block = (tb // ROUTE_TILE, 2 * PEER_HEADS, PEER_TOPK, 1, ROUTE_TILE)
        out_shape += [jax.ShapeDtypeStruct(list_shape, F32), jax.ShapeDtypeStruct(list_shape, jnp.int32)]
        out_specs += [pl.BlockSpec(list_block, lambda i: (i, 0, 0, 0, 0))] * 2
        scratch.append(pltpu.VMEM((2, PEER_KEYS, ROUTE_TILE), F32))
    outs = pl.pallas_call(
        functools.partial(_peer_apply_kernel, tb=tb, side=side is not None),
        out_shape=tuple(out_shape),
        grid=(n_tokens // tb,),
        in_specs=in_specs,
        out_specs=tuple(out_specs),
        scratch_shapes=scratch,
        compiler_params=_cparams(("arbitrary",), disable_bounds_checks=True),
        name="peer_apply_side" if side is not None else "peer_apply",
    )(*args)
    if side is None:
        return outs[0]
    sv, si = (a.reshape(a.shape[:3] + a.shape[4:]) for a in outs[1:])
    return outs[0], sv, si


def _final_norm_kernel(x_ref, g_ref, o_ref):
    x = x_ref[...]
    o_ref[...] = (x * lax.rsqrt(jnp.mean(x * x, axis=-1, keepdims=True) + EPS)) * g_ref[...]


def _final_norm(x2d, g):
    t = x2d.shape[0]
    tm = min(2 * TOKEN_TILE, t)
    return pl.pallas_call(
        _final_norm_kernel,
        out_shape=jax.ShapeDtypeStruct((t, D_MODEL), F32),
        grid=(t // tm,),
        in_specs=[pl.BlockSpec((tm, D_MODEL), lambda i: (i, 0)), pl.BlockSpec((1, D_MODEL), lambda i: (0, 0))],
        out_specs=pl.BlockSpec((tm, D_MODEL), lambda i: (i, 0)),
        compiler_params=_cparams(("parallel",)),
        name="final_norm",
    )(x2d, g)


def _rope_tables(length):
    rows = length // GRID_W
    row = jnp.repeat(jnp.arange(rows, dtype=F32), GRID_W)
    col = jnp.tile(jnp.arange(GRID_W, dtype=F32), rows)
    inv = ROPE_BASE ** (-jnp.arange(ROPE_FREQS, dtype=F32) / ROPE_FREQS)
    ang = jnp.stack([row[:, None] * inv, col[:, None] * inv], axis=1)
    cos, sin = jnp.cos(ang), jnp.sin(ang)
    cos_h = jnp.broadcast_to(cos[:, :, None, :], (length, 2, 2, ROPE_FREQS)).reshape(length, HEAD_DIM)
    sin_h = jnp.stack([-sin, sin], axis=2).reshape(length, HEAD_DIM)
    reps = LANES // HEAD_DIM
    return jnp.tile(cos_h, (1, reps)), jnp.tile(sin_h, (1, reps))


def _pack_expert_tables(u_tab, v_tab):
    def bits(a):
        return lax.bitcast_convert_type(a.astype(BF16), jnp.uint16).astype(jnp.uint32)

    return ((bits(u_tab) << 16) | bits(v_tab)).reshape(-1, D_MODEL // LANES, LANES)


def _experts_by_token(experts):
    return jnp.transpose(experts, (0, 2, 1)).reshape(-1, PEER_SLOTS)


def _peer_layer(x2d, g, sc, sh, gt, rows_per_mod, wq, subkeys, uv_table, n_chunks):
    t = x2d.shape[0]
    q3, h2d = _peer_query(x2d, g, sc, sh, wq, rows_per_mod)
    n = t // n_chunks
    experts, gates = _peer_route(q3, subkeys, n)
    outs = []
    for c in range(n_chunks):
        side = (q3, subkeys, (c + 1) * n) if c + 1 < n_chunks else None
        res = _peer_apply(_experts_by_token(experts), gates, h2d, x2d, gt, uv_table, n, c * n, rows_per_mod, side)
        if side is None:
            outs.append(res)
        else:
            outs.append(res[0])
            experts, gates = _peer_pairs(res[1], res[2])
    return outs[0] if n_chunks == 1 else jnp.concatenate(outs, axis=0)


def kernel(x, c, ctx, c_ctx, w_mod, b_mod, g_mix, g_ffn, w_in, attn_sink, pool_w, pool_scale, sgu_w, sgu_b,
           w_br_attn, w_br_pool, w_br_sgu, w_out, peer_wq, peer_subkeys, peer_u, peer_v, g_final):
    b, l, d = x.shape
    n_ctx = ctx.shape[1]
    depth = w_mod.shape[0]
    assert d == D_MODEL and l % TOKEN_TILE == 0 and n_ctx % SGU_CHUNK == 0

    n_rows = -(-(b + 1) // SUBLANES) * SUBLANES
    cond = jnp.zeros((n_rows, d), F32).at[:b].set(c).at[b].set(c_ctx)
    mod = _adaln(cond, w_mod, b_mod).reshape(depth, n_rows, N_MOD, 1, d)

    rope = _rope_tables(l)
    x2d = x.reshape(b * l, d)
    xc2d = ctx.reshape(b * n_ctx, d)

    for layer in range(depth):
        last = layer == depth - 1
        m_lat = [mod[layer, :b, i] for i in range(N_MOD)]
        m_ctx = [mod[layer, b:b + 1, i] for i in range(N_MOD)]
        g1 = g_mix[layer].reshape(1, d)
        g2 = g_ffn[layer].reshape(1, d)

        w_in_b = w_in[layer].astype(BF16)
        qkv, rest = _in_proj(x2d, g1, m_lat[1], m_lat[0], w_in_b, l, rope, l)
        qkv_c, rest_c = _in_proj(xc2d, g1, m_ctx[1], m_ctx[0], w_in_b, b * n_ctx, None, n_ctx)

        y_attn = _attention(qkv.reshape(b, l, QKV_W), qkv_c.reshape(b, n_ctx, QKV_W), attn_sink[layer], True)

        poolw_bd = jnp.zeros((POOL_W, POOL_W), F32)
        for gi in range(len(POOL_SIZES)):
            sl = slice(gi * POOL_GROUP_W, (gi + 1) * POOL_GROUP_W)
            poolw_bd = poolw_bd.at[sl, sl].set(pool_w[layer, gi])
        poolw_bd = poolw_bd.astype(BF16)
        pscale = pool_scale[layer].reshape(1, POOL_W)
        sguw = sgu_w[layer].astype(BF16)
        sgub_full = jnp.repeat(sgu_b[layer].T, SGU_GROUP_W, axis=1)
        wbr = jnp.concatenate([w_br_attn[layer], w_br_pool[layer], w_br_sgu[layer]], axis=0).astype(BF16)
        wout = w_out[layer].astype(BF16)
        mix_w = (poolw_bd, pscale, sguw, sgub_full, wbr, wout)

        x2d = _mixer(rest, y_attn.reshape(b * l, ATTN_W), x2d, m_lat[2], *mix_w, l, l)
        if not last:
            y_attn_c = _attention(qkv_c.reshape(b, n_ctx, QKV_W), qkv_c.reshape(b, n_ctx, QKV_W),
                                  attn_sink[layer], False)
            xc2d = _mixer(rest_c, y_attn_c.reshape(b * n_ctx, ATTN_W), xc2d, m_ctx[2], *mix_w,
                          b * n_ctx, n_ctx)

        wq = peer_wq[layer].astype(BF16)
        subkeys = peer_subkeys[layer].reshape(2 * PEER_HEADS, PEER_KEYS, PEER_HALF).astype(BF16)
        uv_table = _pack_expert_tables(peer_u[layer], peer_v[layer])
        x2d = _peer_layer(x2d, g2, m_lat[4], m_lat[3], m_lat[5], l, wq, subkeys, uv_table, b)
        if not last:
            xc2d = _peer_layer(xc2d, g2, m_ctx[4], m_ctx[3], m_ctx[5], b * n_ctx, wq, subkeys, uv_table, 1)

    return _final_norm(x2d, g_final.reshape(1, d)).reshape(b, l, d)
```

```python
import functools

import jax
import jax.numpy as jnp
from jax import lax
from jax.experimental import pallas as pl
from jax.experimental.pallas import tpu as pltpu

F32 = jnp.float32
BF16 = jnp.bfloat16

D_MODEL = 1024
EPS = 1e-6
N_MOD = 6
GRID_W = 64

N_HEADS = 8
N_KV_HEADS = 2
GQA_GROUP = N_HEADS // N_KV_HEADS
HEAD_DIM = 64
WINDOW = 128
ATTN_BLOCK = 128
ATTN_W = N_HEADS * HEAD_DIM
KV_W = N_KV_HEADS * HEAD_DIM
QKV_W = ATTN_W + 2 * KV_W
ROPE_BASE = 10000.0
ROPE_FREQS = HEAD_DIM // 4

POOL_SIZES = (2, 4, 8, 16)
POOL_GROUP_W = 64
POOL_W = len(POOL_SIZES) * POOL_GROUP_W
POOL_HALO = 8

SGU_CHUNK = 128
SGU_GROUPS = 4
SGU_W = 256
SGU_GROUP_W = SGU_W // SGU_GROUPS

N_BRANCH = 3
REST_W = POOL_W + 2 * SGU_W + N_BRANCH * D_MODEL
IN_W = QKV_W + REST_W
BR_W = ATTN_W + POOL_W + SGU_W

PEER_HEADS = 8
PEER_KEYS = 128
PEER_HALF = 128
PEER_TOPK = 16
PEER_SLOTS = PEER_HEADS * PEER_TOPK
PEER_QW = PEER_HEADS * 2 * PEER_HALF

LANES = 128
SUBLANES = 8
VMEM_LIMIT = 48 * 1024 * 1024

TOKEN_TILE = 256
ROUTE_TILE = 128
APPLY_TILE = 256
GATHER_SLOTS = 8
TOKEN_GROUP = 8
ISSUE_PER_GROUP = 3
ROUTE_INTERLEAVE = 2
_NO_ROW = 1e9


def _cparams(sem, **kw):
    return pltpu.CompilerParams(dimension_semantics=sem, vmem_limit_bytes=VMEM_LIMIT, **kw)


def _dot(a, b):
    return jnp.dot(a, b, preferred_element_type=F32)


def _dot_nt(a, b):
    return lax.dot_general(a, b, (((1,), (1,)), ((), ())), preferred_element_type=F32)


def _split_bf16(a):
    hi = a.astype(BF16)
    lo = (a - hi.astype(F32)).astype(BF16)
    return hi, lo


def _adaln_kernel(c_ref, w_ref, b_ref, o_ref):
    c = c_ref[...]
    s = c * (1.0 / (1.0 + jnp.exp(-c)))
    s_hi, s_lo = _split_bf16(s)
    w_hi, w_lo = _split_bf16(w_ref[...])
    acc = _dot(s_hi, w_hi) + (_dot(s_hi, w_lo) + _dot(s_lo, w_hi))
    o_ref[...] = acc + b_ref[...]


def _adaln(cond, w_mod, b_mod):
    depth = w_mod.shape[0]
    r = cond.shape[0]
    n = w_mod.shape[2]
    tn = D_MODEL
    return pl.pallas_call(
        _adaln_kernel,
        out_shape=jax.ShapeDtypeStruct((depth, r, n), F32),
        grid=(depth, n // tn),
        in_specs=[
            pl.BlockSpec((r, D_MODEL), lambda l, j: (0, 0)),
            pl.BlockSpec((None, D_MODEL, tn), lambda l, j: (l, 0, j)),
            pl.BlockSpec((None, 1, tn), lambda l, j: (l, 0, j)),
        ],
        out_specs=pl.BlockSpec((None, r, tn), lambda l, j: (l, 0, j)),
        compiler_params=_cparams(("arbitrary", "arbitrary")),
        name="adaln",
    )(cond, w_mod, b_mod.reshape(depth, 1, n))


def _modulated_norm(x, g, sc, sh):
    y = x * lax.rsqrt(jnp.mean(x * x, axis=-1, keepdims=True) + EPS)
    return (y * g) * (1.0 + sc) + sh


def _rope_tile(x, cos, sin_signed):
    lane = lax.broadcasted_iota(jnp.int32, x.shape, 1)
    first_half = (lane % (2 * ROPE_FREQS)) < ROPE_FREQS
    partner = jnp.where(first_half, pltpu.roll(x, LANES - ROPE_FREQS, 1), pltpu.roll(x, ROPE_FREQS, 1))
    return x * cos + partner * sin_signed


def _in_proj_kernel(x_ref, g_ref, sc_ref, sh_ref, w_ref, *rest, rope):
    if rope:
        cos_ref, sin_ref, qkv_ref, rest_ref = rest
    else:
        qkv_ref, rest_ref = rest
    hb = _modulated_norm(x_ref[...], g_ref[...], sc_ref[...], sh_ref[...]).astype(BF16)
    for c in range(QKV_W // LANES):
        cols = slice(c * LANES, (c + 1) * LANES)
        acc = _dot(hb, w_ref[:, cols])
        if rope and c * LANES < ATTN_W + KV_W:
            acc = _rope_tile(acc, cos_ref[...], sin_ref[...])
        qkv_ref[:, cols] = acc
    chunk = 768
    for c in range(REST_W // chunk):
        rest_ref[:, c * chunk:(c + 1) * chunk] = _dot(hb, w_ref[:, QKV_W + c * chunk:QKV_W + (c + 1) * chunk])


def _in_proj(x2d, g, sc, sh, w_bf16, rows_per_mod, rope_tables, seq_len):
    t = x2d.shape[0]
    tm = min(TOKEN_TILE, t)
    tiles_per_mod = rows_per_mod // tm
    rope = rope_tables is not None
    in_specs = [
        pl.BlockSpec((tm, D_MODEL), lambda i: (i, 0)),
        pl.BlockSpec((1, D_MODEL), lambda i: (0, 0)),
        pl.BlockSpec((None, 1, D_MODEL), lambda i: (i // tiles_per_mod, 0, 0)),
        pl.BlockSpec((None, 1, D_MODEL), lambda i: (i // tiles_per_mod, 0, 0)),
        pl.BlockSpec((D_MODEL, IN_W), lambda i: (0, 0)),
    ]
    args = [x2d, g, sc, sh, w_bf16]
    if rope:
        tiles_per_seq = seq_len // tm
        in_specs += [pl.BlockSpec((tm, LANES), lambda i: (i % tiles_per_seq, 0))] * 2
        args += list(rope_tables)
    return pl.pallas_call(
        functools.partial(_in_proj_kernel, rope=rope),
        out_shape=(jax.ShapeDtypeStruct((t, QKV_W), F32), jax.ShapeDtypeStruct((t, REST_W), F32)),
        grid=(t // tm,),
        in_specs=in_specs,
        out_specs=(pl.BlockSpec((tm, QKV_W), lambda i: (i, 0)), pl.BlockSpec((tm, REST_W), lambda i: (i, 0))),
        compiler_params=_cparams(("parallel",)),
        name="in_proj",
    )(*args)


def _peer_query_kernel(x_ref, g_ref, sc_ref, sh_ref, w_ref, q_ref, h_ref):
    h = _modulated_norm(x_ref[...], g_ref[...], sc_ref[...], sh_ref[...])
    h_ref[...] = h
    hb = h.astype(BF16)
    for hp in range(2 * PEER_HEADS):
        q_ref[hp] = _dot(hb, w_ref[:, hp * PEER_HALF:(hp + 1) * PEER_HALF])


def _peer_query(x2d, g, sc, sh, wq_bf16, rows_per_mod):
    t = x2d.shape[0]
    tm = min(TOKEN_TILE, t)
    tiles_per_mod = rows_per_mod // tm
    return pl.pallas_call(
        _peer_query_kernel,
        out_shape=(jax.ShapeDtypeStruct((2 * PEER_HEADS, t, PEER_HALF), F32),
                   jax.ShapeDtypeStruct((t, D_MODEL), F32)),
        grid=(t // tm,),
        in_specs=[
            pl.BlockSpec((tm, D_MODEL), lambda i: (i, 0)),
            pl.BlockSpec((1, D_MODEL), lambda i: (0, 0)),
            pl.BlockSpec((None, 1, D_MODEL), lambda i: (i // tiles_per_mod, 0, 0)),
            pl.BlockSpec((None, 1, D_MODEL), lambda i: (i // tiles_per_mod, 0, 0)),
            pl.BlockSpec((D_MODEL, PEER_QW), lambda i: (0, 0)),
        ],
        out_specs=(pl.BlockSpec((2 * PEER_HEADS, tm, PEER_HALF), lambda i: (0, i, 0)),
                   pl.BlockSpec((tm, D_MODEL), lambda i: (i, 0))),
        compiler_params=_cparams(("parallel",)),
        name="peer_query",
    )(x2d, g, sc, sh, wq_bf16)


def _attn_kernel(sink_ref, q_ref, *rest, window, n_blocks):
    if window:
        kvp_ref, kvc_ref, kvn_ref, ctx_ref, o_ref = rest
    else:
        ctx_ref, o_ref = rest
    n = pl.program_id(1)
    q = q_ref[...] * (HEAD_DIM ** -0.5)
    sources = []
    if window:
        qi = lax.broadcasted_iota(jnp.int32, (GQA_GROUP * ATTN_BLOCK, ATTN_BLOCK), 0) % ATTN_BLOCK
        kj = lax.broadcasted_iota(jnp.int32, (GQA_GROUP * ATTN_BLOCK, ATTN_BLOCK), 1)
        sources.append((kvp_ref[...], (kj >= qi) & (n > 0)))
        sources.append((kvc_ref[...], None))
        sources.append((kvn_ref[...], (kj <= qi) & (n < n_blocks - 1)))
    for c in range(ctx_ref.shape[0] // ATTN_BLOCK):
        sources.append((ctx_ref[c * ATTN_BLOCK:(c + 1) * ATTN_BLOCK, :], None))
    kb = [[kv[:, g * HEAD_DIM:(g + 1) * HEAD_DIM].astype(BF16) for g in range(N_KV_HEADS)] for kv, _ in sources]
    ones = jnp.ones((ATTN_BLOCK, HEAD_DIM), BF16)
    vb = [[jnp.concatenate([kv[:, KV_W + g * HEAD_DIM:KV_W + (g + 1) * HEAD_DIM].astype(BF16), ones], axis=1)
           for g in range(N_KV_HEADS)] for kv, _ in sources]
    groups = [range(g * GQA_GROUP, (g + 1) * GQA_GROUP) for g in range(N_KV_HEADS)]
    sinks = [jnp.concatenate([jnp.full((ATTN_BLOCK, 1), sink_ref[h], F32) for h in heads], axis=0)
             for heads in groups]
    scores = []
    for g, heads in enumerate(groups):
        qg = jnp.concatenate([q[:, h * HEAD_DIM:(h + 1) * HEAD_DIM] for h in heads], axis=0).astype(BF16)
        sg = []
        for si, (_, valid) in enumerate(sources):
            s = _dot_nt(qg, kb[si][g])
            sg.append(s if valid is None else jnp.where(valid, s, -jnp.inf))
        scores.append(sg)
    maxes = [jnp.maximum(jnp.max(functools.reduce(jnp.maximum, scores[g]), axis=-1, keepdims=True), sinks[g])
             for g in range(N_KV_HEADS)]
    outs = []
    for g in range(N_KV_HEADS):
        m = maxes[g]
        acc = jnp.zeros((GQA_GROUP * ATTN_BLOCK, 2 * HEAD_DIM), F32)
        for si, s in enumerate(scores[g]):
            acc = acc + _dot(jnp.exp(s - m).astype(BF16), vb[si][g])
        og = acc[:, :HEAD_DIM] / (acc[:, HEAD_DIM:] + jnp.exp(sinks[g] - m))
        outs += [og[i * ATTN_BLOCK:(i + 1) * ATTN_BLOCK, :] for i in range(GQA_GROUP)]
    o_ref[...] = jnp.concatenate(outs, axis=-1)


def _attention(qkv, kv_ctx, sink, window):
    b, l, _ = qkv.shape
    n_ctx = kv_ctx.shape[1]
    nb = l // ATTN_BLOCK
    kv_col = ATTN_W // (2 * KV_W)
    in_specs = [
        pl.BlockSpec(memory_space=pltpu.SMEM),
        pl.BlockSpec((None, ATTN_BLOCK, ATTN_W), lambda bi, n: (bi, n, 0)),
    ]
    args = [sink, qkv]
    if window:
        in_specs += [
            pl.BlockSpec((None, ATTN_BLOCK, 2 * KV_W), lambda bi, n: (bi, jnp.maximum(n - 1, 0), kv_col)),
            pl.BlockSpec((None, ATTN_BLOCK, 2 * KV_W), lambda bi, n: (bi, n, kv_col)),
            pl.BlockSpec((None, ATTN_BLOCK, 2 * KV_W), lambda bi, n: (bi, jnp.minimum(n + 1, nb - 1), kv_col)),
        ]
        args += [qkv, qkv, qkv]
    in_specs.append(pl.BlockSpec((None, n_ctx, 2 * KV_W), lambda bi, n: (bi, 0, kv_col)))
    args.append(kv_ctx)
    return pl.pallas_call(
        functools.partial(_attn_kernel, window=window, n_blocks=nb),
        out_shape=jax.ShapeDtypeStruct((b, l, ATTN_W), F32),
        grid=(b, nb),
        in_specs=in_specs,
        out_specs=pl.BlockSpec((None, ATTN_BLOCK, ATTN_W), lambda bi, n: (bi, n, 0)),
        compiler_params=_cparams(("parallel", "parallel")),
        name="attention_window" if window else "attention_context",
    )(*args)


def _gelu(x):
    return jax.nn.gelu(x, approximate=True)


def _sigmoid(x):
    return 1.0 / (1.0 + jnp.exp(-x))


def _mixer_kernel(rest_ref, hprev_ref, hnext_ref, ya_ref, x_ref, gt_ref, poolw_ref, pscale_ref, sguw_ref,
                  sgub_ref, wbr_ref, wout_ref, o_ref, ext_ref, *, tm, tiles_per_seq, seq_len):
    ti = pl.program_id(0) % tiles_per_seq
    z = rest_ref[:, 0:POOL_W]
    ext_ref[0:POOL_HALO, :] = jnp.where(ti > 0, hprev_ref[...], 0.0)
    ext_ref[POOL_HALO:POOL_HALO + tm, :] = z
    ext_ref[POOL_HALO + tm:2 * POOL_HALO + tm, :] = jnp.where(ti < tiles_per_seq - 1, hnext_ref[...], 0.0)
    pos = ti * tm + lax.broadcasted_iota(jnp.int32, (tm, LANES), 0)
    lane = lax.broadcasted_iota(jnp.int32, (tm, LANES), 1)

    def count(size):
        hi = jnp.minimum(pos + size // 2, seq_len)
        lo = jnp.maximum(pos - size // 2, 0)
        return (hi - lo).astype(F32)

    diffs = []
    for lt in range(POOL_W // LANES):
        cols = slice(lt * LANES, (lt + 1) * LANES)

        def shifted(d, cols=cols):
            return ext_ref[POOL_HALO + d:POOL_HALO + d + tm, cols]

        small, large = POOL_SIZES[2 * lt], POOL_SIZES[2 * lt + 1]
        s = shifted(-1) + shifted(0)
        width = 2
        sums = {}
        while width <= large:
            sums[width] = s
            half = width // 2
            if 2 * width <= large:
                for d in range(half, width):
                    s = s + shifted(-d - 1) + shifted(d)
            width *= 2
        mean = jnp.where(lane < POOL_GROUP_W, sums[small] / count(small), sums[large] / count(large))
        diffs.append(mean - z[:, cols])
    d = jnp.concatenate(diffs, axis=-1).astype(BF16)
    y_pool = _dot(d, poolw_ref[...]) * pscale_ref[...]

    u = _gelu(rest_ref[:, POOL_W:POOL_W + SGU_W])
    v = _gelu(rest_ref[:, POOL_W + SGU_W:POOL_W + 2 * SGU_W])
    vn = (v * lax.rsqrt(jnp.mean(v * v, axis=-1, keepdims=True) + EPS)).astype(BF16)
    group = lax.broadcasted_iota(jnp.int32, (SGU_CHUNK, SGU_W), 1) // SGU_GROUP_W
    mixed = []
    for c in range(tm // SGU_CHUNK):
        vc = vn[c * SGU_CHUNK:(c + 1) * SGU_CHUNK, :]
        mc = sgub_ref[...]
        for hg in range(SGU_GROUPS):
            mc = mc + jnp.where(group == hg, _dot(sguw_ref[hg], vc), 0.0)
        mixed.append(mc)
    y_sgu = u * jnp.concatenate(mixed, axis=0)

    g_off = POOL_W + 2 * SGU_W
    merged = _sigmoid(rest_ref[:, g_off:g_off + D_MODEL]) * _dot(ya_ref[...].astype(BF16), wbr_ref[0:ATTN_W, :])
    merged = merged + _sigmoid(rest_ref[:, g_off + D_MODEL:g_off + 2 * D_MODEL]) * _dot(
        y_pool.astype(BF16), wbr_ref[ATTN_W:ATTN_W + POOL_W, :])
    merged = merged + _sigmoid(rest_ref[:, g_off + 2 * D_MODEL:g_off + 3 * D_MODEL]) * _dot(
        y_sgu.astype(BF16), wbr_ref[ATTN_W + POOL_W:BR_W, :])
    o_ref[...] = x_ref[...] + gt_ref[...] * _dot(merged.astype(BF16), wout_ref[...])


def _mixer(rest, y_attn, x2d, gt, poolw_bd, pscale, sguw, sgub_full, wbr, wout, rows_per_mod, seq_len):
    t = x2d.shape[0]
    tm = min(TOKEN_TILE, seq_len)
    tiles_per_seq = seq_len // tm
    tiles_per_mod = rows_per_mod // tm
    halo_blocks = tm // POOL_HALO
    n_halo = t // POOL_HALO
    kern = functools.partial(_mixer_kernel, tm=tm, tiles_per_seq=tiles_per_seq, seq_len=seq_len)
    return pl.pallas_call(
        kern,
        out_shape=jax.ShapeDtypeStruct((t, D_MODEL), F32),
        grid=(t // tm,),
        in_specs=[
            pl.BlockSpec((tm, REST_W), lambda i: (i, 0)),
            pl.BlockSpec((POOL_HALO, POOL_W), lambda i: (jnp.maximum(i * halo_blocks - 1, 0), 0)),
            pl.BlockSpec((POOL_HALO, POOL_W), lambda i: (jnp.minimum((i + 1) * halo_blocks, n_halo - 1), 0)),
            pl.BlockSpec((tm, ATTN_W), lambda i: (i, 0)),
            pl.BlockSpec((tm, D_MODEL), lambda i: (i, 0)),
            pl.BlockSpec((None, 1, D_MODEL), lambda i: (i // tiles_per_mod, 0, 0)),
            pl.BlockSpec((POOL_W, POOL_W), lambda i: (0, 0)),
            pl.BlockSpec((1, POOL_W), lambda i: (0, 0)),
            pl.BlockSpec((SGU_GROUPS, SGU_CHUNK, SGU_CHUNK), lambda i: (0, 0, 0)),
            pl.BlockSpec((SGU_CHUNK, SGU_W), lambda i: (0, 0)),
            pl.BlockSpec((BR_W, D_MODEL), lambda i: (0, 0)),
            pl.BlockSpec((D_MODEL, D_MODEL), lambda i: (0, 0)),
        ],
        out_specs=pl.BlockSpec((tm, D_MODEL), lambda i: (i, 0)),
        scratch_shapes=[pltpu.VMEM((tm + 2 * POOL_HALO, POOL_W), F32)],
        compiler_params=_cparams(("parallel",)),
        name="mixer",
    )(rest, rest, rest, y_attn, x2d, gt, poolw_bd, pscale, sguw, sgub_full, wbr, wout)


def _top16_rows(problems, emit):
    scores = [p[0] for p in problems]
    for r in range(PEER_TOPK):
        for i, (_, ids) in enumerate(problems):
            s = scores[i]
            m = jnp.max(s, axis=0, keepdims=True)
            idx = jnp.min(jnp.where(s == m, ids, _NO_ROW), axis=0, keepdims=True)
            hit = ids == idx
            emit(i, r, m, hit, idx)
            scores[i] = jnp.where(hit, -jnp.inf, s)


def _pair_candidates(sv0, sv1, si0, si1, sub):
    subf = sub.astype(F32)
    sums, experts, flat = [], [], []
    for a in range(SUBLANES):
        n_b = PEER_TOPK // (a + 1)
        for b0 in range(0, n_b, SUBLANES):
            s = sv0[a:a + 1, :] + sv1[b0:b0 + SUBLANES, :]
            if n_b - b0 < SUBLANES:
                s = jnp.where(sub < n_b - b0, s, -jnp.inf)
            sums.append(s)
            experts.append(si0[a:a + 1, :] * PEER_KEYS + si1[b0:b0 + SUBLANES, :])
            flat.append(subf + float(a * PEER_TOPK + b0))
    assert PEER_TOPK // (SUBLANES + 1) == 1
    sums.append(sv0[SUBLANES:, :] + sv1[0:1, :])
    experts.append(si0[SUBLANES:, :] * PEER_KEYS + si1[0:1, :])
    flat.append((subf + float(SUBLANES)) * float(PEER_TOPK))
    return jnp.concatenate(sums, axis=0), jnp.concatenate(experts, axis=0), jnp.concatenate(flat, axis=0)


def _route_sub_keys(q_ref, sk_ref, sv_ref, si_ref, tm):
    key_id = lax.broadcasted_iota(jnp.int32, (PEER_KEYS, tm), 0).astype(F32)

    def sub_keys(it, carry):
        hps = [it * ROUTE_INTERLEAVE + u for u in range(ROUTE_INTERLEAVE)]
        problems = [(_dot_nt(sk_ref[hp], q_ref[hp].astype(BF16)), key_id) for hp in hps]

        def emit(i, r, m, hit, idx):
            sv_ref[hps[i], r:r + 1, :] = m
            si_ref[hps[i], r:r + 1, :] = idx.astype(jnp.int32)

        _top16_rows(problems, emit)
        return carry

    lax.fori_loop(0, 2 * PEER_HEADS // ROUTE_INTERLEAVE, sub_keys, 0)


def _route_pairs(sv_ref, si_ref, e_ref, g_ref, ts_ref, te_ref, tm):
    sub = lax.broadcasted_iota(jnp.int32, (SUBLANES, tm), 0)

    def heads(it, carry):
        hs = [it * ROUTE_INTERLEAVE + u for u in range(ROUTE_INTERLEAVE)]
        problems, experts = [], []
        for h in hs:
            cs, ce, cid = _pair_candidates(sv_ref[2 * h], sv_ref[2 * h + 1], si_ref[2 * h], si_ref[2 * h + 1], sub)
            problems.append((cs, cid))
            experts.append(ce)

        def emit(i, r, m, hit, idx):
            ts_ref[i, r:r + 1, :] = m
            te_ref[i, r:r + 1, :] = jnp.max(jnp.where(hit, experts[i], -1), axis=0, keepdims=True)

        _top16_rows(problems, emit)
        for i, h in enumerate(hs):
            row0 = pl.multiple_of(h * PEER_TOPK, PEER_TOPK)
            e_ref[pl.ds(row0, PEER_TOPK), :] = te_ref[i]
            ts = ts_ref[i]
            ex = jnp.exp(ts - jnp.max(ts, axis=0, keepdims=True))
            g_ref[pl.ds(row0, PEER_TOPK), :] = ex / jnp.sum(ex, axis=0, keepdims=True)
        return carry

    lax.fori_loop(0, PEER_HEADS // ROUTE_INTERLEAVE, heads, 0)


def _peer_route_kernel(q_ref, sk_ref, e_ref, g_ref, sv_ref, si_ref, ts_ref, te_ref, *, tm):
    _route_sub_keys(q_ref, sk_ref, sv_ref, si_ref, tm)
    _route_pairs(sv_ref, si_ref, e_ref, g_ref, ts_ref, te_ref, tm)


def _peer_pairs_kernel(sv_ref, si_ref, e_ref, g_ref, ts_ref, te_ref, *, tm):
    _route_pairs(sv_ref, si_ref, e_ref, g_ref, ts_ref, te_ref, tm)


def _route_out(nblk, tm):
    shape = (jax.ShapeDtypeStruct((nblk, PEER_SLOTS, tm), jnp.int32), jax.ShapeDtypeStruct((nblk, PEER_SLOTS, tm), F32))
    specs = (pl.BlockSpec((None, PEER_SLOTS, tm), lambda i: (i, 0, 0)),
             pl.BlockSpec((None, PEER_SLOTS, tm), lambda i: (i, 0, 0)))
    scratch = [pltpu.VMEM((ROUTE_INTERLEAVE, PEER_TOPK, tm), F32), pltpu.VMEM((ROUTE_INTERLEAVE, PEER_TOPK, tm), jnp.int32)]
    return shape, specs, scratch


def _peer_route(q3, subkeys_bf16, n_tokens):
    tm = ROUTE_TILE
    nblk = n_tokens // tm
    out_shape, out_specs, out_scratch = _route_out(nblk, tm)
    return pl.pallas_call(
        functools.partial(_peer_route_kernel, tm=tm),
        out_shape=out_shape,
        grid=(nblk,),
        in_specs=[
            pl.BlockSpec((2 * PEER_HEADS, tm, PEER_HALF), lambda i: (0, i, 0)),
            pl.BlockSpec((2 * PEER_HEADS, PEER_KEYS, PEER_HALF), lambda i: (0, 0, 0)),
        ],
        out_specs=out_specs,
        scratch_shapes=[
            pltpu.VMEM((2 * PEER_HEADS, PEER_TOPK, tm), F32),
            pltpu.VMEM((2 * PEER_HEADS, PEER_TOPK, tm), jnp.int32),
        ] + out_scratch,
        compiler_params=_cparams(("parallel",)),
        name="peer_route",
    )(q3, subkeys_bf16)


def _peer_pairs(sv, si):
    nblk, _, _, tm = sv.shape
    out_shape, out_specs, out_scratch = _route_out(nblk, tm)
    list_spec = pl.BlockSpec((None, 2 * PEER_HEADS, PEER_TOPK, tm), lambda i: (i, 0, 0, 0))
    return pl.pallas_call(
        functools.partial(_peer_pairs_kernel, tm=tm),
        out_shape=out_shape,
        grid=(nblk,),
        in_specs=[list_spec, list_spec],
        out_specs=out_specs,
        scratch_shapes=out_scratch,
        compiler_params=_cparams(("parallel",)),
        name="peer_pairs",
    )(sv, si)


def _peer_apply_kernel(*refs, tb, side):
    if side:
        (idx_ref, gate_ref, h_ref, x_ref, gt_ref, uv_hbm, q_ref, sk_ref, o_ref, sv_ref, si_ref,
         buf_ref, sem_ref, part_ref, acc_ref, s_ref) = refs
    else:
        idx_ref, gate_ref, h_ref, x_ref, gt_ref, uv_hbm, o_ref, buf_ref, sem_ref, part_ref, acc_ref = refs
    lookahead = GATHER_SLOTS - 1
    n_groups = tb // TOKEN_GROUP
    n_tiles = D_MODEL // LANES
    n_slot_groups = PEER_SLOTS // SUBLANES

    def start_rows(tok, slot, k0, n):
        for k in range(k0, k0 + n):
            pltpu.make_async_copy(uv_hbm.at[idx_ref[tok, k]], buf_ref.at[slot, k],
                                  sem_ref.at[slot]).start(priority=k % 2)

    def wait_gather(slot):
        pltpu.make_async_copy(uv_hbm.at[pl.ds(0, PEER_SLOTS)], buf_ref.at[slot], sem_ref.at[slot]).wait()

    for tok in range(lookahead):
        start_rows(tok, tok % GATHER_SLOTS, 0, PEER_SLOTS)

    tok_lane = lax.broadcasted_iota(jnp.int32, (PEER_SLOTS, ROUTE_TILE), 1)
    sub = lax.broadcasted_iota(jnp.int32, (SUBLANES, LANES), 0)

    def fold(x0, x1, d):
        t0 = x0 + pltpu.roll(x0, SUBLANES - d, 0)
        t1 = x1 + pltpu.roll(x1, d, 0)
        return jnp.where((sub & d) == 0, t0, t1)

    key_id = lax.broadcasted_iota(jnp.int32, (PEER_KEYS, ROUTE_TILE), 0).astype(F32)
    groups_per_list_pair = PEER_TOPK // TOKEN_GROUP
    groups_per_tile = groups_per_list_pair * PEER_HEADS
    if side:
        assert n_groups * TOKEN_GROUP == tb and tb % ROUTE_TILE == 0 and groups_per_tile * TOKEN_GROUP == ROUTE_TILE

    def group(gi, is_last):
        base = gi * TOKEN_GROUP if is_last else pl.multiple_of(gi * TOKEN_GROUP, TOKEN_GROUP)
        h8 = h_ref[pl.ds(base, TOKEN_GROUP), :]
        rows = []
        if side:
            tile = gi // groups_per_tile
            hp0 = 2 * ((gi // groups_per_list_pair) % PEER_HEADS)
            r0 = (gi % groups_per_list_pair) * TOKEN_GROUP

            def load_scores():
                rows0 = tile * ROUTE_TILE if is_last else pl.multiple_of(tile * ROUTE_TILE, ROUTE_TILE)
                for pi in range(2):
                    s_ref[pi] = _dot_nt(sk_ref[hp0 + pi], q_ref[hp0 + pi, pl.ds(rows0, ROUTE_TILE), :].astype(BF16))

            if is_last:
                if r0 == 0:
                    load_scores()
            else:
                pl.when(r0 == 0)(load_scores)
        for j in range(TOKEN_GROUP):
            tok = base + j
            slot = j % GATHER_SLOTS
            prefetch = not is_last or j + lookahead < TOKEN_GROUP

            def start_ahead(k0, n, tok=tok, j=j, prefetch=prefetch):
                if prefetch:
                    start_rows(tok + lookahead, (j + lookahead) % GATHER_SLOTS, k0, n)

            wait_gather(slot)
            h_tile = jnp.concatenate([h8[j:j + 1, s * LANES:(s + 1) * LANES] for s in range(n_tiles)], axis=0)
            for g in range(n_slot_groups):
                p = [lax.bitcast_convert_type(buf_ref[slot, g * SUBLANES + c] & jnp.uint32(0xFFFF0000), F32) * h_tile
                     for c in range(SUBLANES)]
                y = [fold(p[c], p[c + 4], 4) for c in range(4)]
                z = [fold(y[c], y[c + 2], 2) for c in range(2)]
                part_ref[g * SUBLANES:(g + 1) * SUBLANES, :] = fold(z[0], z[1], 1)
                start_ahead(g * ISSUE_PER_GROUP, ISSUE_PER_GROUP)
            acc_ref[...] = jnp.zeros(acc_ref.shape, F32)
            start_ahead(n_slot_groups * ISSUE_PER_GROUP, PEER_SLOTS - 2 * n_slot_groups * ISSUE_PER_GROUP)
            if side:
                for pi in range(2):
                    sc = s_ref[pi]
                    m = jnp.max(sc, axis=0, keepdims=True)
                    idx = jnp.min(jnp.where(sc == m, key_id, _NO_ROW), axis=0, keepdims=True)
                    sv_ref[tile, hp0 + pi, r0 + j] = m
                    si_ref[tile, hp0 + pi, r0 + j] = idx.astype(jnp.int32)
                    s_ref[pi] = jnp.where(key_id == idx, -jnp.inf, sc)
            act = jnp.sum(part_ref[...], axis=1, keepdims=True)
            gates = gate_ref[tok // ROUTE_TILE]
            gate = jnp.sum(jnp.where(tok_lane == tok % ROUTE_TILE, gates, 0.0), axis=1, keepdims=True)
            w = gate * _gelu(act)
            for g in range(n_slot_groups):
                terms = [jnp.broadcast_to(w[k:k + 1, :], (SUBLANES, LANES))
                         * lax.bitcast_convert_type(buf_ref[slot, k] << 16, F32)
                         for k in range(g * SUBLANES, (g + 1) * SUBLANES)]
                s4 = [terms[i] + terms[i + 4] for i in range(4)]
                acc_ref[g % 2] = acc_ref[g % 2] + ((s4[0] + s4[1]) + (s4[2] + s4[3]))
                start_ahead(PEER_SLOTS - (n_slot_groups - g) * ISSUE_PER_GROUP, ISSUE_PER_GROUP)
            acc = acc_ref[0] + acc_ref[1]
            rows.append(jnp.concatenate([acc[s:s + 1, :] for s in range(n_tiles)], axis=1))
        out8 = jnp.concatenate(rows, axis=0)
        o_ref[pl.ds(base, TOKEN_GROUP), :] = x_ref[pl.ds(base, TOKEN_GROUP), :] + gt_ref[...] * out8

    def body(gi, carry):
        group(gi, False)
        return carry

    lax.fori_loop(0, n_groups - 1, body, 0)
    group(n_groups - 1, True)


def _peer_apply(experts_tok, gates, h2d, x2d, gt, uv_table, n_tokens, row0, rows_per_mod, side=None):
    tb = min(APPLY_TILE, n_tokens)
    tiles_per_mod = rows_per_mod // tb
    tile0 = row0 // tb
    in_specs = [
        pl.BlockSpec((tb, PEER_SLOTS), lambda i: (i, 0), memory_space=pltpu.SMEM),
        pl.BlockSpec((tb // ROUTE_TILE, PEER_SLOTS, ROUTE_TILE), lambda i: (i, 0, 0)),
        pl.BlockSpec((tb, D_MODEL), lambda i: (tile0 + i, 0)),
        pl.BlockSpec((tb, D_MODEL), lambda i: (tile0 + i, 0)),
        pl.BlockSpec((None, 1, D_MODEL), lambda i: ((tile0 + i) // tiles_per_mod, 0, 0)),
        pl.BlockSpec(memory_space=pl.ANY),
    ]
    args = [experts_tok, gates, h2d, x2d, gt, uv_table]
    out_shape = [jax.ShapeDtypeStruct((n_tokens, D_MODEL), F32)]
    out_specs = [pl.BlockSpec((tb, D_MODEL), lambda i: (i, 0))]
    scratch = [
        pltpu.VMEM((GATHER_SLOTS, PEER_SLOTS, SUBLANES, LANES), jnp.uint32),
        pltpu.SemaphoreType.DMA((GATHER_SLOTS,)),
        pltpu.VMEM((PEER_SLOTS, LANES), F32),
        pltpu.VMEM((2, SUBLANES, LANES), F32),
    ]
    if side is not None:
        q3, subkeys, side_row0 = side
        side_tile0 = side_row0 // tb
        in_specs += [
            pl.BlockSpec((2 * PEER_HEADS, tb, PEER_HALF), lambda i: (0, side_tile0 + i, 0)),
            pl.BlockSpec((2 * PEER_HEADS, PEER_KEYS, PEER_HALF), lambda i: (0, 0, 0)),
        ]
        args += [q3, subkeys]
        list_shape = (n_tokens // ROUTE_TILE, 2 * PEER_HEADS, PEER_TOPK, 1, ROUTE_TILE)
        list_block = (tb // ROUTE_TILE, 2 * PEER_HEADS, PEER_TOPK, 1, ROUTE_TILE)
        out_shape += [jax.ShapeDtypeStruct(list_shape, F32), jax.ShapeDtypeStruct(list_shape, jnp.int32)]
        out_specs += [pl.BlockSpec(list_block, lambda i: (i, 0, 0, 0, 0))] * 2
        scratch.append(pltpu.VMEM((2, PEER_KEYS, ROUTE_TILE), F32))
    outs = pl.pallas_call(
        functools.partial(_peer_apply_kernel, tb=tb, side=side is not None),
        out_shape=tuple(out_shape),
        grid=(n_tokens // tb,),
        in_specs=in_specs,
        out_specs=tuple(out_specs),
        scratch_shapes=scratch,
        compiler_params=_cparams(("arbitrary",), disable_bounds_checks=True),
        name="peer_apply_side" if side is not None else "peer_apply",
    )(*args)
    if side is None:
        return outs[0]
    sv, si = (a.reshape(a.shape[:3] + a.shape[4:]) for a in outs[1:])
    return outs[0], sv, si


def _final_norm_kernel(x_ref, g_ref, o_ref):
    x = x_ref[...]
    o_ref[...] = (x * lax.rsqrt(jnp.mean(x * x, axis=-1, keepdims=True) + EPS)) * g_ref[...]


def _final_norm(x2d, g):
    t = x2d.shape[0]
    tm = min(2 * TOKEN_TILE, t)
    return pl.pallas_call(
        _final_norm_kernel,
        out_shape=jax.ShapeDtypeStruct((t, D_MODEL), F32),
        grid=(t // tm,),
        in_specs=[pl.BlockSpec((tm, D_MODEL), lambda i: (i, 0)), pl.BlockSpec((1, D_MODEL), lambda i: (0, 0))],
        out_specs=pl.BlockSpec((tm, D_MODEL), lambda i: (i, 0)),
        compiler_params=_cparams(("parallel",)),
        name="final_norm",
    )(x2d, g)


def _rope_tables(length):
    rows = length // GRID_W
    row = jnp.repeat(jnp.arange(rows, dtype=F32), GRID_W)
    col = jnp.tile(jnp.arange(GRID_W, dtype=F32), rows)
    inv = ROPE_BASE ** (-jnp.arange(ROPE_FREQS, dtype=F32) / ROPE_FREQS)
    ang = jnp.stack([row[:, None] * inv, col[:, None] * inv], axis=1)
    cos, sin = jnp.cos(ang), jnp.sin(ang)
    cos_h = jnp.broadcast_to(cos[:, :, None, :], (length, 2, 2, ROPE_FREQS)).reshape(length, HEAD_DIM)
    sin_h = jnp.stack([-sin, sin], axis=2).reshape(length, HEAD_DIM)
    reps = LANES // HEAD_DIM
    return jnp.tile(cos_h, (1, reps)), jnp.tile(sin_h, (1, reps))


def _pack_expert_tables(u_tab, v_tab):
    def bits(a):
        return lax.bitcast_convert_type(a.astype(BF16), jnp.uint16).astype(jnp.uint32)

    return ((bits(u_tab) << 16) | bits(v_tab)).reshape(-1, D_MODEL // LANES, LANES)


def _experts_by_token(experts):
    return jnp.transpose(experts, (0, 2, 1)).reshape(-1, PEER_SLOTS)


def _peer_layer(x2d, g, sc, sh, gt, rows_per_mod, wq, subkeys, uv_table, n_chunks):
    t = x2d.shape[0]
    q3, h2d = _peer_query(x2d, g, sc, sh, wq, rows_per_mod)
    n = t // n_chunks
    experts, gates = _peer_route(q3, subkeys, n)
    outs = []
    for c in range(n_chunks):
        side = (q3, subkeys, (c + 1) * n) if c + 1 < n_chunks else None
        res = _peer_apply(_experts_by_token(experts), gates, h2d, x2d, gt, uv_table, n, c * n, rows_per_mod, side)
        if side is None:
            outs.append(res)
        else:
            outs.append(res[0])
            experts, gates = _peer_pairs(res[1], res[2])
    return outs[0] if n_chunks == 1 else jnp.concatenate(outs, axis=0)


def kernel(x, c, ctx, c_ctx, w_mod, b_mod, g_mix, g_ffn, w_in, attn_sink, pool_w, pool_scale, sgu_w, sgu_b,
           w_br_attn, w_br_pool, w_br_sgu, w_out, peer_wq, peer_subkeys, peer_u, peer_v, g_final):
    b, l, d = x.shape
    n_ctx = ctx.shape[1]
    depth = w_mod.shape[0]
    assert d == D_MODEL and l % TOKEN_TILE == 0 and n_ctx % SGU_CHUNK == 0

    n_rows = -(-(b + 1) // SUBLANES) * SUBLANES
    cond = jnp.zeros((n_rows, d), F32).at[:b].set(c).at[b].set(c_ctx)
    mod = _adaln(cond, w_mod, b_mod).reshape(depth, n_rows, N_MOD, 1, d)

    rope = _rope_tables(l)
    x2d = x.reshape(b * l, d)
    xc2d = ctx.reshape(b * n_ctx, d)

    for layer in range(depth):
        last = layer == depth - 1
        m_lat = [mod[layer, :b, i] for i in range(N_MOD)]
        m_ctx = [mod[layer, b:b + 1, i] for i in range(N_MOD)]
        g1 = g_mix[layer].reshape(1, d)
        g2 = g_ffn[layer].reshape(1, d)

        w_in_b = w_in[layer].astype(BF16)
        qkv, rest = _in_proj(x2d, g1, m_lat[1], m_lat[0], w_in_b, l, rope, l)
        qkv_c, rest_c = _in_proj(xc2d, g1, m_ctx[1], m_ctx[0], w_in_b, b * n_ctx, None, n_ctx)

        y_attn = _attention(qkv.reshape(b, l, QKV_W), qkv_c.reshape(b, n_ctx, QKV_W), attn_sink[layer], True)

        poolw_bd = jnp.zeros((POOL_W, POOL_W), F32)
        for gi in range(len(POOL_SIZES)):
            sl = slice(gi * POOL_GROUP_W, (gi + 1) * POOL_GROUP_W)
            poolw_bd = poolw_bd.at[sl, sl].set(pool_w[layer, gi])
        poolw_bd = poolw_bd.astype(BF16)
        pscale = pool_scale[layer].reshape(1, POOL_W)
        sguw = sgu_w[layer].astype(BF16)
        sgub_full = jnp.repeat(sgu_b[layer].T, SGU_GROUP_W, axis=1)
        wbr = jnp.concatenate([w_br_attn[layer], w_br_pool[layer], w_br_sgu[layer]], axis=0).astype(BF16)
        wout = w_out[layer].astype(BF16)
        mix_w = (poolw_bd, pscale, sguw, sgub_full, wbr, wout)

        x2d = _mixer(rest, y_attn.reshape(b * l, ATTN_W), x2d, m_lat[2], *mix_w, l, l)
        if not last:
            y_attn_c = _attention(qkv_c.reshape(b, n_ctx, QKV_W), qkv_c.reshape(b, n_ctx, QKV_W),
                                  attn_sink[layer], False)
            xc2d = _mixer(rest_c, y_attn_c.reshape(b * n_ctx, ATTN_W), xc2d, m_ctx[2], *mix_w,
                          b * n_ctx, n_ctx)

        wq = peer_wq[layer].astype(BF16)
        subkeys = peer_subkeys[layer].reshape(2 * PEER_HEADS, PEER_KEYS, PEER_HALF).astype(BF16)
        uv_table = _pack_expert_tables(peer_u[layer], peer_v[layer])
        x2d = _peer_layer(x2d, g2, m_lat[4], m_lat[3], m_lat[5], l, wq, subkeys, uv_table, b)
        if not last:
            xc2d = _peer_layer(xc2d, g2, m_ctx[4], m_ctx[3], m_ctx[5], b * n_ctx, wq, subkeys, uv_table, 1)

    return _final_norm(x2d, g_final.reshape(1, d)).reshape(b, l, d)
```

```python
import functools

import jax
import jax.numpy as jnp
from jax import lax
from jax.experimental import pallas as pl
from jax.experimental.pallas import tpu as pltpu

F32 = jnp.float32
BF16 = jnp.bfloat16

D_MODEL = 1024
EPS = 1e-6
N_MOD = 6
GRID_W = 64

N_HEADS = 8
N_KV_HEADS = 2
GQA_GROUP = N_HEADS // N_KV_HEADS
HEAD_DIM = 64
WINDOW = 128
ATTN_BLOCK = 128
ATTN_W = N_HEADS * HEAD_DIM
KV_W = N_KV_HEADS * HEAD_DIM
QKV_W = ATTN_W + 2 * KV_W
ROPE_BASE = 10000.0
ROPE_FREQS = HEAD_DIM // 4

POOL_SIZES = (2, 4, 8, 16)
POOL_GROUP_W = 64
POOL_W = len(POOL_SIZES) * POOL_GROUP_W
POOL_HALO = 8

SGU_CHUNK = 128
SGU_GROUPS = 4
SGU_W = 256
SGU_GROUP_W = SGU_W // SGU_GROUPS

N_BRANCH = 3
REST_W = POOL_W + 2 * SGU_W + N_BRANCH * D_MODEL
IN_W = QKV_W + REST_W
BR_W = ATTN_W + POOL_W + SGU_W

PEER_HEADS = 8
PEER_KEYS = 128
PEER_HALF = 128
PEER_TOPK = 16
PEER_SLOTS = PEER_HEADS * PEER_TOPK
PEER_QW = PEER_HEADS * 2 * PEER_HALF

LANES = 128
SUBLANES = 8
VMEM_LIMIT = 48 * 1024 * 1024

TOKEN_TILE = 256
ROUTE_TILE = 128
APPLY_TILE = 256
GATHER_SLOTS = 8
TOKEN_GROUP = 8
ISSUE_PER_GROUP = 3
ROUTE_INTERLEAVE = 2
_NO_ROW = 1e9


def _cparams(sem, **kw):
    return pltpu.CompilerParams(dimension_semantics=sem, vmem_limit_bytes=VMEM_LIMIT, **kw)


def _dot(a, b):
    return jnp.dot(a, b, preferred_element_type=F32)


def _dot_nt(a, b):
    return lax.dot_general(a, b, (((1,), (1,)), ((), ())), preferred_element_type=F32)


def _split_bf16(a):
    hi = a.astype(BF16)
    lo = (a - hi.astype(F32)).astype(BF16)
    return hi, lo


def _adaln_kernel(c_ref, w_ref, b_ref, o_ref):
    c = c_ref[...]
    s = c * (1.0 / (1.0 + jnp.exp(-c)))
    s_hi, s_lo = _split_bf16(s)
    w_hi, w_lo = _split_bf16(w_ref[...])
    acc = _dot(s_hi, w_hi) + (_dot(s_hi, w_lo) + _dot(s_lo, w_hi))
    o_ref[...] = acc + b_ref[...]


def _adaln(cond, w_mod, b_mod):
    depth = w_mod.shape[0]
    r = cond.shape[0]
    n = w_mod.shape[2]
    tn = D_MODEL
    return pl.pallas_call(
        _adaln_kernel,
        out_shape=jax.ShapeDtypeStruct((depth, r, n), F32),
        grid=(depth, n // tn),
        in_specs=[
            pl.BlockSpec((r, D_MODEL), lambda l, j: (0, 0)),
            pl.BlockSpec((None, D_MODEL, tn), lambda l, j: (l, 0, j)),
            pl.BlockSpec((None, 1, tn), lambda l, j: (l, 0, j)),
        ],
        out_specs=pl.BlockSpec((None, r, tn), lambda l, j: (l, 0, j)),
        compiler_params=_cparams(("arbitrary", "arbitrary")),
        name="adaln",
    )(cond, w_mod, b_mod.reshape(depth, 1, n))


def _modulated_norm(x, g, sc, sh):
    y = x * lax.rsqrt(jnp.mean(x * x, axis=-1, keepdims=True) + EPS)
    return (y * g) * (1.0 + sc) + sh


def _rope_tile(x, cos, sin_signed):
    lane = lax.broadcasted_iota(jnp.int32, x.shape, 1)
    first_half = (lane % (2 * ROPE_FREQS)) < ROPE_FREQS
    partner = jnp.where(first_half, pltpu.roll(x, LANES - ROPE_FREQS, 1), pltpu.roll(x, ROPE_FREQS, 1))
    return x * cos + partner * sin_signed


def _in_proj_kernel(x_ref, g_ref, sc_ref, sh_ref, w_ref, *rest, rope):
    if rope:
        cos_ref, sin_ref, qkv_ref, rest_ref = rest
    else:
        qkv_ref, rest_ref = rest
    hb = _modulated_norm(x_ref[...], g_ref[...], sc_ref[...], sh_ref[...]).astype(BF16)
    for c2 in range(QKV_W // (2 * LANES)):
        acc2 = _dot(hb, w_ref[:, c2 * 2 * LANES:(c2 + 1) * 2 * LANES])
        for half in range(2):
            c = 2 * c2 + half
            acc = acc2[:, half * LANES:(half + 1) * LANES]
            if rope and c * LANES < ATTN_W + KV_W:
                acc = _rope_tile(acc, cos_ref[...], sin_ref[...])
            qkv_ref[:, c * LANES:(c + 1) * LANES] = acc
    chunk = 768
    for c in range(REST_W // chunk):
        rest_ref[:, c * chunk:(c + 1) * chunk] = _dot(hb, w_ref[:, QKV_W + c * chunk:QKV_W + (c + 1) * chunk])


def _in_proj(x2d, g, sc, sh, w_bf16, rows_per_mod, rope_tables, seq_len):
    t = x2d.shape[0]
    tm = min(TOKEN_TILE, t)
    tiles_per_mod = rows_per_mod // tm
    rope = rope_tables is not None
    in_specs = [
        pl.BlockSpec((tm, D_MODEL), lambda i: (i, 0)),
        pl.BlockSpec((1, D_MODEL), lambda i: (0, 0)),
        pl.BlockSpec((None, 1, D_MODEL), lambda i: (i // tiles_per_mod, 0, 0)),
        pl.BlockSpec((None, 1, D_MODEL), lambda i: (i // tiles_per_mod, 0, 0)),
        pl.BlockSpec((D_MODEL, IN_W), lambda i: (0, 0)),
    ]
    args = [x2d, g, sc, sh, w_bf16]
    if rope:
        tiles_per_seq = seq_len // tm
        in_specs += [pl.BlockSpec((tm, LANES), lambda i: (i % tiles_per_seq, 0))] * 2
        args += list(rope_tables)
    return pl.pallas_call(
        functools.partial(_in_proj_kernel, rope=rope),
        out_shape=(jax.ShapeDtypeStruct((t, QKV_W), F32), jax.ShapeDtypeStruct((t, REST_W), F32)),
        grid=(t // tm,),
        in_specs=in_specs,
        out_specs=(pl.BlockSpec((tm, QKV_W), lambda i: (i, 0)), pl.BlockSpec((tm, REST_W), lambda i: (i, 0))),
        compiler_params=_cparams(("parallel",)),
        name="in_proj",
    )(*args)


def _peer_query_kernel(x_ref, g_ref, sc_ref, sh_ref, w_ref, q_ref, h_ref):
    h = _modulated_norm(x_ref[...], g_ref[...], sc_ref[...], sh_ref[...])
    h_ref[...] = h
    hb = h.astype(BF16)
    for head in range(PEER_HEADS):
        acc = _dot(hb, w_ref[:, head * 2 * PEER_HALF:(head + 1) * 2 * PEER_HALF])
        q_ref[2 * head] = acc[:, :PEER_HALF]
        q_ref[2 * head + 1] = acc[:, PEER_HALF:]


def _peer_query(x2d, g, sc, sh, wq_bf16, rows_per_mod):
    t = x2d.shape[0]
    tm = min(TOKEN_TILE, t)
    tiles_per_mod = rows_per_mod // tm
    return pl.pallas_call(
        _peer_query_kernel,
        out_shape=(jax.ShapeDtypeStruct((2 * PEER_HEADS, t, PEER_HALF), F32),
                   jax.ShapeDtypeStruct((t, D_MODEL), F32)),
        grid=(t // tm,),
        in_specs=[
            pl.BlockSpec((tm, D_MODEL), lambda i: (i, 0)),
            pl.BlockSpec((1, D_MODEL), lambda i: (0, 0)),
            pl.BlockSpec((None, 1, D_MODEL), lambda i: (i // tiles_per_mod, 0, 0)),
            pl.BlockSpec((None, 1, D_MODEL), lambda i: (i // tiles_per_mod, 0, 0)),
            pl.BlockSpec((D_MODEL, PEER_QW), lambda i: (0, 0)),
        ],
        out_specs=(pl.BlockSpec((2 * PEER_HEADS, tm, PEER_HALF), lambda i: (0, i, 0)),
                   pl.BlockSpec((tm, D_MODEL), lambda i: (i, 0))),
        compiler_params=_cparams(("parallel",)),
        name="peer_query",
    )(x2d, g, sc, sh, wq_bf16)


def _attn_kernel(sink_ref, q_ref, *rest, window, n_blocks):
    if window:
        kvp_ref, kvc_ref, kvn_ref, ctx_ref, o_ref = rest
    else:
        ctx_ref, o_ref = rest
    n = pl.program_id(1)
    q = q_ref[...] * (HEAD_DIM ** -0.5)
    sources = []
    if window:
        qi = lax.broadcasted_iota(jnp.int32, (GQA_GROUP * ATTN_BLOCK, ATTN_BLOCK), 0) % ATTN_BLOCK
        kj = lax.broadcasted_iota(jnp.int32, (GQA_GROUP * ATTN_BLOCK, ATTN_BLOCK), 1)
        sources.append((kvp_ref[...], (kj >= qi) & (n > 0)))
        sources.append((kvc_ref[...], None))
        sources.append((kvn_ref[...], (kj <= qi) & (n < n_blocks - 1)))
    for c in range(ctx_ref.shape[0] // ATTN_BLOCK):
        sources.append((ctx_ref[c * ATTN_BLOCK:(c + 1) * ATTN_BLOCK, :], None))
    kb = [[kv[:, g * HEAD_DIM:(g + 1) * HEAD_DIM].astype(BF16) for g in range(N_KV_HEADS)] for kv, _ in sources]
    ones = jnp.ones((ATTN_BLOCK, HEAD_DIM), BF16)
    vb = [[jnp.concatenate([kv[:, KV_W + g * HEAD_DIM:KV_W + (g + 1) * HEAD_DIM].astype(BF16), ones], axis=1)
           for g in range(N_KV_HEADS)] for kv, _ in sources]
    groups = [range(g * GQA_GROUP, (g + 1) * GQA_GROUP) for g in range(N_KV_HEADS)]
    sinks = [jnp.concatenate([jnp.full((ATTN_BLOCK, 1), sink_ref[h], F32) for h in heads], axis=0)
             for heads in groups]
    scores = []
    for g, heads in enumerate(groups):
        qg = jnp.concatenate([q[:, h * HEAD_DIM:(h + 1) * HEAD_DIM] for h in heads], axis=0).astype(BF16)
        sg = []
        for si, (_, valid) in enumerate(sources):
            s = _dot_nt(qg, kb[si][g])
            sg.append(s if valid is None else jnp.where(valid, s, -jnp.inf))
        scores.append(sg)
    maxes = [jnp.maximum(jnp.max(functools.reduce(jnp.maximum, scores[g]), axis=-1, keepdims=True), sinks[g])
             for g in range(N_KV_HEADS)]
    outs = []
    for g in range(N_KV_HEADS):
        m = maxes[g]
        acc = jnp.zeros((GQA_GROUP * ATTN_BLOCK, 2 * HEAD_DIM), F32)
        for si, s in enumerate(scores[g]):
            acc = acc + _dot(jnp.exp(s - m).astype(BF16), vb[si][g])
        og = acc[:, :HEAD_DIM] / (acc[:, HEAD_DIM:] + jnp.exp(sinks[g] - m))
        outs += [og[i * ATTN_BLOCK:(i + 1) * ATTN_BLOCK, :] for i in range(GQA_GROUP)]
    o_ref[...] = jnp.concatenate(outs, axis=-1)


def _attention(qkv, kv_ctx, sink, window):
    b, l, _ = qkv.shape
    n_ctx = kv_ctx.shape[1]
    nb = l // ATTN_BLOCK
    kv_col = ATTN_W // (2 * KV_W)
    in_specs = [
        pl.BlockSpec(memory_space=pltpu.SMEM),
        pl.BlockSpec((None, ATTN_BLOCK, ATTN_W), lambda bi, n: (bi, n, 0)),
    ]
    args = [sink, qkv]
    if window:
        in_specs += [
            pl.BlockSpec((None, ATTN_BLOCK, 2 * KV_W), lambda bi, n: (bi, jnp.maximum(n - 1, 0), kv_col)),
            pl.BlockSpec((None, ATTN_BLOCK, 2 * KV_W), lambda bi, n: (bi, n, kv_col)),
            pl.BlockSpec((None, ATTN_BLOCK, 2 * KV_W), lambda bi, n: (bi, jnp.minimum(n + 1, nb - 1), kv_col)),
        ]
        args += [qkv, qkv, qkv]
    in_specs.append(pl.BlockSpec((None, n_ctx, 2 * KV_W), lambda bi, n: (bi, 0, kv_col)))
    args.append(kv_ctx)
    return pl.pallas_call(
        functools.partial(_attn_kernel, window=window, n_blocks=nb),
        out_shape=jax.ShapeDtypeStruct((b, l, ATTN_W), F32),
        grid=(b, nb),
        in_specs=in_specs,
        out_specs=pl.BlockSpec((None, ATTN_BLOCK, ATTN_W), lambda bi, n: (bi, n, 0)),
        compiler_params=_cparams(("parallel", "parallel")),
        name="attention_window" if window else "attention_context",
    )(*args)


def _gelu(x):
    return jax.nn.gelu(x, approximate=True)


def _sigmoid(x):
    return 1.0 / (1.0 + jnp.exp(-x))


def _mixer_kernel(rest_ref, hprev_ref, hnext_ref, ya_ref, x_ref, gt_ref, poolw_ref, pscale_ref, sguw_ref,
                  sgub_ref, wbr_ref, wout_ref, o_ref, ext_ref, *, tm, tiles_per_seq, seq_len):
    ti = pl.program_id(0) % tiles_per_seq
    z = rest_ref[:, 0:POOL_W]
    ext_ref[0:POOL_HALO, :] = jnp.where(ti > 0, hprev_ref[...], 0.0)
    ext_ref[POOL_HALO:POOL_HALO + tm, :] = z
    ext_ref[POOL_HALO + tm:2 * POOL_HALO + tm, :] = jnp.where(ti < tiles_per_seq - 1, hnext_ref[...], 0.0)
    pos = ti * tm + lax.broadcasted_iota(jnp.int32, (tm, LANES), 0)
    lane = lax.broadcasted_iota(jnp.int32, (tm, LANES), 1)

    def count(size):
        hi = jnp.minimum(pos + size // 2, seq_len)
        lo = jnp.maximum(pos - size // 2, 0)
        return (hi - lo).astype(F32)

    diffs = []
    for lt in range(POOL_W // LANES):
        cols = slice(lt * LANES, (lt + 1) * LANES)

        def shifted(d, cols=cols):
            return ext_ref[POOL_HALO + d:POOL_HALO + d + tm, cols]

        small, large = POOL_SIZES[2 * lt], POOL_SIZES[2 * lt + 1]
        s = shifted(-1) + shifted(0)
        width = 2
        sums = {}
        while width <= large:
            sums[width] = s
            half = width // 2
            if 2 * width <= large:
                for d in range(half, width):
                    s = s + shifted(-d - 1) + shifted(d)
            width *= 2
        mean = jnp.where(lane < POOL_GROUP_W, sums[small] / count(small), sums[large] / count(large))
        diffs.append(mean - z[:, cols])
    d = jnp.concatenate(diffs, axis=-1).astype(BF16)
    y_pool = _dot(d, poolw_ref[...]) * pscale_ref[...]

    u = _gelu(rest_ref[:, POOL_W:POOL_W + SGU_W])
    v = _gelu(rest_ref[:, POOL_W + SGU_W:POOL_W + 2 * SGU_W])
    vn = (v * lax.rsqrt(jnp.mean(v * v, axis=-1, keepdims=True) + EPS)).astype(BF16)
    group = lax.broadcasted_iota(jnp.int32, (SGU_CHUNK, SGU_W), 1) // SGU_GROUP_W
    mixed = []
    for c in range(tm // SGU_CHUNK):
        vc = vn[c * SGU_CHUNK:(c + 1) * SGU_CHUNK, :]
        mc = sgub_ref[...]
        for hg in range(SGU_GROUPS):
            mc = mc + jnp.where(group == hg, _dot(sguw_ref[hg], vc), 0.0)
        mixed.append(mc)
    y_sgu = u * jnp.concatenate(mixed, axis=0)

    g_off = POOL_W + 2 * SGU_W
    merged = _sigmoid(rest_ref[:, g_off:g_off + D_MODEL]) * _dot(ya_ref[...].astype(BF16), wbr_ref[0:ATTN_W, :])
    merged = merged + _sigmoid(rest_ref[:, g_off + D_MODEL:g_off + 2 * D_MODEL]) * _dot(
        y_pool.astype(BF16), wbr_ref[ATTN_W:ATTN_W + POOL_W, :])
    merged = merged + _sigmoid(rest_ref[:, g_off + 2 * D_MODEL:g_off + 3 * D_MODEL]) * _dot(
        y_sgu.astype(BF16), wbr_ref[ATTN_W + POOL_W:BR_W, :])
    o_ref[...] = x_ref[...] + gt_ref[...] * _dot(merged.astype(BF16), wout_ref[...])


def _mixer(rest, y_attn, x2d, gt, poolw_bd, pscale, sguw, sgub_full, wbr, wout, rows_per_mod, seq_len):
    t = x2d.shape[0]
    tm = min(TOKEN_TILE, seq_len)
    tiles_per_seq = seq_len // tm
    tiles_per_mod = rows_per_mod // tm
    halo_blocks = tm // POOL_HALO
    n_halo = t // POOL_HALO
    kern = functools.partial(_mixer_kernel, tm=tm, tiles_per_seq=tiles_per_seq, seq_len=seq_len)
    return pl.pallas_call(
        kern,
        out_shape=jax.ShapeDtypeStruct((t, D_MODEL), F32),
        grid=(t // tm,),
        in_specs=[
            pl.BlockSpec((tm, REST_W), lambda i: (i, 0)),
            pl.BlockSpec((POOL_HALO, POOL_W), lambda i: (jnp.maximum(i * halo_blocks - 1, 0), 0)),
            pl.BlockSpec((POOL_HALO, POOL_W), lambda i: (jnp.minimum((i + 1) * halo_blocks, n_halo - 1), 0)),
            pl.BlockSpec((tm, ATTN_W), lambda i: (i, 0)),
            pl.BlockSpec((tm, D_MODEL), lambda i: (i, 0)),
            pl.BlockSpec((None, 1, D_MODEL), lambda i: (i // tiles_per_mod, 0, 0)),
            pl.BlockSpec((POOL_W, POOL_W), lambda i: (0, 0)),
            pl.BlockSpec((1, POOL_W), lambda i: (0, 0)),
            pl.BlockSpec((SGU_GROUPS, SGU_CHUNK, SGU_CHUNK), lambda i: (0, 0, 0)),
            pl.BlockSpec((SGU_CHUNK, SGU_W), lambda i: (0, 0)),
            pl.BlockSpec((BR_W, D_MODEL), lambda i: (0, 0)),
            pl.BlockSpec((D_MODEL, D_MODEL), lambda i: (0, 0)),
        ],
        out_specs=pl.BlockSpec((tm, D_MODEL), lambda i: (i, 0)),
        scratch_shapes=[pltpu.VMEM((tm + 2 * POOL_HALO, POOL_W), F32)],
        compiler_params=_cparams(("parallel",)),
        name="mixer",
    )(rest, rest, rest, y_attn, x2d, gt, poolw_bd, pscale, sguw, sgub_full, wbr, wout)


def _top16_rows(problems, emit):
    scores = [p[0] for p in problems]
    for r in range(PEER_TOPK):
        for i, (_, ids) in enumerate(problems):
            s = scores[i]
            m = jnp.max(s, axis=0, keepdims=True)
            idx = jnp.min(jnp.where(s == m, ids, _NO_ROW), axis=0, keepdims=True)
            hit = ids == idx
            emit(i, r, m, hit, idx)
            scores[i] = jnp.where(hit, -jnp.inf, s)


def _pair_candidates(sv0, sv1, si0, si1, sub):
    subf = sub.astype(F32)
    sums, experts, flat = [], [], []
    for a in range(SUBLANES):
        n_b = PEER_TOPK // (a + 1)
        for b0 in range(0, n_b, SUBLANES):
            s = sv0[a:a + 1, :] + sv1[b0:b0 + SUBLANES, :]
            if n_b - b0 < SUBLANES:
                s = jnp.where(sub < n_b - b0, s, -jnp.inf)
            sums.append(s)
            experts.append(si0[a:a + 1, :] * PEER_KEYS + si1[b0:b0 + SUBLANES, :])
            flat.append(subf + float(a * PEER_TOPK + b0))
    assert PEER_TOPK // (SUBLANES + 1) == 1
    sums.append(sv0[SUBLANES:, :] + sv1[0:1, :])
    experts.append(si0[SUBLANES:, :] * PEER_KEYS + si1[0:1, :])
    flat.append((subf + float(SUBLANES)) * float(PEER_TOPK))
    return jnp.concatenate(sums, axis=0), jnp.concatenate(experts, axis=0), jnp.concatenate(flat, axis=0)


def _route_sub_keys(q_ref, sk_ref, sv_ref, si_ref, tm):
    key_id = lax.broadcasted_iota(jnp.int32, (PEER_KEYS, tm), 0).astype(F32)

    def sub_keys(it, carry):
        hps = [it * ROUTE_INTERLEAVE + u for u in range(ROUTE_INTERLEAVE)]
        problems = [(_dot_nt(sk_ref[hp], q_ref[hp].astype(BF16)), key_id) for hp in hps]

        def emit(i, r, m, hit, idx):
            sv_ref[hps[i], r:r + 1, :] = m
            si_ref[hps[i], r:r + 1, :] = idx.astype(jnp.int32)

        _top16_rows(problems, emit)
        return carry

    lax.fori_loop(0, 2 * PEER_HEADS // ROUTE_INTERLEAVE, sub_keys, 0)


def _route_pairs(sv_ref, si_ref, e_ref, g_ref, ts_ref, te_ref, tm):
    sub = lax.broadcasted_iota(jnp.int32, (SUBLANES, tm), 0)

    def heads(it, carry):
        hs = [it * ROUTE_INTERLEAVE + u for u in range(ROUTE_INTERLEAVE)]
        problems, experts = [], []
        for h in hs:
            cs, ce, cid = _pair_candidates(sv_ref[2 * h], sv_ref[2 * h + 1], si_ref[2 * h], si_ref[2 * h + 1], sub)
            problems.append((cs, cid))
            experts.append(ce)

        def emit(i, r, m, hit, idx):
            ts_ref[i, r:r + 1, :] = m
            te_ref[i, r:r + 1, :] = jnp.max(jnp.where(hit, experts[i], -1), axis=0, keepdims=True)

        _top16_rows(problems, emit)
        for i, h in enumerate(hs):
            row0 = pl.multiple_of(h * PEER_TOPK, PEER_TOPK)
            e_ref[pl.ds(row0, PEER_TOPK), :] = te_ref[i]
            ts = ts_ref[i]
            ex = jnp.exp(ts - jnp.max(ts, axis=0, keepdims=True))
            g_ref[pl.ds(row0, PEER_TOPK), :] = ex / jnp.sum(ex, axis=0, keepdims=True)
        return carry

    lax.fori_loop(0, PEER_HEADS // ROUTE_INTERLEAVE, heads, 0)


def _peer_route_kernel(q_ref, sk_ref, e_ref, g_ref, sv_ref, si_ref, ts_ref, te_ref, *, tm):
    _route_sub_keys(q_ref, sk_ref, sv_ref, si_ref, tm)
    _route_pairs(sv_ref, si_ref, e_ref, g_ref, ts_ref, te_ref, tm)


def _peer_pairs_kernel(sv_ref, si_ref, e_ref, g_ref, ts_ref, te_ref, *, tm):
    _route_pairs(sv_ref, si_ref, e_ref, g_ref, ts_ref, te_ref, tm)


def _route_out(nblk, tm):
    shape = (jax.ShapeDtypeStruct((nblk, PEER_SLOTS, tm), jnp.int32), jax.ShapeDtypeStruct((nblk, PEER_SLOTS, tm), F32))
    specs = (pl.BlockSpec((None, PEER_SLOTS, tm), lambda i: (i, 0, 0)),
             pl.BlockSpec((None, PEER_SLOTS, tm), lambda i: (i, 0, 0)))
    scratch = [pltpu.VMEM((ROUTE_INTERLEAVE, PEER_TOPK, tm), F32), pltpu.VMEM((ROUTE_INTERLEAVE, PEER_TOPK, tm), jnp.int32)]
    return shape, specs, scratch


def _peer_route(q3, subkeys_bf16, n_tokens):
    tm = ROUTE_TILE
    nblk = n_tokens // tm
    out_shape, out_specs, out_scratch = _route_out(nblk, tm)
    return pl.pallas_call(
        functools.partial(_peer_route_kernel, tm=tm),
        out_shape=out_shape,
        grid=(nblk,),
        in_specs=[
            pl.BlockSpec((2 * PEER_HEADS, tm, PEER_HALF), lambda i: (0, i, 0)),
            pl.BlockSpec((2 * PEER_HEADS, PEER_KEYS, PEER_HALF), lambda i: (0, 0, 0)),
        ],
        out_specs=out_specs,
        scratch_shapes=[
            pltpu.VMEM((2 * PEER_HEADS, PEER_TOPK, tm), F32),
            pltpu.VMEM((2 * PEER_HEADS, PEER_TOPK, tm), jnp.int32),
        ] + out_scratch,
        compiler_params=_cparams(("parallel",)),
        name="peer_route",
    )(q3, subkeys_bf16)


def _peer_pairs(sv, si):
    nblk, _, _, tm = sv.shape
    out_shape, out_specs, out_scratch = _route_out(nblk, tm)
    list_spec = pl.BlockSpec((None, 2 * PEER_HEADS, PEER_TOPK, tm), lambda i: (i, 0, 0, 0))
    return pl.pallas_call(
        functools.partial(_peer_pairs_kernel, tm=tm),
        out_shape=out_shape,
        grid=(nblk,),
        in_specs=[list_spec, list_spec],
        out_specs=out_specs,
        scratch_shapes=out_scratch,
        compiler_params=_cparams(("parallel",)),
        name="peer_pairs",
    )(sv, si)


def _peer_apply_kernel(*refs, tb, side):
    if side:
        (idx_ref, gate_ref, h_ref, x_ref, gt_ref, uv_hbm, q_ref, sk_ref, o_ref, sv_ref, si_ref,
         buf_ref, sem_ref, part_ref, acc_ref, s_ref) = refs
    else:
        idx_ref, gate_ref, h_ref, x_ref, gt_ref, uv_hbm, o_ref, buf_ref, sem_ref, part_ref, acc_ref = refs
    lookahead = GATHER_SLOTS - 1
    n_groups = tb // TOKEN_GROUP
    n_tiles = D_MODEL // LANES
    n_slot_groups = PEER_SLOTS // SUBLANES

    def start_rows(tok, slot, k0, n):
        for k in range(k0, k0 + n):
            pltpu.make_async_copy(uv_hbm.at[idx_ref[tok, k]], buf_ref.at[slot, k],
                                  sem_ref.at[slot]).start(priority=k % 2)

    def wait_gather(slot):
        pltpu.make_async_copy(uv_hbm.at[pl.ds(0, PEER_SLOTS)], buf_ref.at[slot], sem_ref.at[slot]).wait()

    for tok in range(lookahead):
        start_rows(tok, tok % GATHER_SLOTS, 0, PEER_SLOTS)

    tok_lane = lax.broadcasted_iota(jnp.int32, (PEER_SLOTS, ROUTE_TILE), 1)
    sub = lax.broadcasted_iota(jnp.int32, (SUBLANES, LANES), 0)

    def fold(x0, x1, d):
        t0 = x0 + pltpu.roll(x0, SUBLANES - d, 0)
        t1 = x1 + pltpu.roll(x1, d, 0)
        return jnp.where((sub & d) == 0, t0, t1)

    key_id = lax.broadcasted_iota(jnp.int32, (PEER_KEYS, ROUTE_TILE), 0).astype(F32)
    groups_per_list_pair = PEER_TOPK // TOKEN_GROUP
    groups_per_tile = groups_per_list_pair * PEER_HEADS
    if side:
        assert n_groups * TOKEN_GROUP == tb and tb % ROUTE_TILE == 0 and groups_per_tile * TOKEN_GROUP == ROUTE_TILE

    def group(gi, is_last):
        base = gi * TOKEN_GROUP if is_last else pl.multiple_of(gi * TOKEN_GROUP, TOKEN_GROUP)
        h8 = h_ref[pl.ds(base, TOKEN_GROUP), :]
        rows = []
        if side:
            tile = gi // groups_per_tile
            hp0 = 2 * ((gi // groups_per_list_pair) % PEER_HEADS)
            r0 = (gi % groups_per_list_pair) * TOKEN_GROUP

            def load_scores():
                rows0 = tile * ROUTE_TILE if is_last else pl.multiple_of(tile * ROUTE_TILE, ROUTE_TILE)
                for pi in range(2):
                    s_ref[pi] = _dot_nt(sk_ref[hp0 + pi], q_ref[hp0 + pi, pl.ds(rows0, ROUTE_TILE), :].astype(BF16))

            if is_last:
                if r0 == 0:
                    load_scores()
            else:
                pl.when(r0 == 0)(load_scores)
        for j in range(TOKEN_GROUP):
            tok = base + j
            slot = j % GATHER_SLOTS
            prefetch = not is_last or j + lookahead < TOKEN_GROUP

            def start_ahead(k0, n, tok=tok, j=j, prefetch=prefetch):
                if prefetch:
                    start_rows(tok + lookahead, (j + lookahead) % GATHER_SLOTS, k0, n)

            wait_gather(slot)
            h_tile = jnp.concatenate([h8[j:j + 1, s * LANES:(s + 1) * LANES] for s in range(n_tiles)], axis=0)
            for g in range(n_slot_groups):
                p = [lax.bitcast_convert_type(buf_ref[slot, g * SUBLANES + c] & jnp.uint32(0xFFFF0000), F32) * h_tile
                     for c in range(SUBLANES)]
                y = [fold(p[c], p[c + 4], 4) for c in range(4)]
                z = [fold(y[c], y[c + 2], 2) for c in range(2)]
                part_ref[g * SUBLANES:(g + 1) * SUBLANES, :] = fold(z[0], z[1], 1)
                start_ahead(g * ISSUE_PER_GROUP, ISSUE_PER_GROUP)
            acc_ref[...] = jnp.zeros(acc_ref.shape, F32)
            start_ahead(n_slot_groups * ISSUE_PER_GROUP, PEER_SLOTS - 2 * n_slot_groups * ISSUE_PER_GROUP)
            if side:
                for pi in range(2):
                    sc = s_ref[pi]
                    m = jnp.max(sc, axis=0, keepdims=True)
                    idx = jnp.min(jnp.where(sc == m, key_id, _NO_ROW), axis=0, keepdims=True)
                    sv_ref[tile, hp0 + pi, r0 + j] = m
                    si_ref[tile, hp0 + pi, r0 + j] = idx.astype(jnp.int32)
                    s_ref[pi] = jnp.where(key_id == idx, -jnp.inf, sc)
            act = jnp.sum(part_ref[...], axis=1, keepdims=True)
            gates = gate_ref[tok // ROUTE_TILE]
            gate = jnp.sum(jnp.where(tok_lane == tok % ROUTE_TILE, gates, 0.0), axis=1, keepdims=True)
            w = gate * _gelu(act)
            for g in range(n_slot_groups):
                terms = [jnp.broadcast_to(w[k:k + 1, :], (SUBLANES, LANES))
                         * lax.bitcast_convert_type(buf_ref[slot, k] << 16, F32)
                         for k in range(g * SUBLANES, (g + 1) * SUBLANES)]
                s4 = [terms[i] + terms[i + 4] for i in range(4)]
                acc_ref[g % 2] = acc_ref[g % 2] + ((s4[0] + s4[1]) + (s4[2] + s4[3]))
                start_ahead(PEER_SLOTS - (n_slot_groups - g) * ISSUE_PER_GROUP, ISSUE_PER_GROUP)
            acc = acc_ref[0] + acc_ref[1]
            rows.append(jnp.concatenate([acc[s:s + 1, :] for s in range(n_tiles)], axis=1))
        out8 = jnp.concatenate(rows, axis=0)
        o_ref[pl.ds(base, TOKEN_GROUP), :] = x_ref[pl.ds(base, TOKEN_GROUP), :] + gt_ref[...] * out8

    def body(gi, carry):
        group(gi, False)
        return carry

    lax.fori_loop(0, n_groups - 1, body, 0)
    group(n_groups - 1, True)


def _peer_apply(experts_tok, gates, h2d, x2d, gt, uv_table, n_tokens, row0, rows_per_mod, side=None):
    tb = min(APPLY_TILE, n_tokens)
    tiles_per_mod = rows_per_mod // tb
    tile0 = row0 // tb
    in_specs = [
        pl.BlockSpec((tb, PEER_SLOTS), lambda i: (i, 0), memory_space=pltpu.SMEM),
        pl.BlockSpec((tb // ROUTE_TILE, PEER_SLOTS, ROUTE_TILE), lambda i: (i, 0, 0)),
        pl.BlockSpec((tb, D_MODEL), lambda i: (tile0 + i, 0)),
        pl.BlockSpec((tb, D_MODEL), lambda i: (tile0 + i, 0)),
        pl.BlockSpec((None, 1, D_MODEL), lambda i: ((tile0 + i) // tiles_per_mod, 0, 0)),
        pl.BlockSpec(memory_space=pl.ANY),
    ]
    args = [experts_tok, gates, h2d, x2d, gt, uv_table]
    out_shape = [jax.ShapeDtypeStruct((n_tokens, D_MODEL), F32)]
    out_specs = [pl.BlockSpec((tb, D_MODEL), lambda i: (i, 0))]
    scratch = [
        pltpu.VMEM((GATHER_SLOTS, PEER_SLOTS, SUBLANES, LANES), jnp.uint32),
        pltpu.SemaphoreType.DMA((GATHER_SLOTS,)),
        pltpu.VMEM((PEER_SLOTS, LANES), F32),
        pltpu.VMEM((2, SUBLANES, LANES), F32),
    ]
    if side is not None:
        q3, subkeys, side_row0 = side
        side_tile0 = side_row0 // tb
        in_specs += [
            pl.BlockSpec((2 * PEER_HEADS, tb, PEER_HALF), lambda i: (0, side_tile0 + i, 0)),
            pl.BlockSpec((2 * PEER_HEADS, PEER_KEYS, PEER_HALF), lambda i: (0, 0, 0)),
        ]
        args += [q3, subkeys]
        list_shape = (n_tokens // ROUTE_TILE, 2 * PEER_HEADS, PEER_TOPK, 1, ROUTE_TILE)
        list_block = (tb // ROUTE_TILE, 2 * PEER_HEADS, PEER_TOPK, 1, ROUTE_TILE)
        out_shape += [jax.ShapeDtypeStruct(list_shape, F32), jax.ShapeDtypeStruct(list_shape, jnp.int32)]
        out_specs += [pl.BlockSpec(list_block, lambda i: (i, 0, 0, 0, 0))] * 2
        scratch.append(pltpu.VMEM((2, PEER_KEYS, ROUTE_TILE), F32))
    outs = pl.pallas_call(
        functools.partial(_peer_apply_kernel, tb=tb, side=side is not None),
        out_shape=tuple(out_shape),
        grid=(n_tokens // tb,),
        in_specs=in_specs,
        out_specs=tuple(out_specs),
        scratch_shapes=scratch,
        compiler_params=_cparams(("arbitrary",), disable_bounds_checks=True),
        name="peer_apply_side" if side is not None else "peer_apply",
    )(*args)
    if side is None:
        return outs[0]
    sv, si = (a.reshape(a.shape[:3] + a.shape[4:]) for a in outs[1:])
    return outs[0], sv, si


def _final_norm_kernel(x_ref, g_ref, o_ref):
    x = x_ref[...]
    o_ref[...] = (x * lax.rsqrt(jnp.mean(x * x, axis=-1, keepdims=True) + EPS)) * g_ref[...]


def _final_norm(x2d, g):
    t = x2d.shape[0]
    tm = min(2 * TOKEN_TILE, t)
    return pl.pallas_call(
        _final_norm_kernel,
        out_shape=jax.ShapeDtypeStruct((t, D_MODEL), F32),
        grid=(t // tm,),
        in_specs=[pl.BlockSpec((tm, D_MODEL), lambda i: (i, 0)), pl.BlockSpec((1, D_MODEL), lambda i: (0, 0))],
        out_specs=pl.BlockSpec((tm, D_MODEL), lambda i: (i, 0)),
        compiler_params=_cparams(("parallel",)),
        name="final_norm",
    )(x2d, g)


def _rope_tables(length):
    rows = length // GRID_W
    row = jnp.repeat(jnp.arange(rows, dtype=F32), GRID_W)
    col = jnp.tile(jnp.arange(GRID_W, dtype=F32), rows)
    inv = ROPE_BASE ** (-jnp.arange(ROPE_FREQS, dtype=F32) / ROPE_FREQS)
    ang = jnp.stack([row[:, None] * inv, col[:, None] * inv], axis=1)
    cos, sin = jnp.cos(ang), jnp.sin(ang)
    cos_h = jnp.broadcast_to(cos[:, :, None, :], (length, 2, 2, ROPE_FREQS)).reshape(length, HEAD_DIM)
    sin_h = jnp.stack([-sin, sin], axis=2).reshape(length, HEAD_DIM)
    reps = LANES // HEAD_DIM
    return jnp.tile(cos_h, (1, reps)), jnp.tile(sin_h, (1, reps))


def _pack_expert_tables(u_tab, v_tab):
    def bits(a):
        return lax.bitcast_convert_type(a.astype(BF16), jnp.uint16).astype(jnp.uint32)

    return ((bits(u_tab) << 16) | bits(v_tab)).reshape(-1, D_MODEL // LANES, LANES)


def _experts_by_token(experts):
    return jnp.transpose(experts, (0, 2, 1)).reshape(-1, PEER_SLOTS)


def _peer_layer(x2d, g, sc, sh, gt, rows_per_mod, wq, subkeys, uv_table, n_chunks):
    t = x2d.shape[0]
    q3, h2d = _peer_query(x2d, g, sc, sh, wq, rows_per_mod)
    n = t // n_chunks
    experts, gates = _peer_route(q3, subkeys, n)
    outs = []
    for c in range(n_chunks):
        side = (q3, subkeys, (c + 1) * n) if c + 1 < n_chunks else None
        res = _peer_apply(_experts_by_token(experts), gates, h2d, x2d, gt, uv_table, n, c * n, rows_per_mod, side)
        if side is None:
            outs.append(res)
        else:
            outs.append(res[0])
            experts, gates = _peer_pairs(res[1], res[2])
    return outs[0] if n_chunks == 1 else jnp.concatenate(outs, axis=0)


def kernel(x, c, ctx, c_ctx, w_mod, b_mod, g_mix, g_ffn, w_in, attn_sink, pool_w, pool_scale, sgu_w, sgu_b,
           w_br_attn, w_br_pool, w_br_sgu, w_out, peer_wq, peer_subkeys, peer_u, peer_v, g_final):
    b, l, d = x.shape
    n_ctx = ctx.shape[1]
    depth = w_mod.shape[0]
    assert d == D_MODEL and l % TOKEN_TILE == 0 and n_ctx % SGU_CHUNK == 0

    n_rows = -(-(b + 1) // SUBLANES) * SUBLANES
    cond = jnp.zeros((n_rows, d), F32).at[:b].set(c).at[b].set(c_ctx)
    mod = _adaln(cond, w_mod, b_mod).reshape(depth, n_rows, N_MOD, 1, d)

    rope = _rope_tables(l)
    x2d = x.reshape(b * l, d)
    xc2d = ctx.reshape(b * n_ctx, d)

    for layer in range(depth):
        last = layer == depth - 1
        m_lat = [mod[layer, :b, i] for i in range(N_MOD)]
        m_ctx = [mod[layer, b:b + 1, i] for i in range(N_MOD)]
        g1 = g_mix[layer].reshape(1, d)
        g2 = g_ffn[layer].reshape(1, d)

        w_in_b = w_in[layer].astype(BF16)
        qkv, rest = _in_proj(x2d, g1, m_lat[1], m_lat[0], w_in_b, l, rope, l)
        qkv_c, rest_c = _in_proj(xc2d, g1, m_ctx[1], m_ctx[0], w_in_b, b * n_ctx, None, n_ctx)

        y_attn = _attention(qkv.reshape(b, l, QKV_W), qkv_c.reshape(b, n_ctx, QKV_W), attn_sink[layer], True)

        poolw_bd = jnp.zeros((POOL_W, POOL_W), F32)
        for gi in range(len(POOL_SIZES)):
            sl = slice(gi * POOL_GROUP_W, (gi + 1) * POOL_GROUP_W)
            poolw_bd = poolw_bd.at[sl, sl].set(pool_w[layer, gi])
        poolw_bd = poolw_bd.astype(BF16)
        pscale = pool_scale[layer].reshape(1, POOL_W)
        sguw = sgu_w[layer].astype(BF16)
        sgub_full = jnp.repeat(sgu_b[layer].T, SGU_GROUP_W, axis=1)
        wbr = jnp.concatenate([w_br_attn[layer], w_br_pool[layer], w_br_sgu[layer]], axis=0).astype(BF16)
        wout = w_out[layer].astype(BF16)
        mix_w = (poolw_bd, pscale, sguw, sgub_full, wbr, wout)

        x2d = _mixer(rest, y_attn.reshape(b * l, ATTN_W), x2d, m_lat[2], *mix_w, l, l)
        if not last:
            y_attn_c = _attention(qkv_c.reshape(b, n_ctx, QKV_W), qkv_c.reshape(b, n_ctx, QKV_W),
                                  attn_sink[layer], False)
            xc2d = _mixer(rest_c, y_attn_c.reshape(b * n_ctx, ATTN_W), xc2d, m_ctx[2], *mix_w,
                          b * n_ctx, n_ctx)

        wq = peer_wq[layer].astype(BF16)
        subkeys = peer_subkeys[layer].reshape(2 * PEER_HEADS, PEER_KEYS, PEER_HALF).astype(BF16)
        uv_table = _pack_expert_tables(peer_u[layer], peer_v[layer])
        x2d = _peer_layer(x2d, g2, m_lat[4], m_lat[3], m_lat[5], l, wq, subkeys, uv_table, b)
        if not last:
            xc2d = _peer_layer(xc2d, g2, m_ctx[4], m_ctx[3], m_ctx[5], b * n_ctx, wq, subkeys, uv_table, 1)

    return _final_norm(x2d, g_final.reshape(1, d)).reshape(b, l, d)
```

```python
import functools

import jax
import jax.numpy as jnp
from jax import lax
from jax.experimental import pallas as pl
from jax.experimental.pallas import tpu as pltpu

F32 = jnp.float32
BF16 = jnp.bfloat16

D_MODEL = 1024
EPS = 1e-6
N_MOD = 6
GRID_W = 64

N_HEADS = 8
N_KV_HEADS = 2
GQA_GROUP = N_HEADS // N_KV_HEADS
HEAD_DIM = 64
WINDOW = 128
ATTN_BLOCK = 128
ATTN_W = N_HEADS * HEAD_DIM
KV_W = N_KV_HEADS * HEAD_DIM
QKV_W = ATTN_W + 2 * KV_W
ROPE_BASE = 10000.0
ROPE_FREQS = HEAD_DIM // 4

POOL_SIZES = (2, 4, 8, 16)
POOL_GROUP_W = 64
POOL_W = len(POOL_SIZES) * POOL_GROUP_W
POOL_HALO = 8

SGU_CHUNK = 128
SGU_GROUPS = 4
SGU_W = 256
SGU_GROUP_W = SGU_W // SGU_GROUPS

N_BRANCH = 3
REST_W = POOL_W + 2 * SGU_W + N_BRANCH * D_MODEL
IN_W = QKV_W + REST_W
BR_W = ATTN_W + POOL_W + SGU_W

PEER_HEADS = 8
PEER_KEYS = 128
PEER_HALF = 128
PEER_TOPK = 16
PEER_SLOTS = PEER_HEADS * PEER_TOPK
PEER_QW = PEER_HEADS * 2 * PEER_HALF

LANES = 128
SUBLANES = 8
VMEM_LIMIT = 48 * 1024 * 1024

TOKEN_TILE = 256
ROUTE_TILE = 128
APPLY_TILE = 256
PEER_CHUNK = 2048
GATHER_SLOTS = 8
TOKEN_GROUP = 8
ISSUE_PER_GROUP = 3
ROUTE_INTERLEAVE = 2
_NO_ROW = 1e9


def _cparams(sem, **kw):
    return pltpu.CompilerParams(dimension_semantics=sem, vmem_limit_bytes=VMEM_LIMIT, **kw)


def _dot(a, b):
    return jnp.dot(a, b, preferred_element_type=F32)


def _dot_nt(a, b):
    return lax.dot_general(a, b, (((1,), (1,)), ((), ())), preferred_element_type=F32)


def _split_bf16(a):
    hi = a.astype(BF16)
    lo = (a - hi.astype(F32)).astype(BF16)
    return hi, lo


def _adaln_kernel(c_ref, w_ref, b_ref, o_ref):
    c = c_ref[...]
    s = c * (1.0 / (1.0 + jnp.exp(-c)))
    s_hi, s_lo = _split_bf16(s)
    w_hi, w_lo = _split_bf16(w_ref[...])
    acc = _dot(s_hi, w_hi) + (_dot(s_hi, w_lo) + _dot(s_lo, w_hi))
    o_ref[...] = acc + b_ref[...]


def _adaln(cond, w_mod, b_mod):
    depth = w_mod.shape[0]
    r = cond.shape[0]
    n = w_mod.shape[2]
    tn = D_MODEL
    return pl.pallas_call(
        _adaln_kernel,
        out_shape=jax.ShapeDtypeStruct((depth, r, n), F32),
        grid=(depth, n // tn),
        in_specs=[
            pl.BlockSpec((r, D_MODEL), lambda l, j: (0, 0)),
            pl.BlockSpec((None, D_MODEL, tn), lambda l, j: (l, 0, j)),
            pl.BlockSpec((None, 1, tn), lambda l, j: (l, 0, j)),
        ],
        out_specs=pl.BlockSpec((None, r, tn), lambda l, j: (l, 0, j)),
        compiler_params=_cparams(("arbitrary", "arbitrary")),
        name="adaln",
    )(cond, w_mod, b_mod.reshape(depth, 1, n))


def _modulated_norm(x, g, sc, sh):
    y = x * lax.rsqrt(jnp.mean(x * x, axis=-1, keepdims=True) + EPS)
    return (y * g) * (1.0 + sc) + sh


def _rope_tile(x, cos, sin_signed):
    lane = lax.broadcasted_iota(jnp.int32, x.shape, 1)
    first_half = (lane % (2 * ROPE_FREQS)) < ROPE_FREQS
    partner = jnp.where(first_half, pltpu.roll(x, LANES - ROPE_FREQS, 1), pltpu.roll(x, ROPE_FREQS, 1))
    return x * cos + partner * sin_signed


def _in_proj_kernel(x_ref, g_ref, sc_ref, sh_ref, w_ref, *rest, rope):
    if rope:
        cos_ref, sin_ref, qkv_ref, rest_ref = rest
    else:
        qkv_ref, rest_ref = rest
    hb = _modulated_norm(x_ref[...], g_ref[...], sc_ref[...], sh_ref[...]).astype(BF16)
    for c2 in range(QKV_W // (2 * LANES)):
        acc2 = _dot(hb, w_ref[:, c2 * 2 * LANES:(c2 + 1) * 2 * LANES])
        for half in range(2):
            c = 2 * c2 + half
            acc = acc2[:, half * LANES:(half + 1) * LANES]
            if rope and c * LANES < ATTN_W + KV_W:
                acc = _rope_tile(acc, cos_ref[...], sin_ref[...])
            qkv_ref[:, c * LANES:(c + 1) * LANES] = acc
    chunk = 768
    for c in range(REST_W // chunk):
        rest_ref[:, c * chunk:(c + 1) * chunk] = _dot(hb, w_ref[:, QKV_W + c * chunk:QKV_W + (c + 1) * chunk])


def _in_proj(x2d, g, sc, sh, w_bf16, rows_per_mod, rope_tables, seq_len):
    t = x2d.shape[0]
    tm = min(TOKEN_TILE, t)
    tiles_per_mod = rows_per_mod // tm
    rope = rope_tables is not None
    in_specs = [
        pl.BlockSpec((tm, D_MODEL), lambda i: (i, 0)),
        pl.BlockSpec((1, D_MODEL), lambda i: (0, 0)),
        pl.BlockSpec((None, 1, D_MODEL), lambda i: (i // tiles_per_mod, 0, 0)),
        pl.BlockSpec((None, 1, D_MODEL), lambda i: (i // tiles_per_mod, 0, 0)),
        pl.BlockSpec((D_MODEL, IN_W), lambda i: (0, 0)),
    ]
    args = [x2d, g, sc, sh, w_bf16]
    if rope:
        tiles_per_seq = seq_len // tm
        in_specs += [pl.BlockSpec((tm, LANES), lambda i: (i % tiles_per_seq, 0))] * 2
        args += list(rope_tables)
    return pl.pallas_call(
        functools.partial(_in_proj_kernel, rope=rope),
        out_shape=(jax.ShapeDtypeStruct((t, QKV_W), F32), jax.ShapeDtypeStruct((t, REST_W), F32)),
        grid=(t // tm,),
        in_specs=in_specs,
        out_specs=(pl.BlockSpec((tm, QKV_W), lambda i: (i, 0)), pl.BlockSpec((tm, REST_W), lambda i: (i, 0))),
        compiler_params=_cparams(("parallel",)),
        name="in_proj",
    )(*args)


def _peer_query_kernel(x_ref, g_ref, sc_ref, sh_ref, w_ref, q_ref, h_ref):
    h = _modulated_norm(x_ref[...], g_ref[...], sc_ref[...], sh_ref[...])
    h_ref[...] = h
    hb = h.astype(BF16)
    for head in range(PEER_HEADS):
        acc = _dot(hb, w_ref[:, head * 2 * PEER_HALF:(head + 1) * 2 * PEER_HALF])
        q_ref[2 * head] = acc[:, :PEER_HALF]
        q_ref[2 * head + 1] = acc[:, PEER_HALF:]


def _peer_query(x2d, g, sc, sh, wq_bf16, rows_per_mod):
    t = x2d.shape[0]
    tm = min(TOKEN_TILE, t)
    tiles_per_mod = rows_per_mod // tm
    return pl.pallas_call(
        _peer_query_kernel,
        out_shape=(jax.ShapeDtypeStruct((2 * PEER_HEADS, t, PEER_HALF), F32),
                   jax.ShapeDtypeStruct((t, D_MODEL), F32)),
        grid=(t // tm,),
        in_specs=[
            pl.BlockSpec((tm, D_MODEL), lambda i: (i, 0)),
            pl.BlockSpec((1, D_MODEL), lambda i: (0, 0)),
            pl.BlockSpec((None, 1, D_MODEL), lambda i: (i // tiles_per_mod, 0, 0)),
            pl.BlockSpec((None, 1, D_MODEL), lambda i: (i // tiles_per_mod, 0, 0)),
            pl.BlockSpec((D_MODEL, PEER_QW), lambda i: (0, 0)),
        ],
        out_specs=(pl.BlockSpec((2 * PEER_HEADS, tm, PEER_HALF), lambda i: (0, i, 0)),
                   pl.BlockSpec((tm, D_MODEL), lambda i: (i, 0))),
        compiler_params=_cparams(("parallel",)),
        name="peer_query",
    )(x2d, g, sc, sh, wq_bf16)


def _attn_kernel(sink_ref, q_ref, *rest, window, n_blocks):
    if window:
        kvp_ref, kvc_ref, kvn_ref, ctx_ref, o_ref = rest
    else:
        ctx_ref, o_ref = rest
    n = pl.program_id(1)
    q = q_ref[...] * (HEAD_DIM ** -0.5)
    sources = []
    if window:
        qi = lax.broadcasted_iota(jnp.int32, (GQA_GROUP * ATTN_BLOCK, ATTN_BLOCK), 0) % ATTN_BLOCK
        kj = lax.broadcasted_iota(jnp.int32, (GQA_GROUP * ATTN_BLOCK, ATTN_BLOCK), 1)
        sources.append((kvp_ref[...], (kj >= qi) & (n > 0)))
        sources.append((kvc_ref[...], None))
        sources.append((kvn_ref[...], (kj <= qi) & (n < n_blocks - 1)))
    for c in range(ctx_ref.shape[0] // ATTN_BLOCK):
        sources.append((ctx_ref[c * ATTN_BLOCK:(c + 1) * ATTN_BLOCK, :], None))
    kb = [[kv[:, g * HEAD_DIM:(g + 1) * HEAD_DIM].astype(BF16) for g in range(N_KV_HEADS)] for kv, _ in sources]
    ones = jnp.ones((ATTN_BLOCK, HEAD_DIM), BF16)
    vb = [[jnp.concatenate([kv[:, KV_W + g * HEAD_DIM:KV_W + (g + 1) * HEAD_DIM].astype(BF16), ones], axis=1)
           for g in range(N_KV_HEADS)] for kv, _ in sources]
    groups = [range(g * GQA_GROUP, (g + 1) * GQA_GROUP) for g in range(N_KV_HEADS)]
    sinks = [jnp.concatenate([jnp.full((ATTN_BLOCK, 1), sink_ref[h], F32) for h in heads], axis=0)
             for heads in groups]
    scores = []
    for g, heads in enumerate(groups):
        qg = jnp.concatenate([q[:, h * HEAD_DIM:(h + 1) * HEAD_DIM] for h in heads], axis=0).astype(BF16)
        sg = []
        for si, (_, valid) in enumerate(sources):
            s = _dot_nt(qg, kb[si][g])
            sg.append(s if valid is None else jnp.where(valid, s, -jnp.inf))
        scores.append(sg)
    maxes = [jnp.maximum(jnp.max(functools.reduce(jnp.maximum, scores[g]), axis=-1, keepdims=True), sinks[g])
             for g in range(N_KV_HEADS)]
    outs = []
    for g in range(N_KV_HEADS):
        m = maxes[g]
        acc = jnp.zeros((GQA_GROUP * ATTN_BLOCK, 2 * HEAD_DIM), F32)
        for si, s in enumerate(scores[g]):
            acc = acc + _dot(jnp.exp(s - m).astype(BF16), vb[si][g])
        og = acc[:, :HEAD_DIM] / (acc[:, HEAD_DIM:] + jnp.exp(sinks[g] - m))
        outs += [og[i * ATTN_BLOCK:(i + 1) * ATTN_BLOCK, :] for i in range(GQA_GROUP)]
    o_ref[...] = jnp.concatenate(outs, axis=-1)


def _attention(qkv, kv_ctx, sink, window):
    b, l, _ = qkv.shape
    n_ctx = kv_ctx.shape[1]
    nb = l // ATTN_BLOCK
    kv_col = ATTN_W // (2 * KV_W)
    in_specs = [
        pl.BlockSpec(memory_space=pltpu.SMEM),
        pl.BlockSpec((None, ATTN_BLOCK, ATTN_W), lambda bi, n: (bi, n, 0)),
    ]
    args = [sink, qkv]
    if window:
        in_specs += [
            pl.BlockSpec((None, ATTN_BLOCK, 2 * KV_W), lambda bi, n: (bi, jnp.maximum(n - 1, 0), kv_col)),
            pl.BlockSpec((None, ATTN_BLOCK, 2 * KV_W), lambda bi, n: (bi, n, kv_col)),
            pl.BlockSpec((None, ATTN_BLOCK, 2 * KV_W), lambda bi, n: (bi, jnp.minimum(n + 1, nb - 1), kv_col)),
        ]
        args += [qkv, qkv, qkv]
    in_specs.append(pl.BlockSpec((None, n_ctx, 2 * KV_W), lambda bi, n: (bi, 0, kv_col)))
    args.append(kv_ctx)
    return pl.pallas_call(
        functools.partial(_attn_kernel, window=window, n_blocks=nb),
        out_shape=jax.ShapeDtypeStruct((b, l, ATTN_W), F32),
        grid=(b, nb),
        in_specs=in_specs,
        out_specs=pl.BlockSpec((None, ATTN_BLOCK, ATTN_W), lambda bi, n: (bi, n, 0)),
        compiler_params=_cparams(("parallel", "parallel")),
        name="attention_window" if window else "attention_context",
    )(*args)


def _gelu(x):
    return jax.nn.gelu(x, approximate=True)


def _sigmoid(x):
    return 1.0 / (1.0 + jnp.exp(-x))


def _mixer_kernel(rest_ref, hprev_ref, hnext_ref, ya_ref, x_ref, gt_ref, poolw_ref, pscale_ref, sguw_ref,
                  sgub_ref, wbr_ref, wout_ref, o_ref, ext_ref, *, tm, tiles_per_seq, seq_len):
    ti = pl.program_id(0) % tiles_per_seq
    z = rest_ref[:, 0:POOL_W]
    ext_ref[0:POOL_HALO, :] = jnp.where(ti > 0, hprev_ref[...], 0.0)
    ext_ref[POOL_HALO:POOL_HALO + tm, :] = z
    ext_ref[POOL_HALO + tm:2 * POOL_HALO + tm, :] = jnp.where(ti < tiles_per_seq - 1, hnext_ref[...], 0.0)
    pos = ti * tm + lax.broadcasted_iota(jnp.int32, (tm, LANES), 0)
    lane = lax.broadcasted_iota(jnp.int32, (tm, LANES), 1)

    def count(size):
        hi = jnp.minimum(pos + size // 2, seq_len)
        lo = jnp.maximum(pos - size // 2, 0)
        return (hi - lo).astype(F32)

    diffs = []
    for lt in range(POOL_W // LANES):
        cols = slice(lt * LANES, (lt + 1) * LANES)

        def shifted(d, cols=cols):
            return ext_ref[POOL_HALO + d:POOL_HALO + d + tm, cols]

        small, large = POOL_SIZES[2 * lt], POOL_SIZES[2 * lt + 1]
        s = shifted(-1) + shifted(0)
        width = 2
        sums = {}
        while width <= large:
            sums[width] = s
            half = width // 2
            if 2 * width <= large:
                for d in range(half, width):
                    s = s + shifted(-d - 1) + shifted(d)
            width *= 2
        mean = jnp.where(lane < POOL_GROUP_W, sums[small] / count(small), sums[large] / count(large))
        diffs.append(mean - z[:, cols])
    d = jnp.concatenate(diffs, axis=-1).astype(BF16)
    y_pool = _dot(d, poolw_ref[...]) * pscale_ref[...]

    u = _gelu(rest_ref[:, POOL_W:POOL_W + SGU_W])
    v = _gelu(rest_ref[:, POOL_W + SGU_W:POOL_W + 2 * SGU_W])
    vn = (v * lax.rsqrt(jnp.mean(v * v, axis=-1, keepdims=True) + EPS)).astype(BF16)
    group = lax.broadcasted_iota(jnp.int32, (SGU_CHUNK, SGU_W), 1) // SGU_GROUP_W
    mixed = []
    for c in range(tm // SGU_CHUNK):
        vc = vn[c * SGU_CHUNK:(c + 1) * SGU_CHUNK, :]
        mc = sgub_ref[...]
        for hg in range(SGU_GROUPS):
            mc = mc + jnp.where(group == hg, _dot(sguw_ref[hg], vc), 0.0)
        mixed.append(mc)
    y_sgu = u * jnp.concatenate(mixed, axis=0)

    g_off = POOL_W + 2 * SGU_W
    merged = _sigmoid(rest_ref[:, g_off:g_off + D_MODEL]) * _dot(ya_ref[...].astype(BF16), wbr_ref[0:ATTN_W, :])
    merged = merged + _sigmoid(rest_ref[:, g_off + D_MODEL:g_off + 2 * D_MODEL]) * _dot(
        y_pool.astype(BF16), wbr_ref[ATTN_W:ATTN_W + POOL_W, :])
    merged = merged + _sigmoid(rest_ref[:, g_off + 2 * D_MODEL:g_off + 3 * D_MODEL]) * _dot(
        y_sgu.astype(BF16), wbr_ref[ATTN_W + POOL_W:BR_W, :])
    o_ref[...] = x_ref[...] + gt_ref[...] * _dot(merged.astype(BF16), wout_ref[...])


def _mixer(rest, y_attn, x2d, gt, poolw_bd, pscale, sguw, sgub_full, wbr, wout, rows_per_mod, seq_len):
    t = x2d.shape[0]
    tm = min(TOKEN_TILE, seq_len)
    tiles_per_seq = seq_len // tm
    tiles_per_mod = rows_per_mod // tm
    halo_blocks = tm // POOL_HALO
    n_halo = t // POOL_HALO
    kern = functools.partial(_mixer_kernel, tm=tm, tiles_per_seq=tiles_per_seq, seq_len=seq_len)
    return pl.pallas_call(
        kern,
        out_shape=jax.ShapeDtypeStruct((t, D_MODEL), F32),
        grid=(t // tm,),
        in_specs=[
            pl.BlockSpec((tm, REST_W), lambda i: (i, 0)),
            pl.BlockSpec((POOL_HALO, POOL_W), lambda i: (jnp.maximum(i * halo_blocks - 1, 0), 0)),
            pl.BlockSpec((POOL_HALO, POOL_W), lambda i: (jnp.minimum((i + 1) * halo_blocks, n_halo - 1), 0)),
            pl.BlockSpec((tm, ATTN_W), lambda i: (i, 0)),
            pl.BlockSpec((tm, D_MODEL), lambda i: (i, 0)),
            pl.BlockSpec((None, 1, D_MODEL), lambda i: (i // tiles_per_mod, 0, 0)),
            pl.BlockSpec((POOL_W, POOL_W), lambda i: (0, 0)),
            pl.BlockSpec((1, POOL_W), lambda i: (0, 0)),
            pl.BlockSpec((SGU_GROUPS, SGU_CHUNK, SGU_CHUNK), lambda i: (0, 0, 0)),
            pl.BlockSpec((SGU_CHUNK, SGU_W), lambda i: (0, 0)),
            pl.BlockSpec((BR_W, D_MODEL), lambda i: (0, 0)),
            pl.BlockSpec((D_MODEL, D_MODEL), lambda i: (0, 0)),
        ],
        out_specs=pl.BlockSpec((tm, D_MODEL), lambda i: (i, 0)),
        scratch_shapes=[pltpu.VMEM((tm + 2 * POOL_HALO, POOL_W), F32)],
        compiler_params=_cparams(("parallel",)),
        name="mixer",
    )(rest, rest, rest, y_attn, x2d, gt, poolw_bd, pscale, sguw, sgub_full, wbr, wout)


def _top16_rows(problems, emit):
    scores = [p[0] for p in problems]
    for r in range(PEER_TOPK):
        for i, (_, ids) in enumerate(problems):
            s = scores[i]
            m = jnp.max(s, axis=0, keepdims=True)
            idx = jnp.min(jnp.where(s == m, ids, _NO_ROW), axis=0, keepdims=True)
            hit = ids == idx
            emit(i, r, m, hit, idx)
            scores[i] = jnp.where(hit, -jnp.inf, s)


def _pair_candidates(sv0, sv1, si0, si1, sub):
    subf = sub.astype(F32)
    sums, experts, flat = [], [], []
    for a in range(SUBLANES):
        n_b = PEER_TOPK // (a + 1)
        for b0 in range(0, n_b, SUBLANES):
            s = sv0[a:a + 1, :] + sv1[b0:b0 + SUBLANES, :]
            if n_b - b0 < SUBLANES:
                s = jnp.where(sub < n_b - b0, s, -jnp.inf)
            sums.append(s)
            experts.append(si0[a:a + 1, :] * PEER_KEYS + si1[b0:b0 + SUBLANES, :])
            flat.append(subf + float(a * PEER_TOPK + b0))
    assert PEER_TOPK // (SUBLANES + 1) == 1
    sums.append(sv0[SUBLANES:, :] + sv1[0:1, :])
    experts.append(si0[SUBLANES:, :] * PEER_KEYS + si1[0:1, :])
    flat.append((subf + float(SUBLANES)) * float(PEER_TOPK))
    return jnp.concatenate(sums, axis=0), jnp.concatenate(experts, axis=0), jnp.concatenate(flat, axis=0)


def _route_sub_keys(q_ref, sk_ref, sv_ref, si_ref, tm):
    key_id = lax.broadcasted_iota(jnp.int32, (PEER_KEYS, tm), 0).astype(F32)

    def sub_keys(it, carry):
        hps = [it * ROUTE_INTERLEAVE + u for u in range(ROUTE_INTERLEAVE)]
        problems = [(_dot_nt(sk_ref[hp], q_ref[hp].astype(BF16)), key_id) for hp in hps]

        def emit(i, r, m, hit, idx):
            sv_ref[hps[i], r:r + 1, :] = m
            si_ref[hps[i], r:r + 1, :] = idx.astype(jnp.int32)

        _top16_rows(problems, emit)
        return carry

    lax.fori_loop(0, 2 * PEER_HEADS // ROUTE_INTERLEAVE, sub_keys, 0)


def _route_pairs(sv_ref, si_ref, e_ref, g_ref, ts_ref, te_ref, tm):
    sub = lax.broadcasted_iota(jnp.int32, (SUBLANES, tm), 0)

    def heads(it, carry):
        hs = [it * ROUTE_INTERLEAVE + u for u in range(ROUTE_INTERLEAVE)]
        problems, experts = [], []
        for h in hs:
            cs, ce, cid = _pair_candidates(sv_ref[2 * h], sv_ref[2 * h + 1], si_ref[2 * h], si_ref[2 * h + 1], sub)
            problems.append((cs, cid))
            experts.append(ce)

        def emit(i, r, m, hit, idx):
            ts_ref[i, r:r + 1, :] = m
            te_ref[i, r:r + 1, :] = jnp.max(jnp.where(hit, experts[i], -1), axis=0, keepdims=True)

        _top16_rows(problems, emit)
        for i, h in enumerate(hs):
            row0 = pl.multiple_of(h * PEER_TOPK, PEER_TOPK)
            e_ref[pl.ds(row0, PEER_TOPK), :] = te_ref[i]
            ts = ts_ref[i]
            ex = jnp.exp(ts - jnp.max(ts, axis=0, keepdims=True))
            g_ref[pl.ds(row0, PEER_TOPK), :] = ex / jnp.sum(ex, axis=0, keepdims=True)
        return carry

    lax.fori_loop(0, PEER_HEADS // ROUTE_INTERLEAVE, heads, 0)


def _peer_route_kernel(q_ref, sk_ref, e_ref, g_ref, sv_ref, si_ref, ts_ref, te_ref, *, tm):
    _route_sub_keys(q_ref, sk_ref, sv_ref, si_ref, tm)
    _route_pairs(sv_ref, si_ref, e_ref, g_ref, ts_ref, te_ref, tm)


def _peer_pairs_kernel(sv_ref, si_ref, e_ref, g_ref, ts_ref, te_ref, *, tm):
    _route_pairs(sv_ref, si_ref, e_ref, g_ref, ts_ref, te_ref, tm)


def _route_out(nblk, tm):
    shape = (jax.ShapeDtypeStruct((nblk, PEER_SLOTS, tm), jnp.int32), jax.ShapeDtypeStruct((nblk, PEER_SLOTS, tm), F32))
    specs = (pl.BlockSpec((None, PEER_SLOTS, tm), lambda i: (i, 0, 0)),
             pl.BlockSpec((None, PEER_SLOTS, tm), lambda i: (i, 0, 0)))
    scratch = [pltpu.VMEM((ROUTE_INTERLEAVE, PEER_TOPK, tm), F32), pltpu.VMEM((ROUTE_INTERLEAVE, PEER_TOPK, tm), jnp.int32)]
    return shape, specs, scratch


def _peer_route(q3, subkeys_bf16, n_tokens):
    tm = ROUTE_TILE
    nblk = n_tokens // tm
    out_shape, out_specs, out_scratch = _route_out(nblk, tm)
    return pl.pallas_call(
        functools.partial(_peer_route_kernel, tm=tm),
        out_shape=out_shape,
        grid=(nblk,),
        in_specs=[
            pl.BlockSpec((2 * PEER_HEADS, tm, PEER_HALF), lambda i: (0, i, 0)),
            pl.BlockSpec((2 * PEER_HEADS, PEER_KEYS, PEER_HALF), lambda i: (0, 0, 0)),
        ],
        out_specs=out_specs,
        scratch_shapes=[
            pltpu.VMEM((2 * PEER_HEADS, PEER_TOPK, tm), F32),
            pltpu.VMEM((2 * PEER_HEADS, PEER_TOPK, tm), jnp.int32),
        ] + out_scratch,
        compiler_params=_cparams(("parallel",)),
        name="peer_route",
    )(q3, subkeys_bf16)


def _peer_pairs(sv, si):
    nblk, _, _, tm = sv.shape
    out_shape, out_specs, out_scratch = _route_out(nblk, tm)
    list_spec = pl.BlockSpec((None, 2 * PEER_HEADS, PEER_TOPK, tm), lambda i: (i, 0, 0, 0))
    return pl.pallas_call(
        functools.partial(_peer_pairs_kernel, tm=tm),
        out_shape=out_shape,
        grid=(nblk,),
        in_specs=[list_spec, list_spec],
        out_specs=out_specs,
        scratch_shapes=out_scratch,
        compiler_params=_cparams(("parallel",)),
        name="peer_pairs",
    )(sv, si)


def _peer_apply_kernel(*refs, tb, side):
    if side:
        (idx_ref, gate_ref, h_ref, x_ref, gt_ref, uv_hbm, q_ref, sk_ref, o_ref, sv_ref, si_ref,
         buf_ref, sem_ref, part_ref, acc_ref, s_ref) = refs
    else:
        idx_ref, gate_ref, h_ref, x_ref, gt_ref, uv_hbm, o_ref, buf_ref, sem_ref, part_ref, acc_ref = refs
    lookahead = GATHER_SLOTS - 1
    n_groups = tb // TOKEN_GROUP
    n_tiles = D_MODEL // LANES
    n_slot_groups = PEER_SLOTS // SUBLANES

    def start_rows(tok, slot, k0, n):
        for k in range(k0, k0 + n):
            pltpu.make_async_copy(uv_hbm.at[idx_ref[tok, k]], buf_ref.at[slot, k],
                                  sem_ref.at[slot]).start(priority=k % 2)

    def wait_gather(slot):
        pltpu.make_async_copy(uv_hbm.at[pl.ds(0, PEER_SLOTS)], buf_ref.at[slot], sem_ref.at[slot]).wait()

    for tok in range(lookahead):
        start_rows(tok, tok % GATHER_SLOTS, 0, PEER_SLOTS)

    tok_lane = lax.broadcasted_iota(jnp.int32, (PEER_SLOTS, ROUTE_TILE), 1)
    sub = lax.broadcasted_iota(jnp.int32, (SUBLANES, LANES), 0)

    def fold(x0, x1, d):
        t0 = x0 + pltpu.roll(x0, SUBLANES - d, 0)
        t1 = x1 + pltpu.roll(x1, d, 0)
        return jnp.where((sub & d) == 0, t0, t1)

    key_id = lax.broadcasted_iota(jnp.int32, (PEER_KEYS, ROUTE_TILE), 0).astype(F32)
    groups_per_list_pair = PEER_TOPK // TOKEN_GROUP
    groups_per_tile = groups_per_list_pair * PEER_HEADS
    if side:
        assert n_groups * TOKEN_GROUP == tb and tb % ROUTE_TILE == 0 and groups_per_tile * TOKEN_GROUP == ROUTE_TILE

    def group(gi, is_last):
        base = gi * TOKEN_GROUP if is_last else pl.multiple_of(gi * TOKEN_GROUP, TOKEN_GROUP)
        h8 = h_ref[pl.ds(base, TOKEN_GROUP), :]
        rows = []
        if side:
            tile = gi // groups_per_tile
            hp0 = 2 * ((gi // groups_per_list_pair) % PEER_HEADS)
            r0 = (gi % groups_per_list_pair) * TOKEN_GROUP

            def load_scores():
                rows0 = tile * ROUTE_TILE if is_last else pl.multiple_of(tile * ROUTE_TILE, ROUTE_TILE)
                for pi in range(2):
                    s_ref[pi] = _dot_nt(sk_ref[hp0 + pi], q_ref[hp0 + pi, pl.ds(rows0, ROUTE_TILE), :].astype(BF16))

            if is_last:
                if r0 == 0:
                    load_scores()
            else:
                pl.when(r0 == 0)(load_scores)
        for j in range(TOKEN_GROUP):
            tok = base + j
            slot = j % GATHER_SLOTS
            prefetch = not is_last or j + lookahead < TOKEN_GROUP

            def start_ahead(k0, n, tok=tok, j=j, prefetch=prefetch):
                if prefetch:
                    start_rows(tok + lookahead, (j + lookahead) % GATHER_SLOTS, k0, n)

            wait_gather(slot)
            h_tile = jnp.concatenate([h8[j:j + 1, s * LANES:(s + 1) * LANES] for s in range(n_tiles)], axis=0)
            for g in range(n_slot_groups):
                p = [lax.bitcast_convert_type(buf_ref[slot, g * SUBLANES + c] & jnp.uint32(0xFFFF0000), F32) * h_tile
                     for c in range(SUBLANES)]
                y = [fold(p[c], p[c + 4], 4) for c in range(4)]
                z = [fold(y[c], y[c + 2], 2) for c in range(2)]
                part_ref[g * SUBLANES:(g + 1) * SUBLANES, :] = fold(z[0], z[1], 1)
                start_ahead(g * ISSUE_PER_GROUP, ISSUE_PER_GROUP)
            acc_ref[...] = jnp.zeros(acc_ref.shape, F32)
            start_ahead(n_slot_groups * ISSUE_PER_GROUP, PEER_SLOTS - 2 * n_slot_groups * ISSUE_PER_GROUP)
            if side:
                for pi in range(2):
                    sc = s_ref[pi]
                    m = jnp.max(sc, axis=0, keepdims=True)
                    idx = jnp.min(jnp.where(sc == m, key_id, _NO_ROW), axis=0, keepdims=True)
                    sv_ref[tile, hp0 + pi, r0 + j] = m
                    si_ref[tile, hp0 + pi, r0 + j] = idx.astype(jnp.int32)
                    s_ref[pi] = jnp.where(key_id == idx, -jnp.inf, sc)
            act = jnp.sum(part_ref[...], axis=1, keepdims=True)
            gates = gate_ref[tok // ROUTE_TILE]
            gate = jnp.sum(jnp.where(tok_lane == tok % ROUTE_TILE, gates, 0.0), axis=1, keepdims=True)
            w = gate * _gelu(act)
            for g in range(n_slot_groups):
                terms = [jnp.broadcast_to(w[k:k + 1, :], (SUBLANES, LANES))
                         * lax.bitcast_convert_type(buf_ref[slot, k] << 16, F32)
                         for k in range(g * SUBLANES, (g + 1) * SUBLANES)]
                s4 = [terms[i] + terms[i + 4] for i in range(4)]
                acc_ref[g % 2] = acc_ref[g % 2] + ((s4[0] + s4[1]) + (s4[2] + s4[3]))
                start_ahead(PEER_SLOTS - (n_slot_groups - g) * ISSUE_PER_GROUP, ISSUE_PER_GROUP)
            acc = acc_ref[0] + acc_ref[1]
            rows.append(jnp.concatenate([acc[s:s + 1, :] for s in range(n_tiles)], axis=1))
        out8 = jnp.concatenate(rows, axis=0)
        o_ref[pl.ds(base, TOKEN_GROUP), :] = x_ref[pl.ds(base, TOKEN_GROUP), :] + gt_ref[...] * out8

    def body(gi, carry):
        group(gi, False)
        return carry

    lax.fori_loop(0, n_groups - 1, body, 0)
    group(n_groups - 1, True)


def _peer_apply(experts_tok, gates, h2d, x2d, gt, uv_table, n_tokens, row0, rows_per_mod, side=None):
    tb = min(APPLY_TILE, n_tokens)
    tiles_per_mod = rows_per_mod // tb
    tile0 = row0 // tb
    in_specs = [
        pl.BlockSpec((tb, PEER_SLOTS), lambda i: (i, 0), memory_space=pltpu.SMEM),
        pl.BlockSpec((tb // ROUTE_TILE, PEER_SLOTS, ROUTE_TILE), lambda i: (i, 0, 0)),
        pl.BlockSpec((tb, D_MODEL), lambda i: (tile0 + i, 0)),
        pl.BlockSpec((tb, D_MODEL), lambda i: (tile0 + i, 0)),
        pl.BlockSpec((None, 1, D_MODEL), lambda i: ((tile0 + i) // tiles_per_mod, 0, 0)),
        pl.BlockSpec(memory_space=pl.ANY),
    ]
    args = [experts_tok, gates, h2d, x2d, gt, uv_table]
    out_shape = [jax.ShapeDtypeStruct((n_tokens, D_MODEL), F32)]
    out_specs = [pl.BlockSpec((tb, D_MODEL), lambda i: (i, 0))]
    scratch = [
        pltpu.VMEM((GATHER_SLOTS, PEER_SLOTS, SUBLANES, LANES), jnp.uint32),
        pltpu.SemaphoreType.DMA((GATHER_SLOTS,)),
        pltpu.VMEM((PEER_SLOTS, LANES), F32),
        pltpu.VMEM((2, SUBLANES, LANES), F32),
    ]
    if side is not None:
        q3, subkeys, side_row0 = side
        side_tile0 = side_row0 // tb
        in_specs += [
            pl.BlockSpec((2 * PEER_HEADS, tb, PEER_HALF), lambda i: (0, side_tile0 + i, 0)),
            pl.BlockSpec((2 * PEER_HEADS, PEER_KEYS, PEER_HALF), lambda i: (0, 0, 0)),
        ]
        args += [q3, subkeys]
        list_shape = (n_tokens // ROUTE_TILE, 2 * PEER_HEADS, PEER_TOPK, 1, ROUTE_TILE)
        list_block = (tb // ROUTE_TILE, 2 * PEER_HEADS, PEER_TOPK, 1, ROUTE_TILE)
        out_shape += [jax.ShapeDtypeStruct(list_shape, F32), jax.ShapeDtypeStruct(list_shape, jnp.int32)]
        out_specs += [pl.BlockSpec(list_block, lambda i: (i, 0, 0, 0, 0))] * 2
        scratch.append(pltpu.VMEM((2, PEER_KEYS, ROUTE_TILE), F32))
    outs = pl.pallas_call(
        functools.partial(_peer_apply_kernel, tb=tb, side=side is not None),
        out_shape=tuple(out_shape),
        grid=(n_tokens // tb,),
        in_specs=in_specs,
        out_specs=tuple(out_specs),
        scratch_shapes=scratch,
        compiler_params=_cparams(("arbitrary",), disable_bounds_checks=True),
        name="peer_apply_side" if side is not None else "peer_apply",
    )(*args)
    if side is None:
        return outs[0]
    sv, si = (a.reshape(a.shape[:3] + a.shape[4:]) for a in outs[1:])
    return outs[0], sv, si


def _final_norm_kernel(x_ref, g_ref, o_ref):
    x = x_ref[...]
    o_ref[...] = (x * lax.rsqrt(jnp.mean(x * x, axis=-1, keepdims=True) + EPS)) * g_ref[...]


def _final_norm(x2d, g):
    t = x2d.shape[0]
    tm = min(2 * TOKEN_TILE, t)
    return pl.pallas_call(
        _final_norm_kernel,
        out_shape=jax.ShapeDtypeStruct((t, D_MODEL), F32),
        grid=(t // tm,),
        in_specs=[pl.BlockSpec((tm, D_MODEL), lambda i: (i, 0)), pl.BlockSpec((1, D_MODEL), lambda i: (0, 0))],
        out_specs=pl.BlockSpec((tm, D_MODEL), lambda i: (i, 0)),
        compiler_params=_cparams(("parallel",)),
        name="final_norm",
    )(x2d, g)


def _rope_tables(length):
    rows = length // GRID_W
    row = jnp.repeat(jnp.arange(rows, dtype=F32), GRID_W)
    col = jnp.tile(jnp.arange(GRID_W, dtype=F32), rows)
    inv = ROPE_BASE ** (-jnp.arange(ROPE_FREQS, dtype=F32) / ROPE_FREQS)
    ang = jnp.stack([row[:, None] * inv, col[:, None] * inv], axis=1)
    cos, sin = jnp.cos(ang), jnp.sin(ang)
    cos_h = jnp.broadcast_to(cos[:, :, None, :], (length, 2, 2, ROPE_FREQS)).reshape(length, HEAD_DIM)
    sin_h = jnp.stack([-sin, sin], axis=2).reshape(length, HEAD_DIM)
    reps = LANES // HEAD_DIM
    return jnp.tile(cos_h, (1, reps)), jnp.tile(sin_h, (1, reps))


def _pack_expert_tables(u_tab, v_tab):
    def bits(a):
        return lax.bitcast_convert_type(a.astype(BF16), jnp.uint16).astype(jnp.uint32)

    return ((bits(u_tab) << 16) | bits(v_tab)).reshape(-1, D_MODEL // LANES, LANES)


def _experts_by_token(experts):
    return jnp.transpose(experts, (0, 2, 1)).reshape(-1, PEER_SLOTS)


def _peer_layer(x2d, g, sc, sh, gt, rows_per_mod, wq, subkeys, uv_table, n_chunks):
    t = x2d.shape[0]
    q3, h2d = _peer_query(x2d, g, sc, sh, wq, rows_per_mod)
    n = t // n_chunks
    experts, gates = _peer_route(q3, subkeys, n)
    outs = []
    for c in range(n_chunks):
        side = (q3, subkeys, (c + 1) * n) if c + 1 < n_chunks else None
        res = _peer_apply(_experts_by_token(experts), gates, h2d, x2d, gt, uv_table, n, c * n, rows_per_mod, side)
        if side is None:
            outs.append(res)
        else:
            outs.append(res[0])
            experts, gates = _peer_pairs(res[1], res[2])
    return outs[0] if n_chunks == 1 else jnp.concatenate(outs, axis=0)


def kernel(x, c, ctx, c_ctx, w_mod, b_mod, g_mix, g_ffn, w_in, attn_sink, pool_w, pool_scale, sgu_w, sgu_b,
           w_br_attn, w_br_pool, w_br_sgu, w_out, peer_wq, peer_subkeys, peer_u, peer_v, g_final):
    b, l, d = x.shape
    n_ctx = ctx.shape[1]
    depth = w_mod.shape[0]
    assert d == D_MODEL and l % TOKEN_TILE == 0 and n_ctx % SGU_CHUNK == 0

    n_rows = -(-(b + 1) // SUBLANES) * SUBLANES
    cond = jnp.zeros((n_rows, d), F32).at[:b].set(c).at[b].set(c_ctx)
    mod = _adaln(cond, w_mod, b_mod).reshape(depth, n_rows, N_MOD, 1, d)

    rope = _rope_tables(l)
    x2d = x.reshape(b * l, d)
    xc2d = ctx.reshape(b * n_ctx, d)

    for layer in range(depth):
        last = layer == depth - 1
        m_lat = [mod[layer, :b, i] for i in range(N_MOD)]
        m_ctx = [mod[layer, b:b + 1, i] for i in range(N_MOD)]
        g1 = g_mix[layer].reshape(1, d)
        g2 = g_ffn[layer].reshape(1, d)

        w_in_b = w_in[layer].astype(BF16)
        qkv, rest = _in_proj(x2d, g1, m_lat[1], m_lat[0], w_in_b, l, rope, l)
        qkv_c, rest_c = _in_proj(xc2d, g1, m_ctx[1], m_ctx[0], w_in_b, b * n_ctx, None, n_ctx)

        y_attn = _attention(qkv.reshape(b, l, QKV_W), qkv_c.reshape(b, n_ctx, QKV_W), attn_sink[layer], True)

        poolw_bd = jnp.zeros((POOL_W, POOL_W), F32)
        for gi in range(len(POOL_SIZES)):
            sl = slice(gi * POOL_GROUP_W, (gi + 1) * POOL_GROUP_W)
            poolw_bd = poolw_bd.at[sl, sl].set(pool_w[layer, gi])
        poolw_bd = poolw_bd.astype(BF16)
        pscale = pool_scale[layer].reshape(1, POOL_W)
        sguw = sgu_w[layer].astype(BF16)
        sgub_full = jnp.repeat(sgu_b[layer].T, SGU_GROUP_W, axis=1)
        wbr = jnp.concatenate([w_br_attn[layer], w_br_pool[layer], w_br_sgu[layer]], axis=0).astype(BF16)
        wout = w_out[layer].astype(BF16)
        mix_w = (poolw_bd, pscale, sguw, sgub_full, wbr, wout)

        x2d = _mixer(rest, y_attn.reshape(b * l, ATTN_W), x2d, m_lat[2], *mix_w, l, l)
        if not last:
            y_attn_c = _attention(qkv_c.reshape(b, n_ctx, QKV_W), qkv_c.reshape(b, n_ctx, QKV_W),
                                  attn_sink[layer], False)
            xc2d = _mixer(rest_c, y_attn_c.reshape(b * n_ctx, ATTN_W), xc2d, m_ctx[2], *mix_w,
                          b * n_ctx, n_ctx)

        wq = peer_wq[layer].astype(BF16)
        subkeys = peer_subkeys[layer].reshape(2 * PEER_HEADS, PEER_KEYS, PEER_HALF).astype(BF16)
        uv_table = _pack_expert_tables(peer_u[layer], peer_v[layer])
        n_chunks = (b * l) // PEER_CHUNK if (b * l) % PEER_CHUNK == 0 and l % PEER_CHUNK == 0 else b
        x2d = _peer_layer(x2d, g2, m_lat[4], m_lat[3], m_lat[5], l, wq, subkeys, uv_table, n_chunks)
        if not last:
            xc2d = _peer_layer(xc2d, g2, m_ctx[4], m_ctx[3], m_ctx[5], b * n_ctx, wq, subkeys, uv_table, 1)

    return _final_norm(x2d, g_final.reshape(1, d)).reshape(b, l, d)
```

```python
import functools

import jax
import jax.numpy as jnp
from jax import lax
from jax.experimental import pallas as pl
from jax.experimental.pallas import tpu as pltpu

F32 = jnp.float32
BF16 = jnp.bfloat16

D_MODEL = 1024
EPS = 1e-6
N_MOD = 6
GRID_W = 64

N_HEADS = 8
N_KV_HEADS = 2
GQA_GROUP = N_HEADS // N_KV_HEADS
HEAD_DIM = 64
WINDOW = 128
ATTN_BLOCK = 128
ATTN_W = N_HEADS * HEAD_DIM
KV_W = N_KV_HEADS * HEAD_DIM
QKV_W = ATTN_W + 2 * KV_W
ROPE_BASE = 10000.0
ROPE_FREQS = HEAD_DIM // 4

POOL_SIZES = (2, 4, 8, 16)
POOL_GROUP_W = 64
POOL_W = len(POOL_SIZES) * POOL_GROUP_W
POOL_HALO = 8

SGU_CHUNK = 128
SGU_GROUPS = 4
SGU_W = 256
SGU_GROUP_W = SGU_W // SGU_GROUPS

N_BRANCH = 3
REST_W = POOL_W + 2 * SGU_W + N_BRANCH * D_MODEL
IN_W = QKV_W + REST_W
BR_W = ATTN_W + POOL_W + SGU_W

PEER_HEADS = 8
PEER_KEYS = 128
PEER_HALF = 128
PEER_TOPK = 16
PEER_SLOTS = PEER_HEADS * PEER_TOPK
PEER_QW = PEER_HEADS * 2 * PEER_HALF

LANES = 128
SUBLANES = 8
VMEM_LIMIT = 48 * 1024 * 1024

TOKEN_TILE = 256
ROUTE_TILE = 128
APPLY_TILE = 256
GATHER_SLOTS = 8
TOKEN_GROUP = 8
ISSUE_PER_GROUP = 3
ROUTE_INTERLEAVE = 2
_NO_ROW = 1e9


def _cparams(sem, **kw):
    return pltpu.CompilerParams(dimension_semantics=sem, vmem_limit_bytes=VMEM_LIMIT, **kw)


def _dot(a, b):
    return jnp.dot(a, b, preferred_element_type=F32)


def _dot_nt(a, b):
    return lax.dot_general(a, b, (((1,), (1,)), ((), ())), preferred_element_type=F32)


def _split_bf16(a):
    hi = a.astype(BF16)
    lo = (a - hi.astype(F32)).astype(BF16)
    return hi, lo


def _adaln_kernel(c_ref, w_ref, b_ref, o_ref):
    c = c_ref[...]
    s = c * (1.0 / (1.0 + jnp.exp(-c)))
    s_hi, s_lo = _split_bf16(s)
    w_hi, w_lo = _split_bf16(w_ref[...])
    acc = _dot(s_hi, w_hi) + (_dot(s_hi, w_lo) + _dot(s_lo, w_hi))
    o_ref[...] = acc + b_ref[...]


def _adaln(cond, w_mod, b_mod):
    depth = w_mod.shape[0]
    r = cond.shape[0]
    n = w_mod.shape[2]
    tn = D_MODEL
    return pl.pallas_call(
        _adaln_kernel,
        out_shape=jax.ShapeDtypeStruct((depth, r, n), F32),
        grid=(depth, n // tn),
        in_specs=[
            pl.BlockSpec((r, D_MODEL), lambda l, j: (0, 0)),
            pl.BlockSpec((None, D_MODEL, tn), lambda l, j: (l, 0, j)),
            pl.BlockSpec((None, 1, tn), lambda l, j: (l, 0, j)),
        ],
        out_specs=pl.BlockSpec((None, r, tn), lambda l, j: (l, 0, j)),
        compiler_params=_cparams(("arbitrary", "arbitrary")),
        name="adaln",
    )(cond, w_mod, b_mod.reshape(depth, 1, n))


def _modulated_norm(x, g, sc, sh):
    y = x * lax.rsqrt(jnp.mean(x * x, axis=-1, keepdims=True) + EPS)
    return (y * g) * (1.0 + sc) + sh


def _rope_tile(x, cos, sin_signed):
    lane = lax.broadcasted_iota(jnp.int32, x.shape, 1)
    first_half = (lane % (2 * ROPE_FREQS)) < ROPE_FREQS
    partner = jnp.where(first_half, pltpu.roll(x, LANES - ROPE_FREQS, 1), pltpu.roll(x, ROPE_FREQS, 1))
    return x * cos + partner * sin_signed


def _in_proj_kernel(x_ref, g_ref, sc_ref, sh_ref, w_ref, *rest, rope):
    if rope:
        cos_ref, sin_ref, qkv_ref, rest_ref = rest
    else:
        qkv_ref, rest_ref = rest
    hb = _modulated_norm(x_ref[...], g_ref[...], sc_ref[...], sh_ref[...]).astype(BF16)
    for c2 in range(QKV_W // (2 * LANES)):
        acc2 = _dot(hb, w_ref[:, c2 * 2 * LANES:(c2 + 1) * 2 * LANES])
        for half in range(2):
            c = 2 * c2 + half
            acc = acc2[:, half * LANES:(half + 1) * LANES]
            if rope and c * LANES < ATTN_W + KV_W:
                acc = _rope_tile(acc, cos_ref[...], sin_ref[...])
            qkv_ref[:, c * LANES:(c + 1) * LANES] = acc
    chunk = 768
    for c in range(REST_W // chunk):
        rest_ref[:, c * chunk:(c + 1) * chunk] = _dot(hb, w_ref[:, QKV_W + c * chunk:QKV_W + (c + 1) * chunk])


def _in_proj(x2d, g, sc, sh, w_bf16, rows_per_mod, rope_tables, seq_len):
    t = x2d.shape[0]
    tm = min(TOKEN_TILE, t)
    tiles_per_mod = rows_per_mod // tm
    rope = rope_tables is not None
    in_specs = [
        pl.BlockSpec((tm, D_MODEL), lambda i: (i, 0)),
        pl.BlockSpec((1, D_MODEL), lambda i: (0, 0)),
        pl.BlockSpec((None, 1, D_MODEL), lambda i: (i // tiles_per_mod, 0, 0)),
        pl.BlockSpec((None, 1, D_MODEL), lambda i: (i // tiles_per_mod, 0, 0)),
        pl.BlockSpec((D_MODEL, IN_W), lambda i: (0, 0)),
    ]
    args = [x2d, g, sc, sh, w_bf16]
    if rope:
        tiles_per_seq = seq_len // tm
        in_specs += [pl.BlockSpec((tm, LANES), lambda i: (i % tiles_per_seq, 0))] * 2
        args += list(rope_tables)
    return pl.pallas_call(
        functools.partial(_in_proj_kernel, rope=rope),
        out_shape=(jax.ShapeDtypeStruct((t, QKV_W), F32), jax.ShapeDtypeStruct((t, REST_W), F32)),
        grid=(t // tm,),
        in_specs=in_specs,
        out_specs=(pl.BlockSpec((tm, QKV_W), lambda i: (i, 0)), pl.BlockSpec((tm, REST_W), lambda i: (i, 0))),
        compiler_params=_cparams(("parallel",)),
        name="in_proj",
    )(*args)


def _peer_query_kernel(x_ref, g_ref, sc_ref, sh_ref, w_ref, q_ref, h_ref):
    h = _modulated_norm(x_ref[...], g_ref[...], sc_ref[...], sh_ref[...])
    h_ref[...] = h
    hb = h.astype(BF16)
    for head in range(PEER_HEADS):
        acc = _dot(hb, w_ref[:, head * 2 * PEER_HALF:(head + 1) * 2 * PEER_HALF])
        q_ref[2 * head] = acc[:, :PEER_HALF]
        q_ref[2 * head + 1] = acc[:, PEER_HALF:]


def _peer_query(x2d, g, sc, sh, wq_bf16, rows_per_mod):
    t = x2d.shape[0]
    tm = min(TOKEN_TILE, t)
    tiles_per_mod = rows_per_mod // tm
    return pl.pallas_call(
        _peer_query_kernel,
        out_shape=(jax.ShapeDtypeStruct((2 * PEER_HEADS, t, PEER_HALF), F32),
                   jax.ShapeDtypeStruct((t, D_MODEL), F32)),
        grid=(t // tm,),
        in_specs=[
            pl.BlockSpec((tm, D_MODEL), lambda i: (i, 0)),
            pl.BlockSpec((1, D_MODEL), lambda i: (0, 0)),
            pl.BlockSpec((None, 1, D_MODEL), lambda i: (i // tiles_per_mod, 0, 0)),
            pl.BlockSpec((None, 1, D_MODEL), lambda i: (i // tiles_per_mod, 0, 0)),
            pl.BlockSpec((D_MODEL, PEER_QW), lambda i: (0, 0)),
        ],
        out_specs=(pl.BlockSpec((2 * PEER_HEADS, tm, PEER_HALF), lambda i: (0, i, 0)),
                   pl.BlockSpec((tm, D_MODEL), lambda i: (i, 0))),
        compiler_params=_cparams(("parallel",)),
        name="peer_query",
    )(x2d, g, sc, sh, wq_bf16)


def _attn_kernel(sink_ref, q_ref, *rest, window, n_blocks):
    if window:
        kvp_ref, kvc_ref, kvn_ref, ctx_ref, o_ref = rest
    else:
        ctx_ref, o_ref = rest
    n = pl.program_id(1)
    q = q_ref[...] * (HEAD_DIM ** -0.5)
    sources = []
    if window:
        qi = lax.broadcasted_iota(jnp.int32, (GQA_GROUP * ATTN_BLOCK, ATTN_BLOCK), 0) % ATTN_BLOCK
        kj = lax.broadcasted_iota(jnp.int32, (GQA_GROUP * ATTN_BLOCK, ATTN_BLOCK), 1)
        sources.append((kvp_ref[...], (kj >= qi) & (n > 0)))
        sources.append((kvc_ref[...], None))
        sources.append((kvn_ref[...], (kj <= qi) & (n < n_blocks - 1)))
    for c in range(ctx_ref.shape[0] // ATTN_BLOCK):
        sources.append((ctx_ref[c * ATTN_BLOCK:(c + 1) * ATTN_BLOCK, :], None))
    kb = [[kv[:, g * HEAD_DIM:(g + 1) * HEAD_DIM].astype(BF16) for g in range(N_KV_HEADS)] for kv, _ in sources]
    ones = jnp.ones((ATTN_BLOCK, HEAD_DIM), BF16)
    vb = [[jnp.concatenate([kv[:, KV_W + g * HEAD_DIM:KV_W + (g + 1) * HEAD_DIM].astype(BF16), ones], axis=1)
           for g in range(N_KV_HEADS)] for kv, _ in sources]
    groups = [range(g * GQA_GROUP, (g + 1) * GQA_GROUP) for g in range(N_KV_HEADS)]
    sinks = [jnp.concatenate([jnp.full((ATTN_BLOCK, 1), sink_ref[h], F32) for h in heads], axis=0)
             for heads in groups]
    scores = []
    for g, heads in enumerate(groups):
        qg = jnp.concatenate([q[:, h * HEAD_DIM:(h + 1) * HEAD_DIM] for h in heads], axis=0).astype(BF16)
        sg = []
        for si, (_, valid) in enumerate(sources):
            s = _dot_nt(qg, kb[si][g])
            sg.append(s if valid is None else jnp.where(valid, s, -jnp.inf))
        scores.append(sg)
    maxes = [jnp.maximum(jnp.max(functools.reduce(jnp.maximum, scores[g]), axis=-1, keepdims=True), sinks[g])
             for g in range(N_KV_HEADS)]
    outs = []
    for g in range(N_KV_HEADS):
        m = maxes[g]
        acc = jnp.zeros((GQA_GROUP * ATTN_BLOCK, 2 * HEAD_DIM), F32)
        for si, s in enumerate(scores[g]):
            acc = acc + _dot(jnp.exp(s - m).astype(BF16), vb[si][g])
        og = acc[:, :HEAD_DIM] / (acc[:, HEAD_DIM:] + jnp.exp(sinks[g] - m))
        outs += [og[i * ATTN_BLOCK:(i + 1) * ATTN_BLOCK, :] for i in range(GQA_GROUP)]
    o_ref[...] = jnp.concatenate(outs, axis=-1)


def _attention(qkv, kv_ctx, sink, window):
    b, l, _ = qkv.shape
    n_ctx = kv_ctx.shape[1]
    nb = l // ATTN_BLOCK
    kv_col = ATTN_W // (2 * KV_W)
    in_specs = [
        pl.BlockSpec(memory_space=pltpu.SMEM),
        pl.BlockSpec((None, ATTN_BLOCK, ATTN_W), lambda bi, n: (bi, n, 0)),
    ]
    args = [sink, qkv]
    if window:
        in_specs += [
            pl.BlockSpec((None, ATTN_BLOCK, 2 * KV_W), lambda bi, n: (bi, jnp.maximum(n - 1, 0), kv_col)),
            pl.BlockSpec((None, ATTN_BLOCK, 2 * KV_W), lambda bi, n: (bi, n, kv_col)),
            pl.BlockSpec((None, ATTN_BLOCK, 2 * KV_W), lambda bi, n: (bi, jnp.minimum(n + 1, nb - 1), kv_col)),
        ]
        args += [qkv, qkv, qkv]
    in_specs.append(pl.BlockSpec((None, n_ctx, 2 * KV_W), lambda bi, n: (bi, 0, kv_col)))
    args.append(kv_ctx)
    return pl.pallas_call(
        functools.partial(_attn_kernel, window=window, n_blocks=nb),
        out_shape=jax.ShapeDtypeStruct((b, l, ATTN_W), F32),
        grid=(b, nb),
        in_specs=in_specs,
        out_specs=pl.BlockSpec((None, ATTN_BLOCK, ATTN_W), lambda bi, n: (bi, n, 0)),
        compiler_params=_cparams(("parallel", "parallel")),
        name="attention_window" if window else "attention_context",
    )(*args)


def _gelu(x):
    return jax.nn.gelu(x, approximate=True)


def _sigmoid(x):
    return 1.0 / (1.0 + jnp.exp(-x))


def _mixer_kernel(rest_ref, hprev_ref, hnext_ref, ya_ref, x_ref, gt_ref, poolw_ref, pscale_ref, sguw_ref,
                  sgub_ref, wbr_ref, wout_ref, o_ref, ext_ref, *, tm, tiles_per_seq, seq_len):
    ti = pl.program_id(0) % tiles_per_seq
    z = rest_ref[:, 0:POOL_W]
    ext_ref[0:POOL_HALO, :] = jnp.where(ti > 0, hprev_ref[...], 0.0)
    ext_ref[POOL_HALO:POOL_HALO + tm, :] = z
    ext_ref[POOL_HALO + tm:2 * POOL_HALO + tm, :] = jnp.where(ti < tiles_per_seq - 1, hnext_ref[...], 0.0)
    pos = ti * tm + lax.broadcasted_iota(jnp.int32, (tm, LANES), 0)
    lane = lax.broadcasted_iota(jnp.int32, (tm, LANES), 1)

    def count(size):
        hi = jnp.minimum(pos + size // 2, seq_len)
        lo = jnp.maximum(pos - size // 2, 0)
        return (hi - lo).astype(F32)

    diffs = []
    for lt in range(POOL_W // LANES):
        cols = slice(lt * LANES, (lt + 1) * LANES)

        def shifted(d, cols=cols):
            return ext_ref[POOL_HALO + d:POOL_HALO + d + tm, cols]

        small, large = POOL_SIZES[2 * lt], POOL_SIZES[2 * lt + 1]
        s = shifted(-1) + shifted(0)
        width = 2
        sums = {}
        while width <= large:
            sums[width] = s
            half = width // 2
            if 2 * width <= large:
                for d in range(half, width):
                    s = s + shifted(-d - 1) + shifted(d)
            width *= 2
        mean = jnp.where(lane < POOL_GROUP_W, sums[small] / count(small), sums[large] / count(large))
        diffs.append(mean - z[:, cols])
    d = jnp.concatenate(diffs, axis=-1).astype(BF16)
    y_pool = _dot(d, poolw_ref[...]) * pscale_ref[...]

    u = _gelu(rest_ref[:, POOL_W:POOL_W + SGU_W])
    v = _gelu(rest_ref[:, POOL_W + SGU_W:POOL_W + 2 * SGU_W])
    vn = (v * lax.rsqrt(jnp.mean(v * v, axis=-1, keepdims=True) + EPS)).astype(BF16)
    group = lax.broadcasted_iota(jnp.int32, (SGU_CHUNK, SGU_W), 1) // SGU_GROUP_W
    mixed = []
    for c in range(tm // SGU_CHUNK):
        vc = vn[c * SGU_CHUNK:(c + 1) * SGU_CHUNK, :]
        mc = sgub_ref[...]
        for hg in range(SGU_GROUPS):
            mc = mc + jnp.where(group == hg, _dot(sguw_ref[hg], vc), 0.0)
        mixed.append(mc)
    y_sgu = u * jnp.concatenate(mixed, axis=0)

    g_off = POOL_W + 2 * SGU_W
    merged = _sigmoid(rest_ref[:, g_off:g_off + D_MODEL]) * _dot(ya_ref[...].astype(BF16), wbr_ref[0:ATTN_W, :])
    merged = merged + _sigmoid(rest_ref[:, g_off + D_MODEL:g_off + 2 * D_MODEL]) * _dot(
        y_pool.astype(BF16), wbr_ref[ATTN_W:ATTN_W + POOL_W, :])
    merged = merged + _sigmoid(rest_ref[:, g_off + 2 * D_MODEL:g_off + 3 * D_MODEL]) * _dot(
        y_sgu.astype(BF16), wbr_ref[ATTN_W + POOL_W:BR_W, :])
    o_ref[...] = x_ref[...] + gt_ref[...] * _dot(merged.astype(BF16), wout_ref[...])


def _mixer(rest, y_attn, x2d, gt, poolw_bd, pscale, sguw, sgub_full, wbr, wout, rows_per_mod, seq_len):
    t = x2d.shape[0]
    tm = min(TOKEN_TILE, seq_len)
    tiles_per_seq = seq_len // tm
    tiles_per_mod = rows_per_mod // tm
    halo_blocks = tm // POOL_HALO
    n_halo = t // POOL_HALO
    kern = functools.partial(_mixer_kernel, tm=tm, tiles_per_seq=tiles_per_seq, seq_len=seq_len)
    return pl.pallas_call(
        kern,
        out_shape=jax.ShapeDtypeStruct((t, D_MODEL), F32),
        grid=(t // tm,),
        in_specs=[
            pl.BlockSpec((tm, REST_W), lambda i: (i, 0)),
            pl.BlockSpec((POOL_HALO, POOL_W), lambda i: (jnp.maximum(i * halo_blocks - 1, 0), 0)),
            pl.BlockSpec((POOL_HALO, POOL_W), lambda i: (jnp.minimum((i + 1) * halo_blocks, n_halo - 1), 0)),
            pl.BlockSpec((tm, ATTN_W), lambda i: (i, 0)),
            pl.BlockSpec((tm, D_MODEL), lambda i: (i, 0)),
            pl.BlockSpec((None, 1, D_MODEL), lambda i: (i // tiles_per_mod, 0, 0)),
            pl.BlockSpec((POOL_W, POOL_W), lambda i: (0, 0)),
            pl.BlockSpec((1, POOL_W), lambda i: (0, 0)),
            pl.BlockSpec((SGU_GROUPS, SGU_CHUNK, SGU_CHUNK), lambda i: (0, 0, 0)),
            pl.BlockSpec((SGU_CHUNK, SGU_W), lambda i: (0, 0)),
            pl.BlockSpec((BR_W, D_MODEL), lambda i: (0, 0)),
            pl.BlockSpec((D_MODEL, D_MODEL), lambda i: (0, 0)),
        ],
        out_specs=pl.BlockSpec((tm, D_MODEL), lambda i: (i, 0)),
        scratch_shapes=[pltpu.VMEM((tm + 2 * POOL_HALO, POOL_W), F32)],
        compiler_params=_cparams(("parallel",)),
        name="mixer",
    )(rest, rest, rest, y_attn, x2d, gt, poolw_bd, pscale, sguw, sgub_full, wbr, wout)


def _top16_rows(problems, emit):
    scores = [p[0] for p in problems]
    for r in range(PEER_TOPK):
        for i, (_, ids) in enumerate(problems):
            s = scores[i]
            m = jnp.max(s, axis=0, keepdims=True)
            idx = jnp.min(jnp.where(s == m, ids, _NO_ROW), axis=0, keepdims=True)
            hit = ids == idx
            emit(i, r, m, hit, idx)
            scores[i] = jnp.where(hit, -jnp.inf, s)


def _pair_candidates(sv0, sv1, si0, si1, sub):
    subf = sub.astype(F32)
    sums, experts, flat = [], [], []
    for a in range(SUBLANES):
        n_b = PEER_TOPK // (a + 1)
        for b0 in range(0, n_b, SUBLANES):
            s = sv0[a:a + 1, :] + sv1[b0:b0 + SUBLANES, :]
            if n_b - b0 < SUBLANES:
                s = jnp.where(sub < n_b - b0, s, -jnp.inf)
            sums.append(s)
            experts.append(si0[a:a + 1, :] * PEER_KEYS + si1[b0:b0 + SUBLANES, :])
            flat.append(subf + float(a * PEER_TOPK + b0))
    assert PEER_TOPK // (SUBLANES + 1) == 1
    sums.append(sv0[SUBLANES:, :] + sv1[0:1, :])
    experts.append(si0[SUBLANES:, :] * PEER_KEYS + si1[0:1, :])
    flat.append((subf + float(SUBLANES)) * float(PEER_TOPK))
    return jnp.concatenate(sums, axis=0), jnp.concatenate(experts, axis=0), jnp.concatenate(flat, axis=0)


def _route_sub_keys(q_ref, sk_ref, sv_ref, si_ref, tm):
    key_id = lax.broadcasted_iota(jnp.int32, (PEER_KEYS, tm), 0).astype(F32)

    def sub_keys(it, carry):
        hps = [it * ROUTE_INTERLEAVE + u for u in range(ROUTE_INTERLEAVE)]
        problems = [(_dot_nt(sk_ref[hp], q_ref[hp].astype(BF16)), key_id) for hp in hps]

        def emit(i, r, m, hit, idx):
            sv_ref[hps[i], r:r + 1, :] = m
            si_ref[hps[i], r:r + 1, :] = idx.astype(jnp.int32)

        _top16_rows(problems, emit)
        return carry

    lax.fori_loop(0, 2 * PEER_HEADS // ROUTE_INTERLEAVE, sub_keys, 0)


def _route_pairs(sv_ref, si_ref, e_ref, g_ref, ts_ref, te_ref, tm):
    sub = lax.broadcasted_iota(jnp.int32, (SUBLANES, tm), 0)

    def heads(it, carry):
        hs = [it * ROUTE_INTERLEAVE + u for u in range(ROUTE_INTERLEAVE)]
        problems, experts = [], []
        for h in hs:
            cs, ce, cid = _pair_candidates(sv_ref[2 * h], sv_ref[2 * h + 1], si_ref[2 * h], si_ref[2 * h + 1], sub)
            problems.append((cs, cid))
            experts.append(ce)

        def emit(i, r, m, hit, idx):
            ts_ref[i, r:r + 1, :] = m
            te_ref[i, r:r + 1, :] = jnp.max(jnp.where(hit, experts[i], -1), axis=0, keepdims=True)

        _top16_rows(problems, emit)
        for i, h in enumerate(hs):
            row0 = pl.multiple_of(h * PEER_TOPK, PEER_TOPK)
            e_ref[pl.ds(row0, PEER_TOPK), :] = te_ref[i]
            ts = ts_ref[i]
            ex = jnp.exp(ts - jnp.max(ts, axis=0, keepdims=True))
            g_ref[pl.ds(row0, PEER_TOPK), :] = ex / jnp.sum(ex, axis=0, keepdims=True)
        return carry

    lax.fori_loop(0, PEER_HEADS // ROUTE_INTERLEAVE, heads, 0)


def _peer_route_kernel(q_ref, sk_ref, e_ref, g_ref, sv_ref, si_ref, ts_ref, te_ref, *, tm):
    _route_sub_keys(q_ref, sk_ref, sv_ref, si_ref, tm)
    _route_pairs(sv_ref, si_ref, e_ref, g_ref, ts_ref, te_ref, tm)


def _peer_pairs_kernel(sv_ref, si_ref, e_ref, g_ref, ts_ref, te_ref, *, tm):
    _route_pairs(sv_ref, si_ref, e_ref, g_ref, ts_ref, te_ref, tm)


def _route_out(nblk, tm):
    shape = (jax.ShapeDtypeStruct((nblk, PEER_SLOTS, tm), jnp.int32), jax.ShapeDtypeStruct((nblk, PEER_SLOTS, tm), F32))
    specs = (pl.BlockSpec((None, PEER_SLOTS, tm), lambda i: (i, 0, 0)),
             pl.BlockSpec((None, PEER_SLOTS, tm), lambda i: (i, 0, 0)))
    scratch = [pltpu.VMEM((ROUTE_INTERLEAVE, PEER_TOPK, tm), F32), pltpu.VMEM((ROUTE_INTERLEAVE, PEER_TOPK, tm), jnp.int32)]
    return shape, specs, scratch


def _peer_route(q3, subkeys_bf16, n_tokens):
    tm = ROUTE_TILE
    nblk = n_tokens // tm
    out_shape, out_specs, out_scratch = _route_out(nblk, tm)
    return pl.pallas_call(
        functools.partial(_peer_route_kernel, tm=tm),
        out_shape=out_shape,
        grid=(nblk,),
        in_specs=[
            pl.BlockSpec((2 * PEER_HEADS, tm, PEER_HALF), lambda i: (0, i, 0)),
            pl.BlockSpec((2 * PEER_HEADS, PEER_KEYS, PEER_HALF), lambda i: (0, 0, 0)),
        ],
        out_specs=out_specs,
        scratch_shapes=[
            pltpu.VMEM((2 * PEER_HEADS, PEER_TOPK, tm), F32),
            pltpu.VMEM((2 * PEER_HEADS, PEER_TOPK, tm), jnp.int32),
        ] + out_scratch,
        compiler_params=_cparams(("parallel",)),
        name="peer_route",
    )(q3, subkeys_bf16)


def _peer_pairs(sv, si):
    nblk, _, _, tm = sv.shape
    out_shape, out_specs, out_scratch = _route_out(nblk, tm)
    list_spec = pl.BlockSpec((None, 2 * PEER_HEADS, PEER_TOPK, tm), lambda i: (i, 0, 0, 0))
    return pl.pallas_call(
        functools.partial(_peer_pairs_kernel, tm=tm),
        out_shape=out_shape,
        grid=(nblk,),
        in_specs=[list_spec, list_spec],
        out_specs=out_specs,
        scratch_shapes=out_scratch,
        compiler_params=_cparams(("parallel",)),
        name="peer_pairs",
    )(sv, si)


def _peer_apply_kernel(*refs, tb, side):
    if side:
        (idx_ref, gate_ref, h_ref, x_ref, gt_ref, uv_hbm, q_ref, sk_ref, o_ref, sv_ref, si_ref,
         buf_ref, sem_ref, part_ref, acc_ref, s_ref) = refs
    else:
        idx_ref, gate_ref, h_ref, x_ref, gt_ref, uv_hbm, o_ref, buf_ref, sem_ref, part_ref, acc_ref = refs
    lookahead = GATHER_SLOTS - 1
    n_groups = tb // TOKEN_GROUP
    n_tiles = D_MODEL // LANES
    n_slot_groups = PEER_SLOTS // SUBLANES

    def start_rows(tok, slot, k0, n):
        for k in range(k0, k0 + n):
            pltpu.make_async_copy(uv_hbm.at[idx_ref[tok, k]], buf_ref.at[slot, k],
                                  sem_ref.at[slot]).start(priority=k % 2)

    def wait_gather(slot):
        pltpu.make_async_copy(uv_hbm.at[pl.ds(0, PEER_SLOTS)], buf_ref.at[slot], sem_ref.at[slot]).wait()

    for tok in range(lookahead):
        start_rows(tok, tok % GATHER_SLOTS, 0, PEER_SLOTS)

    tok_lane = lax.broadcasted_iota(jnp.int32, (PEER_SLOTS, ROUTE_TILE), 1)
    sub = lax.broadcasted_iota(jnp.int32, (SUBLANES, LANES), 0)

    def fold(x0, x1, d):
        t0 = x0 + pltpu.roll(x0, SUBLANES - d, 0)
        t1 = x1 + pltpu.roll(x1, d, 0)
        return jnp.where((sub & d) == 0, t0, t1)

    key_id = lax.broadcasted_iota(jnp.int32, (PEER_KEYS, ROUTE_TILE), 0).astype(F32)
    groups_per_list_pair = PEER_TOPK // TOKEN_GROUP
    groups_per_tile = groups_per_list_pair * PEER_HEADS
    if side:
        assert n_groups * TOKEN_GROUP == tb and tb % ROUTE_TILE == 0 and groups_per_tile * TOKEN_GROUP == ROUTE_TILE

    def group(gi, is_last):
        base = gi * TOKEN_GROUP if is_last else pl.multiple_of(gi * TOKEN_GROUP, TOKEN_GROUP)
        h8 = h_ref[pl.ds(base, TOKEN_GROUP), :]
        rows = []
        if side:
            tile = gi // groups_per_tile
            hp0 = 2 * ((gi // groups_per_list_pair) % PEER_HEADS)
            r0 = (gi % groups_per_list_pair) * TOKEN_GROUP

            def load_scores():
                rows0 = tile * ROUTE_TILE if is_last else pl.multiple_of(tile * ROUTE_TILE, ROUTE_TILE)
                for pi in range(2):
                    s_ref[pi] = _dot_nt(sk_ref[hp0 + pi], q_ref[hp0 + pi, pl.ds(rows0, ROUTE_TILE), :].astype(BF16))

            if is_last:
                if r0 == 0:
                    load_scores()
            else:
                pl.when(r0 == 0)(load_scores)
        for j in range(TOKEN_GROUP):
            tok = base + j
            slot = j % GATHER_SLOTS
            prefetch = not is_last or j + lookahead < TOKEN_GROUP

            def start_ahead(k0, n, tok=tok, j=j, prefetch=prefetch):
                if prefetch:
                    start_rows(tok + lookahead, (j + lookahead) % GATHER_SLOTS, k0, n)

            wait_gather(slot)
            h_tile = jnp.concatenate([h8[j:j + 1, s * LANES:(s + 1) * LANES] for s in range(n_tiles)], axis=0)
            for g in range(n_slot_groups):
                p = [lax.bitcast_convert_type(buf_ref[slot, g * SUBLANES + c] & jnp.uint32(0xFFFF0000), F32) * h_tile
                     for c in range(SUBLANES)]
                y = [fold(p[c], p[c + 4], 4) for c in range(4)]
                z = [fold(y[c], y[c + 2], 2) for c in range(2)]
                part_ref[g * SUBLANES:(g + 1) * SUBLANES, :] = fold(z[0], z[1], 1)
                start_ahead(g * ISSUE_PER_GROUP, ISSUE_PER_GROUP)
            acc_ref[...] = jnp.zeros(acc_ref.shape, F32)
            start_ahead(n_slot_groups * ISSUE_PER_GROUP, PEER_SLOTS - 2 * n_slot_groups * ISSUE_PER_GROUP)
            if side:
                for pi in range(2):
                    sc = s_ref[pi]
                    m = jnp.max(sc, axis=0, keepdims=True)
                    idx = jnp.min(jnp.where(sc == m, key_id, _NO_ROW), axis=0, keepdims=True)
                    sv_ref[tile, hp0 + pi, r0 + j] = m
                    si_ref[tile, hp0 + pi, r0 + j] = idx.astype(jnp.int32)
                    s_ref[pi] = jnp.where(key_id == idx, -jnp.inf, sc)
            act = jnp.sum(part_ref[...], axis=1, keepdims=True)
            gates = gate_ref[tok // ROUTE_TILE]
            gate = jnp.sum(jnp.where(tok_lane == tok % ROUTE_TILE, gates, 0.0), axis=1, keepdims=True)
            w = gate * _gelu(act)
            for g in range(n_slot_groups):
                terms = [jnp.broadcast_to(w[k:k + 1, :], (SUBLANES, LANES))
                         * lax.bitcast_convert_type(buf_ref[slot, k] << 16, F32)
                         for k in range(g * SUBLANES, (g + 1) * SUBLANES)]
                s4 = [terms[i] + terms[i + 4] for i in range(4)]
                acc_ref[g % 2] = acc_ref[g % 2] + ((s4[0] + s4[1]) + (s4[2] + s4[3]))
                start_ahead(PEER_SLOTS - (n_slot_groups - g) * ISSUE_PER_GROUP, ISSUE_PER_GROUP)
            acc = acc_ref[0] + acc_ref[1]
            rows.append(jnp.concatenate([acc[s:s + 1, :] for s in range(n_tiles)], axis=1))
        out8 = jnp.concatenate(rows, axis=0)
        o_ref[pl.ds(base, TOKEN_GROUP), :] = x_ref[pl.ds(base, TOKEN_GROUP), :] + gt_ref[...] * out8

    def body(gi, carry):
        group(gi, False)
        return carry

    lax.fori_loop(0, n_groups - 1, body, 0)
    group(n_groups - 1, True)


def _peer_apply(experts_tok, gates, h2d, x2d, gt, uv_table, n_tokens, row0, rows_per_mod, side=None):
    tb = min(APPLY_TILE, n_tokens)
    tiles_per_mod = rows_per_mod // tb
    tile0 = row0 // tb
    in_specs = [
        pl.BlockSpec((tb, PEER_SLOTS), lambda i: (i, 0), memory_space=pltpu.SMEM),
        pl.BlockSpec((tb // ROUTE_TILE, PEER_SLOTS, ROUTE_TILE), lambda i: (i, 0, 0)),
        pl.BlockSpec((tb, D_MODEL), lambda i: (tile0 + i, 0)),
        pl.BlockSpec((tb, D_MODEL), lambda i: (tile0 + i, 0)),
        pl.BlockSpec((None, 1, D_MODEL), lambda i: ((tile0 + i) // tiles_per_mod, 0, 0)),
        pl.BlockSpec(memory_space=pl.ANY),
    ]
    args = [experts_tok, gates, h2d, x2d, gt, uv_table]
    out_shape = [jax.ShapeDtypeStruct(x2d.shape, F32)]
    out_specs = [pl.BlockSpec((tb, D_MODEL), lambda i: (tile0 + i, 0))]
    scratch = [
        pltpu.VMEM((GATHER_SLOTS, PEER_SLOTS, SUBLANES, LANES), jnp.uint32),
        pltpu.SemaphoreType.DMA((GATHER_SLOTS,)),
        pltpu.VMEM((PEER_SLOTS, LANES), F32),
        pltpu.VMEM((2, SUBLANES, LANES), F32),
    ]
    if side is not None:
        q3, subkeys, side_row0 = side
        side_tile0 = side_row0 // tb
        in_specs += [
            pl.BlockSpec((2 * PEER_HEADS, tb, PEER_HALF), lambda i: (0, side_tile0 + i, 0)),
            pl.BlockSpec((2 * PEER_HEADS, PEER_KEYS, PEER_HALF), lambda i: (0, 0, 0)),
        ]
        args += [q3, subkeys]
        list_shape = (n_tokens // ROUTE_TILE, 2 * PEER_HEADS, PEER_TOPK, 1, ROUTE_TILE)
        list_block = (tb // ROUTE_TILE, 2 * PEER_HEADS, PEER_TOPK, 1, ROUTE_TILE)
        out_shape += [jax.ShapeDtypeStruct(list_shape, F32), jax.ShapeDtypeStruct(list_shape, jnp.int32)]
        out_specs += [pl.BlockSpec(list_block, lambda i: (i, 0, 0, 0, 0))] * 2
        scratch.append(pltpu.VMEM((2, PEER_KEYS, ROUTE_TILE), F32))
    outs = pl.pallas_call(
        functools.partial(_peer_apply_kernel, tb=tb, side=side is not None),
        out_shape=tuple(out_shape),
        grid=(n_tokens // tb,),
        in_specs=in_specs,
        out_specs=tuple(out_specs),
        scratch_shapes=scratch,
        input_output_aliases={3: 0},
        compiler_params=_cparams(("arbitrary",), disable_bounds_checks=True),
        name="peer_apply_side" if side is not None else "peer_apply",
    )(*args)
    if side is None:
        return outs[0]
    sv, si = (a.reshape(a.shape[:3] + a.shape[4:]) for a in outs[1:])
    return outs[0], sv, si


def _final_norm_kernel(x_ref, g_ref, o_ref):
    x = x_ref[...]
    o_ref[...] = (x * lax.rsqrt(jnp.mean(x * x, axis=-1, keepdims=True) + EPS)) * g_ref[...]


def _final_norm(x2d, g):
    t = x2d.shape[0]
    tm = min(2 * TOKEN_TILE, t)
    return pl.pallas_call(
        _final_norm_kernel,
        out_shape=jax.ShapeDtypeStruct((t, D_MODEL), F32),
        grid=(t // tm,),
        in_specs=[pl.BlockSpec((tm, D_MODEL), lambda i: (i, 0)), pl.BlockSpec((1, D_MODEL), lambda i: (0, 0))],
        out_specs=pl.BlockSpec((tm, D_MODEL), lambda i: (i, 0)),
        compiler_params=_cparams(("parallel",)),
        name="final_norm",
    )(x2d, g)


def _rope_tables(length):
    rows = length // GRID_W
    row = jnp.repeat(jnp.arange(rows, dtype=F32), GRID_W)
    col = jnp.tile(jnp.arange(GRID_W, dtype=F32), rows)
    inv = ROPE_BASE ** (-jnp.arange(ROPE_FREQS, dtype=F32) / ROPE_FREQS)
    ang = jnp.stack([row[:, None] * inv, col[:, None] * inv], axis=1)
    cos, sin = jnp.cos(ang), jnp.sin(ang)
    cos_h = jnp.broadcast_to(cos[:, :, None, :], (length, 2, 2, ROPE_FREQS)).reshape(length, HEAD_DIM)
    sin_h = jnp.stack([-sin, sin], axis=2).reshape(length, HEAD_DIM)
    reps = LANES // HEAD_DIM
    return jnp.tile(cos_h, (1, reps)), jnp.tile(sin_h, (1, reps))


def _pack_expert_tables(u_tab, v_tab):
    def bits(a):
        return lax.bitcast_convert_type(a.astype(BF16), jnp.uint16).astype(jnp.uint32)

    return ((bits(u_tab) << 16) | bits(v_tab)).reshape(-1, D_MODEL // LANES, LANES)


def _experts_by_token(experts):
    return jnp.transpose(experts, (0, 2, 1)).reshape(-1, PEER_SLOTS)


def _peer_layer(x2d, g, sc, sh, gt, rows_per_mod, wq, subkeys, uv_table, n_chunks):
    t = x2d.shape[0]
    q3, h2d = _peer_query(x2d, g, sc, sh, wq, rows_per_mod)
    n = t // n_chunks
    experts, gates = _peer_route(q3, subkeys, n)
    for c in range(n_chunks):
        side = (q3, subkeys, (c + 1) * n) if c + 1 < n_chunks else None
        res = _peer_apply(_experts_by_token(experts), gates, h2d, x2d, gt, uv_table, n, c * n, rows_per_mod, side)
        if side is None:
            x2d = res
        else:
            x2d = res[0]
            experts, gates = _peer_pairs(res[1], res[2])
    return x2d


def kernel(x, c, ctx, c_ctx, w_mod, b_mod, g_mix, g_ffn, w_in, attn_sink, pool_w, pool_scale, sgu_w, sgu_b,
           w_br_attn, w_br_pool, w_br_sgu, w_out, peer_wq, peer_subkeys, peer_u, peer_v, g_final):
    b, l, d = x.shape
    n_ctx = ctx.shape[1]
    depth = w_mod.shape[0]
    assert d == D_MODEL and l % TOKEN_TILE == 0 and n_ctx % SGU_CHUNK == 0

    n_rows = -(-(b + 1) // SUBLANES) * SUBLANES
    cond = jnp.zeros((n_rows, d), F32).at[:b].set(c).at[b].set(c_ctx)
    mod = _adaln(cond, w_mod, b_mod).reshape(depth, n_rows, N_MOD, 1, d)

    rope = _rope_tables(l)
    x2d = x.reshape(b * l, d)
    xc2d = ctx.reshape(b * n_ctx, d)

    for layer in range(depth):
        last = layer == depth - 1
        m_lat = [mod[layer, :b, i] for i in range(N_MOD)]
        m_ctx = [mod[layer, b:b + 1, i] for i in range(N_MOD)]
        g1 = g_mix[layer].reshape(1, d)
        g2 = g_ffn[layer].reshape(1, d)

        w_in_b = w_in[layer].astype(BF16)
        qkv, rest = _in_proj(x2d, g1, m_lat[1], m_lat[0], w_in_b, l, rope, l)
        qkv_c, rest_c = _in_proj(xc2d, g1, m_ctx[1], m_ctx[0], w_in_b, b * n_ctx, None, n_ctx)

        y_attn = _attention(qkv.reshape(b, l, QKV_W), qkv_c.reshape(b, n_ctx, QKV_W), attn_sink[layer], True)

        poolw_bd = jnp.zeros((POOL_W, POOL_W), F32)
        for gi in range(len(POOL_SIZES)):
            sl = slice(gi * POOL_GROUP_W, (gi + 1) * POOL_GROUP_W)
            poolw_bd = poolw_bd.at[sl, sl].set(pool_w[layer, gi])
        poolw_bd = poolw_bd.astype(BF16)
        pscale = pool_scale[layer].reshape(1, POOL_W)
        sguw = sgu_w[layer].astype(BF16)
        sgub_full = jnp.repeat(sgu_b[layer].T, SGU_GROUP_W, axis=1)
        wbr = jnp.concatenate([w_br_attn[layer], w_br_pool[layer], w_br_sgu[layer]], axis=0).astype(BF16)
        wout = w_out[layer].astype(BF16)
        mix_w = (poolw_bd, pscale, sguw, sgub_full, wbr, wout)

        x2d = _mixer(rest, y_attn.reshape(b * l, ATTN_W), x2d, m_lat[2], *mix_w, l, l)
        if not last:
            y_attn_c = _attention(qkv_c.reshape(b, n_ctx, QKV_W), qkv_c.reshape(b, n_ctx, QKV_W),
                                  attn_sink[layer], False)
            xc2d = _mixer(rest_c, y_attn_c.reshape(b * n_ctx, ATTN_W), xc2d, m_ctx[2], *mix_w,
                          b * n_ctx, n_ctx)

        wq = peer_wq[layer].astype(BF16)
        subkeys = peer_subkeys[layer].reshape(2 * PEER_HEADS, PEER_KEYS, PEER_HALF).astype(BF16)
        uv_table = _pack_expert_tables(peer_u[layer], peer_v[layer])
        x2d = _peer_layer(x2d, g2, m_lat[4], m_lat[3], m_lat[5], l, wq, subkeys, uv_table, b)
        if not last:
            xc2d = _peer_layer(xc2d, g2, m_ctx[4], m_ctx[3], m_ctx[5], b * n_ctx, wq, subkeys, uv_table, 1)

    return _final_norm(x2d, g_final.reshape(1, d)).reshape(b, l, d)
```

```python
import functools

import jax
import jax.numpy as jnp
from jax import lax
from jax.experimental import pallas as pl
from jax.experimental.pallas import tpu as pltpu

F32 = jnp.float32
BF16 = jnp.bfloat16

D_MODEL = 1024
EPS = 1e-6
N_MOD = 6
GRID_W = 64

N_HEADS = 8
N_KV_HEADS = 2
GQA_GROUP = N_HEADS // N_KV_HEADS
HEAD_DIM = 64
WINDOW = 128
ATTN_BLOCK = 128
ATTN_W = N_HEADS * HEAD_DIM
KV_W = N_KV_HEADS * HEAD_DIM
QKV_W = ATTN_W + 2 * KV_W
ROPE_BASE = 10000.0
ROPE_FREQS = HEAD_DIM // 4

POOL_SIZES = (2, 4, 8, 16)
POOL_GROUP_W = 64
POOL_W = len(POOL_SIZES) * POOL_GROUP_W
POOL_HALO = 8

SGU_CHUNK = 128
SGU_GROUPS = 4
SGU_W = 256
SGU_GROUP_W = SGU_W // SGU_GROUPS

N_BRANCH = 3
REST_W = POOL_W + 2 * SGU_W + N_BRANCH * D_MODEL
IN_W = QKV_W + REST_W
BR_W = ATTN_W + POOL_W + SGU_W

PEER_HEADS = 8
PEER_KEYS = 128
PEER_HALF = 128
PEER_TOPK = 16
PEER_SLOTS = PEER_HEADS * PEER_TOPK
PEER_QW = PEER_HEADS * 2 * PEER_HALF

LANES = 128
SUBLANES = 8
VMEM_LIMIT = 48 * 1024 * 1024

TOKEN_TILE = 256
ROUTE_TILE = 128
APPLY_TILE = 256
GATHER_SLOTS = 8
TOKEN_GROUP = 8
ISSUE_PER_GROUP = 3
ROUTE_INTERLEAVE = 2
PAIR_INTERLEAVE = 4
_NO_ROW = 1e9


def _cparams(sem, **kw):
    return pltpu.CompilerParams(dimension_semantics=sem, vmem_limit_bytes=VMEM_LIMIT, **kw)


def _dot(a, b):
    return jnp.dot(a, b, preferred_element_type=F32)


def _dot_nt(a, b):
    return lax.dot_general(a, b, (((1,), (1,)), ((), ())), preferred_element_type=F32)


def _split_bf16(a):
    hi = a.astype(BF16)
    lo = (a - hi.astype(F32)).astype(BF16)
    return hi, lo


def _adaln_kernel(c_ref, w_ref, b_ref, o_ref):
    c = c_ref[...]
    s = c * (1.0 / (1.0 + jnp.exp(-c)))
    s_hi, s_lo = _split_bf16(s)
    w_hi, w_lo = _split_bf16(w_ref[...])
    acc = _dot(s_hi, w_hi) + (_dot(s_hi, w_lo) + _dot(s_lo, w_hi))
    o_ref[...] = acc + b_ref[...]


def _adaln(cond, w_mod, b_mod):
    depth = w_mod.shape[0]
    r = cond.shape[0]
    n = w_mod.shape[2]
    tn = D_MODEL
    return pl.pallas_call(
        _adaln_kernel,
        out_shape=jax.ShapeDtypeStruct((depth, r, n), F32),
        grid=(depth, n // tn),
        in_specs=[
            pl.BlockSpec((r, D_MODEL), lambda l, j: (0, 0)),
            pl.BlockSpec((None, D_MODEL, tn), lambda l, j: (l, 0, j)),
            pl.BlockSpec((None, 1, tn), lambda l, j: (l, 0, j)),
        ],
        out_specs=pl.BlockSpec((None, r, tn), lambda l, j: (l, 0, j)),
        compiler_params=_cparams(("arbitrary", "arbitrary")),
        name="adaln",
    )(cond, w_mod, b_mod.reshape(depth, 1, n))


def _modulated_norm(x, g, sc, sh):
    y = x * lax.rsqrt(jnp.mean(x * x, axis=-1, keepdims=True) + EPS)
    return (y * g) * (1.0 + sc) + sh


def _rope_tile(x, cos, sin_signed):
    lane = lax.broadcasted_iota(jnp.int32, x.shape, 1)
    first_half = (lane % (2 * ROPE_FREQS)) < ROPE_FREQS
    partner = jnp.where(first_half, pltpu.roll(x, LANES - ROPE_FREQS, 1), pltpu.roll(x, ROPE_FREQS, 1))
    return x * cos + partner * sin_signed


def _in_proj_kernel(x_ref, g_ref, sc_ref, sh_ref, w_ref, *rest, rope):
    if rope:
        cos_ref, sin_ref, qkv_ref, rest_ref = rest
    else:
        qkv_ref, rest_ref = rest
    hb = _modulated_norm(x_ref[...], g_ref[...], sc_ref[...], sh_ref[...]).astype(BF16)
    for c2 in range(QKV_W // (2 * LANES)):
        acc2 = _dot(hb, w_ref[:, c2 * 2 * LANES:(c2 + 1) * 2 * LANES])
        for half in range(2):
            c = 2 * c2 + half
            acc = acc2[:, half * LANES:(half + 1) * LANES]
            if rope and c * LANES < ATTN_W + KV_W:
                acc = _rope_tile(acc, cos_ref[...], sin_ref[...])
            qkv_ref[:, c * LANES:(c + 1) * LANES] = acc
    chunk = 768
    for c in range(REST_W // chunk):
        rest_ref[:, c * chunk:(c + 1) * chunk] = _dot(hb, w_ref[:, QKV_W + c * chunk:QKV_W + (c + 1) * chunk])


def _in_proj(x2d, g, sc, sh, w_bf16, rows_per_mod, rope_tables, seq_len):
    t = x2d.shape[0]
    tm = min(TOKEN_TILE, t)
    tiles_per_mod = rows_per_mod // tm
    rope = rope_tables is not None
    in_specs = [
        pl.BlockSpec((tm, D_MODEL), lambda i: (i, 0)),
        pl.BlockSpec((1, D_MODEL), lambda i: (0, 0)),
        pl.BlockSpec((None, 1, D_MODEL), lambda i: (i // tiles_per_mod, 0, 0)),
        pl.BlockSpec((None, 1, D_MODEL), lambda i: (i // tiles_per_mod, 0, 0)),
        pl.BlockSpec((D_MODEL, IN_W), lambda i: (0, 0)),
    ]
    args = [x2d, g, sc, sh, w_bf16]
    if rope:
        tiles_per_seq = seq_len // tm
        in_specs += [pl.BlockSpec((tm, LANES), lambda i: (i % tiles_per_seq, 0))] * 2
        args += list(rope_tables)
    return pl.pallas_call(
        functools.partial(_in_proj_kernel, rope=rope),
        out_shape=(jax.ShapeDtypeStruct((t, QKV_W), F32), jax.ShapeDtypeStruct((t, REST_W), F32)),
        grid=(t // tm,),
        in_specs=in_specs,
        out_specs=(pl.BlockSpec((tm, QKV_W), lambda i: (i, 0)), pl.BlockSpec((tm, REST_W), lambda i: (i, 0))),
        compiler_params=_cparams(("parallel",)),
        name="in_proj",
    )(*args)


def _peer_query_kernel(x_ref, g_ref, sc_ref, sh_ref, w_ref, q_ref, h_ref):
    h = _modulated_norm(x_ref[...], g_ref[...], sc_ref[...], sh_ref[...])
    h_ref[...] = h
    hb = h.astype(BF16)
    for head in range(PEER_HEADS):
        acc = _dot(hb, w_ref[:, head * 2 * PEER_HALF:(head + 1) * 2 * PEER_HALF])
        q_ref[2 * head] = acc[:, :PEER_HALF]
        q_ref[2 * head + 1] = acc[:, PEER_HALF:]


def _peer_query(x2d, g, sc, sh, wq_bf16, rows_per_mod):
    t = x2d.shape[0]
    tm = min(TOKEN_TILE, t)
    tiles_per_mod = rows_per_mod // tm
    return pl.pallas_call(
        _peer_query_kernel,
        out_shape=(jax.ShapeDtypeStruct((2 * PEER_HEADS, t, PEER_HALF), F32),
                   jax.ShapeDtypeStruct((t, D_MODEL), F32)),
        grid=(t // tm,),
        in_specs=[
            pl.BlockSpec((tm, D_MODEL), lambda i: (i, 0)),
            pl.BlockSpec((1, D_MODEL), lambda i: (0, 0)),
            pl.BlockSpec((None, 1, D_MODEL), lambda i: (i // tiles_per_mod, 0, 0)),
            pl.BlockSpec((None, 1, D_MODEL), lambda i: (i // tiles_per_mod, 0, 0)),
            pl.BlockSpec((D_MODEL, PEER_QW), lambda i: (0, 0)),
        ],
        out_specs=(pl.BlockSpec((2 * PEER_HEADS, tm, PEER_HALF), lambda i: (0, i, 0)),
                   pl.BlockSpec((tm, D_MODEL), lambda i: (i, 0))),
        compiler_params=_cparams(("parallel",)),
        name="peer_query",
    )(x2d, g, sc, sh, wq_bf16)


def _attn_kernel(sink_ref, q_ref, *rest, window, n_blocks):
    if window:
        kvp_ref, kvc_ref, kvn_ref, ctx_ref, o_ref = rest
    else:
        ctx_ref, o_ref = rest
    n = pl.program_id(1)
    q = q_ref[...] * (HEAD_DIM ** -0.5)
    sources = []
    if window:
        qi = lax.broadcasted_iota(jnp.int32, (GQA_GROUP * ATTN_BLOCK, ATTN_BLOCK), 0) % ATTN_BLOCK
        kj = lax.broadcasted_iota(jnp.int32, (GQA_GROUP * ATTN_BLOCK, ATTN_BLOCK), 1)
        sources.append((kvp_ref[...], (kj >= qi) & (n > 0)))
        sources.append((kvc_ref[...], None))
        sources.append((kvn_ref[...], (kj <= qi) & (n < n_blocks - 1)))
    for c in range(ctx_ref.shape[0] // ATTN_BLOCK):
        sources.append((ctx_ref[c * ATTN_BLOCK:(c + 1) * ATTN_BLOCK, :], None))
    kb = [[kv[:, g * HEAD_DIM:(g + 1) * HEAD_DIM].astype(BF16) for g in range(N_KV_HEADS)] for kv, _ in sources]
    ones = jnp.ones((ATTN_BLOCK, HEAD_DIM), BF16)
    vb = [[jnp.concatenate([kv[:, KV_W + g * HEAD_DIM:KV_W + (g + 1) * HEAD_DIM].astype(BF16), ones], axis=1)
           for g in range(N_KV_HEADS)] for kv, _ in sources]
    groups = [range(g * GQA_GROUP, (g + 1) * GQA_GROUP) for g in range(N_KV_HEADS)]
    sinks = [jnp.concatenate([jnp.full((ATTN_BLOCK, 1), sink_ref[h], F32) for h in heads], axis=0)
             for heads in groups]
    scores = []
    for g, heads in enumerate(groups):
        qg = jnp.concatenate([q[:, h * HEAD_DIM:(h + 1) * HEAD_DIM] for h in heads], axis=0).astype(BF16)
        sg = []
        for si, (_, valid) in enumerate(sources):
            s = _dot_nt(qg, kb[si][g])
            sg.append(s if valid is None else jnp.where(valid, s, -jnp.inf))
        scores.append(sg)
    maxes = [jnp.maximum(jnp.max(functools.reduce(jnp.maximum, scores[g]), axis=-1, keepdims=True), sinks[g])
             for g in range(N_KV_HEADS)]
    outs = []
    for g in range(N_KV_HEADS):
        m = maxes[g]
        acc = jnp.zeros((GQA_GROUP * ATTN_BLOCK, 2 * HEAD_DIM), F32)
        for si, s in enumerate(scores[g]):
            acc = acc + _dot(jnp.exp(s - m).astype(BF16), vb[si][g])
        og = acc[:, :HEAD_DIM] / (acc[:, HEAD_DIM:] + jnp.exp(sinks[g] - m))
        outs += [og[i * ATTN_BLOCK:(i + 1) * ATTN_BLOCK, :] for i in range(GQA_GROUP)]
    o_ref[...] = jnp.concatenate(outs, axis=-1)


def _attention(qkv, kv_ctx, sink, window):
    b, l, _ = qkv.shape
    n_ctx = kv_ctx.shape[1]
    nb = l // ATTN_BLOCK
    kv_col = ATTN_W // (2 * KV_W)
    in_specs = [
        pl.BlockSpec(memory_space=pltpu.SMEM),
        pl.BlockSpec((None, ATTN_BLOCK, ATTN_W), lambda bi, n: (bi, n, 0)),
    ]
    args = [sink, qkv]
    if window:
        in_specs += [
            pl.BlockSpec((None, ATTN_BLOCK, 2 * KV_W), lambda bi, n: (bi, jnp.maximum(n - 1, 0), kv_col)),
            pl.BlockSpec((None, ATTN_BLOCK, 2 * KV_W), lambda bi, n: (bi, n, kv_col)),
            pl.BlockSpec((None, ATTN_BLOCK, 2 * KV_W), lambda bi, n: (bi, jnp.minimum(n + 1, nb - 1), kv_col)),
        ]
        args += [qkv, qkv, qkv]
    in_specs.append(pl.BlockSpec((None, n_ctx, 2 * KV_W), lambda bi, n: (bi, 0, kv_col)))
    args.append(kv_ctx)
    return pl.pallas_call(
        functools.partial(_attn_kernel, window=window, n_blocks=nb),
        out_shape=jax.ShapeDtypeStruct((b, l, ATTN_W), F32),
        grid=(b, nb),
        in_specs=in_specs,
        out_specs=pl.BlockSpec((None, ATTN_BLOCK, ATTN_W), lambda bi, n: (bi, n, 0)),
        compiler_params=_cparams(("parallel", "parallel")),
        name="attention_window" if window else "attention_context",
    )(*args)


def _gelu(x):
    return jax.nn.gelu(x, approximate=True)


def _sigmoid(x):
    return 1.0 / (1.0 + jnp.exp(-x))


def _mixer_kernel(rest_ref, hprev_ref, hnext_ref, ya_ref, x_ref, gt_ref, poolw_ref, pscale_ref, sguw_ref,
                  sgub_ref, wbr_ref, wout_ref, o_ref, ext_ref, *, tm, tiles_per_seq, seq_len):
    ti = pl.program_id(0) % tiles_per_seq
    z = rest_ref[:, 0:POOL_W]
    ext_ref[0:POOL_HALO, :] = jnp.where(ti > 0, hprev_ref[...], 0.0)
    ext_ref[POOL_HALO:POOL_HALO + tm, :] = z
    ext_ref[POOL_HALO + tm:2 * POOL_HALO + tm, :] = jnp.where(ti < tiles_per_seq - 1, hnext_ref[...], 0.0)
    pos = ti * tm + lax.broadcasted_iota(jnp.int32, (tm, LANES), 0)
    lane = lax.broadcasted_iota(jnp.int32, (tm, LANES), 1)

    def count(size):
        hi = jnp.minimum(pos + size // 2, seq_len)
        lo = jnp.maximum(pos - size // 2, 0)
        return (hi - lo).astype(F32)

    diffs = []
    for lt in range(POOL_W // LANES):
        cols = slice(lt * LANES, (lt + 1) * LANES)

        def shifted(d, cols=cols):
            return ext_ref[POOL_HALO + d:POOL_HALO + d + tm, cols]

        small, large = POOL_SIZES[2 * lt], POOL_SIZES[2 * lt + 1]
        s = shifted(-1) + shifted(0)
        width = 2
        sums = {}
        while width <= large:
            sums[width] = s
            half = width // 2
            if 2 * width <= large:
                for d in range(half, width):
                    s = s + shifted(-d - 1) + shifted(d)
            width *= 2
        mean = jnp.where(lane < POOL_GROUP_W, sums[small] / count(small), sums[large] / count(large))
        diffs.append(mean - z[:, cols])
    d = jnp.concatenate(diffs, axis=-1).astype(BF16)
    y_pool = _dot(d, poolw_ref[...]) * pscale_ref[...]

    u = _gelu(rest_ref[:, POOL_W:POOL_W + SGU_W])
    v = _gelu(rest_ref[:, POOL_W + SGU_W:POOL_W + 2 * SGU_W])
    vn = (v * lax.rsqrt(jnp.mean(v * v, axis=-1, keepdims=True) + EPS)).astype(BF16)
    group = lax.broadcasted_iota(jnp.int32, (SGU_CHUNK, SGU_W), 1) // SGU_GROUP_W
    mixed = []
    for c in range(tm // SGU_CHUNK):
        vc = vn[c * SGU_CHUNK:(c + 1) * SGU_CHUNK, :]
        mc = sgub_ref[...]
        for hg in range(SGU_GROUPS):
            mc = mc + jnp.where(group == hg, _dot(sguw_ref[hg], vc), 0.0)
        mixed.append(mc)
    y_sgu = u * jnp.concatenate(mixed, axis=0)

    g_off = POOL_W + 2 * SGU_W
    merged = _sigmoid(rest_ref[:, g_off:g_off + D_MODEL]) * _dot(ya_ref[...].astype(BF16), wbr_ref[0:ATTN_W, :])
    merged = merged + _sigmoid(rest_ref[:, g_off + D_MODEL:g_off + 2 * D_MODEL]) * _dot(
        y_pool.astype(BF16), wbr_ref[ATTN_W:ATTN_W + POOL_W, :])
    merged = merged + _sigmoid(rest_ref[:, g_off + 2 * D_MODEL:g_off + 3 * D_MODEL]) * _dot(
        y_sgu.astype(BF16), wbr_ref[ATTN_W + POOL_W:BR_W, :])
    o_ref[...] = x_ref[...] + gt_ref[...] * _dot(merged.astype(BF16), wout_ref[...])


def _mixer(rest, y_attn, x2d, gt, poolw_bd, pscale, sguw, sgub_full, wbr, wout, rows_per_mod, seq_len):
    t = x2d.shape[0]
    tm = min(TOKEN_TILE, seq_len)
    tiles_per_seq = seq_len // tm
    tiles_per_mod = rows_per_mod // tm
    halo_blocks = tm // POOL_HALO
    n_halo = t // POOL_HALO
    kern = functools.partial(_mixer_kernel, tm=tm, tiles_per_seq=tiles_per_seq, seq_len=seq_len)
    return pl.pallas_call(
        kern,
        out_shape=jax.ShapeDtypeStruct((t, D_MODEL), F32),
        grid=(t // tm,),
        in_specs=[
            pl.BlockSpec((tm, REST_W), lambda i: (i, 0)),
            pl.BlockSpec((POOL_HALO, POOL_W), lambda i: (jnp.maximum(i * halo_blocks - 1, 0), 0)),
            pl.BlockSpec((POOL_HALO, POOL_W), lambda i: (jnp.minimum((i + 1) * halo_blocks, n_halo - 1), 0)),
            pl.BlockSpec((tm, ATTN_W), lambda i: (i, 0)),
            pl.BlockSpec((tm, D_MODEL), lambda i: (i, 0)),
            pl.BlockSpec((None, 1, D_MODEL), lambda i: (i // tiles_per_mod, 0, 0)),
            pl.BlockSpec((POOL_W, POOL_W), lambda i: (0, 0)),
            pl.BlockSpec((1, POOL_W), lambda i: (0, 0)),
            pl.BlockSpec((SGU_GROUPS, SGU_CHUNK, SGU_CHUNK), lambda i: (0, 0, 0)),
            pl.BlockSpec((SGU_CHUNK, SGU_W), lambda i: (0, 0)),
            pl.BlockSpec((BR_W, D_MODEL), lambda i: (0, 0)),
            pl.BlockSpec((D_MODEL, D_MODEL), lambda i: (0, 0)),
        ],
        out_specs=pl.BlockSpec((tm, D_MODEL), lambda i: (i, 0)),
        scratch_shapes=[pltpu.VMEM((tm + 2 * POOL_HALO, POOL_W), F32)],
        compiler_params=_cparams(("parallel",)),
        name="mixer",
    )(rest, rest, rest, y_attn, x2d, gt, poolw_bd, pscale, sguw, sgub_full, wbr, wout)


def _top16_rows(problems, emit):
    scores = [p[0] for p in problems]
    for r in range(PEER_TOPK):
        for i, (_, ids) in enumerate(problems):
            s = scores[i]
            m = jnp.max(s, axis=0, keepdims=True)
            idx = jnp.min(jnp.where(s == m, ids, _NO_ROW), axis=0, keepdims=True)
            hit = ids == idx
            emit(i, r, m, hit, idx)
            scores[i] = jnp.where(hit, -jnp.inf, s)


def _pair_candidates(sv0, sv1, si0, si1, sub):
    subf = sub.astype(F32)
    sums, experts, flat = [], [], []
    for a in range(SUBLANES):
        n_b = PEER_TOPK // (a + 1)
        for b0 in range(0, n_b, SUBLANES):
            s = sv0[a:a + 1, :] + sv1[b0:b0 + SUBLANES, :]
            if n_b - b0 < SUBLANES:
                s = jnp.where(sub < n_b - b0, s, -jnp.inf)
            sums.append(s)
            experts.append(si0[a:a + 1, :] * PEER_KEYS + si1[b0:b0 + SUBLANES, :])
            flat.append(subf + float(a * PEER_TOPK + b0))
    assert PEER_TOPK // (SUBLANES + 1) == 1
    sums.append(sv0[SUBLANES:, :] + sv1[0:1, :])
    experts.append(si0[SUBLANES:, :] * PEER_KEYS + si1[0:1, :])
    flat.append((subf + float(SUBLANES)) * float(PEER_TOPK))
    return jnp.concatenate(sums, axis=0), jnp.concatenate(experts, axis=0), jnp.concatenate(flat, axis=0)


def _route_sub_keys(q_ref, sk_ref, sv_ref, si_ref, tm):
    key_id = lax.broadcasted_iota(jnp.int32, (PEER_KEYS, tm), 0).astype(F32)

    def sub_keys(it, carry):
        hps = [it * ROUTE_INTERLEAVE + u for u in range(ROUTE_INTERLEAVE)]
        problems = [(_dot_nt(sk_ref[hp], q_ref[hp].astype(BF16)), key_id) for hp in hps]

        def emit(i, r, m, hit, idx):
            sv_ref[hps[i], r:r + 1, :] = m
            si_ref[hps[i], r:r + 1, :] = idx.astype(jnp.int32)

        _top16_rows(problems, emit)
        return carry

    lax.fori_loop(0, 2 * PEER_HEADS // ROUTE_INTERLEAVE, sub_keys, 0)


def _route_pairs(sv_ref, si_ref, e_ref, g_ref, ts_ref, te_ref, tm):
    sub = lax.broadcasted_iota(jnp.int32, (SUBLANES, tm), 0)

    def heads(it, carry):
        hs = [it * PAIR_INTERLEAVE + u for u in range(PAIR_INTERLEAVE)]
        problems, experts = [], []
        for h in hs:
            cs, ce, cid = _pair_candidates(sv_ref[2 * h], sv_ref[2 * h + 1], si_ref[2 * h], si_ref[2 * h + 1], sub)
            problems.append((cs, cid))
            experts.append(ce)

        def emit(i, r, m, hit, idx):
            ts_ref[i, r:r + 1, :] = m
            te_ref[i, r:r + 1, :] = jnp.max(jnp.where(hit, experts[i], -1), axis=0, keepdims=True)

        _top16_rows(problems, emit)
        for i, h in enumerate(hs):
            row0 = pl.multiple_of(h * PEER_TOPK, PEER_TOPK)
            e_ref[pl.ds(row0, PEER_TOPK), :] = te_ref[i]
            ts = ts_ref[i]
            ex = jnp.exp(ts - jnp.max(ts, axis=0, keepdims=True))
            g_ref[pl.ds(row0, PEER_TOPK), :] = ex / jnp.sum(ex, axis=0, keepdims=True)
        return carry

    lax.fori_loop(0, PEER_HEADS // PAIR_INTERLEAVE, heads, 0)


def _peer_route_kernel(q_ref, sk_ref, e_ref, g_ref, sv_ref, si_ref, ts_ref, te_ref, *, tm):
    _route_sub_keys(q_ref, sk_ref, sv_ref, si_ref, tm)
    _route_pairs(sv_ref, si_ref, e_ref, g_ref, ts_ref, te_ref, tm)


def _peer_pairs_kernel(sv_ref, si_ref, e_ref, g_ref, ts_ref, te_ref, *, tm):
    _route_pairs(sv_ref, si_ref, e_ref, g_ref, ts_ref, te_ref, tm)


def _route_out(nblk, tm):
    shape = (jax.ShapeDtypeStruct((nblk, PEER_SLOTS, tm), jnp.int32), jax.ShapeDtypeStruct((nblk, PEER_SLOTS, tm), F32))
    specs = (pl.BlockSpec((None, PEER_SLOTS, tm), lambda i: (i, 0, 0)),
             pl.BlockSpec((None, PEER_SLOTS, tm), lambda i: (i, 0, 0)))
    scratch = [pltpu.VMEM((PAIR_INTERLEAVE, PEER_TOPK, tm), F32), pltpu.VMEM((PAIR_INTERLEAVE, PEER_TOPK, tm), jnp.int32)]
    return shape, specs, scratch


def _peer_route(q3, subkeys_bf16, n_tokens):
    tm = ROUTE_TILE
    nblk = n_tokens // tm
    out_shape, out_specs, out_scratch = _route_out(nblk, tm)
    return pl.pallas_call(
        functools.partial(_peer_route_kernel, tm=tm),
        out_shape=out_shape,
        grid=(nblk,),
        in_specs=[
            pl.BlockSpec((2 * PEER_HEADS, tm, PEER_HALF), lambda i: (0, i, 0)),
            pl.BlockSpec((2 * PEER_HEADS, PEER_KEYS, PEER_HALF), lambda i: (0, 0, 0)),
        ],
        out_specs=out_specs,
        scratch_shapes=[
            pltpu.VMEM((2 * PEER_HEADS, PEER_TOPK, tm), F32),
            pltpu.VMEM((2 * PEER_HEADS, PEER_TOPK, tm), jnp.int32),
        ] + out_scratch,
        compiler_params=_cparams(("parallel",)),
        name="peer_route",
    )(q3, subkeys_bf16)


def _peer_pairs(sv, si):
    nblk, _, _, tm = sv.shape
    out_shape, out_specs, out_scratch = _route_out(nblk, tm)
    list_spec = pl.BlockSpec((None, 2 * PEER_HEADS, PEER_TOPK, tm), lambda i: (i, 0, 0, 0))
    return pl.pallas_call(
        functools.partial(_peer_pairs_kernel, tm=tm),
        out_shape=out_shape,
        grid=(nblk,),
        in_specs=[list_spec, list_spec],
        out_specs=out_specs,
        scratch_shapes=out_scratch,
        compiler_params=_cparams(("parallel",)),
        name="peer_pairs",
    )(sv, si)


def _peer_apply_kernel(*refs, tb, side):
    if side:
        (idx_ref, gate_ref, h_ref, x_ref, gt_ref, uv_hbm, q_ref, sk_ref, o_ref, sv_ref, si_ref,
         buf_ref, sem_ref, part_ref, acc_ref, s_ref) = refs
    else:
        idx_ref, gate_ref, h_ref, x_ref, gt_ref, uv_hbm, o_ref, buf_ref, sem_ref, part_ref, acc_ref = refs
    lookahead = GATHER_SLOTS - 1
    n_groups = tb // TOKEN_GROUP
    n_tiles = D_MODEL // LANES
    n_slot_groups = PEER_SLOTS // SUBLANES

    def start_rows(tok, slot, k0, n):
        for k in range(k0, k0 + n):
            pltpu.make_async_copy(uv_hbm.at[idx_ref[tok, k]], buf_ref.at[slot, k],
                                  sem_ref.at[slot]).start(priority=k % 2)

    def wait_gather(slot):
        pltpu.make_async_copy(uv_hbm.at[pl.ds(0, PEER_SLOTS)], buf_ref.at[slot], sem_ref.at[slot]).wait()

    for tok in range(lookahead):
        start_rows(tok, tok % GATHER_SLOTS, 0, PEER_SLOTS)

    tok_lane = lax.broadcasted_iota(jnp.int32, (PEER_SLOTS, ROUTE_TILE), 1)
    sub = lax.broadcasted_iota(jnp.int32, (SUBLANES, LANES), 0)

    def fold(x0, x1, d):
        t0 = x0 + pltpu.roll(x0, SUBLANES - d, 0)
        t1 = x1 + pltpu.roll(x1, d, 0)
        return jnp.where((sub & d) == 0, t0, t1)

    key_id = lax.broadcasted_iota(jnp.int32, (PEER_KEYS, ROUTE_TILE), 0).astype(F32)
    groups_per_list_pair = PEER_TOPK // TOKEN_GROUP
    groups_per_tile = groups_per_list_pair * PEER_HEADS
    if side:
        assert n_groups * TOKEN_GROUP == tb and tb % ROUTE_TILE == 0 and groups_per_tile * TOKEN_GROUP == ROUTE_TILE

    def group(gi, is_last):
        base = gi * TOKEN_GROUP if is_last else pl.multiple_of(gi * TOKEN_GROUP, TOKEN_GROUP)
        h8 = h_ref[pl.ds(base, TOKEN_GROUP), :]
        rows = []
        if side:
            tile = gi // groups_per_tile
            hp0 = 2 * ((gi // groups_per_list_pair) % PEER_HEADS)
            r0 = (gi % groups_per_list_pair) * TOKEN_GROUP

            def load_scores():
                rows0 = tile * ROUTE_TILE if is_last else pl.multiple_of(tile * ROUTE_TILE, ROUTE_TILE)
                for pi in range(2):
                    s_ref[pi] = _dot_nt(sk_ref[hp0 + pi], q_ref[hp0 + pi, pl.ds(rows0, ROUTE_TILE), :].astype(BF16))

            if is_last:
                if r0 == 0:
                    load_scores()
            else:
                pl.when(r0 == 0)(load_scores)
        for j in range(TOKEN_GROUP):
            tok = base + j
            slot = j % GATHER_SLOTS
            prefetch = not is_last or j + lookahead < TOKEN_GROUP

            def start_ahead(k0, n, tok=tok, j=j, prefetch=prefetch):
                if prefetch:
                    start_rows(tok + lookahead, (j + lookahead) % GATHER_SLOTS, k0, n)

            wait_gather(slot)
            h_tile = jnp.concatenate([h8[j:j + 1, s * LANES:(s + 1) * LANES] for s in range(n_tiles)], axis=0)
            for g in range(n_slot_groups):
                p = [lax.bitcast_convert_type(buf_ref[slot, g * SUBLANES + c] & jnp.uint32(0xFFFF0000), F32) * h_tile
                     for c in range(SUBLANES)]
                y = [fold(p[c], p[c + 4], 4) for c in range(4)]
                z = [fold(y[c], y[c + 2], 2) for c in range(2)]
                part_ref[g * SUBLANES:(g + 1) * SUBLANES, :] = fold(z[0], z[1], 1)
                start_ahead(g * ISSUE_PER_GROUP, ISSUE_PER_GROUP)
            acc_ref[...] = jnp.zeros(acc_ref.shape, F32)
            start_ahead(n_slot_groups * ISSUE_PER_GROUP, PEER_SLOTS - 2 * n_slot_groups * ISSUE_PER_GROUP)
            if side:
                for pi in range(2):
                    sc = s_ref[pi]
                    m = jnp.max(sc, axis=0, keepdims=True)
                    idx = jnp.min(jnp.where(sc == m, key_id, _NO_ROW), axis=0, keepdims=True)
                    sv_ref[tile, hp0 + pi, r0 + j] = m
                    si_ref[tile, hp0 + pi, r0 + j] = idx.astype(jnp.int32)
                    s_ref[pi] = jnp.where(key_id == idx, -jnp.inf, sc)
            act = jnp.sum(part_ref[...], axis=1, keepdims=True)
            gates = gate_ref[tok // ROUTE_TILE]
            gate = jnp.sum(jnp.where(tok_lane == tok % ROUTE_TILE, gates, 0.0), axis=1, keepdims=True)
            w = gate * _gelu(act)
            for g in range(n_slot_groups):
                terms = [jnp.broadcast_to(w[k:k + 1, :], (SUBLANES, LANES))
                         * lax.bitcast_convert_type(buf_ref[slot, k] << 16, F32)
                         for k in range(g * SUBLANES, (g + 1) * SUBLANES)]
                s4 = [terms[i] + terms[i + 4] for i in range(4)]
                acc_ref[g % 2] = acc_ref[g % 2] + ((s4[0] + s4[1]) + (s4[2] + s4[3]))
                start_ahead(PEER_SLOTS - (n_slot_groups - g) * ISSUE_PER_GROUP, ISSUE_PER_GROUP)
            acc = acc_ref[0] + acc_ref[1]
            rows.append(jnp.concatenate([acc[s:s + 1, :] for s in range(n_tiles)], axis=1))
        out8 = jnp.concatenate(rows, axis=0)
        o_ref[pl.ds(base, TOKEN_GROUP), :] = x_ref[pl.ds(base, TOKEN_GROUP), :] + gt_ref[...] * out8

    def body(gi, carry):
        group(gi, False)
        return carry

    lax.fori_loop(0, n_groups - 1, body, 0)
    group(n_groups - 1, True)


def _peer_apply(experts_tok, gates, h2d, x2d, gt, uv_table, n_tokens, row0, rows_per_mod, side=None):
    tb = min(APPLY_TILE, n_tokens)
    tiles_per_mod = rows_per_mod // tb
    tile0 = row0 // tb
    in_specs = [
        pl.BlockSpec((tb, PEER_SLOTS), lambda i: (i, 0), memory_space=pltpu.SMEM),
        pl.BlockSpec((tb // ROUTE_TILE, PEER_SLOTS, ROUTE_TILE), lambda i: (i, 0, 0)),
        pl.BlockSpec((tb, D_MODEL), lambda i: (tile0 + i, 0)),
        pl.BlockSpec((tb, D_MODEL), lambda i: (tile0 + i, 0)),
        pl.BlockSpec((None, 1, D_MODEL), lambda i: ((tile0 + i) // tiles_per_mod, 0, 0)),
        pl.BlockSpec(memory_space=pl.ANY),
    ]
    args = [experts_tok, gates, h2d, x2d, gt, uv_table]
    out_shape = [jax.ShapeDtypeStruct(x2d.shape, F32)]
    out_specs = [pl.BlockSpec((tb, D_MODEL), lambda i: (tile0 + i, 0))]
    scratch = [
        pltpu.VMEM((GATHER_SLOTS, PEER_SLOTS, SUBLANES, LANES), jnp.uint32),
        pltpu.SemaphoreType.DMA((GATHER_SLOTS,)),
        pltpu.VMEM((PEER_SLOTS, LANES), F32),
        pltpu.VMEM((2, SUBLANES, LANES), F32),
    ]
    if side is not None:
        q3, subkeys, side_row0 = side
        side_tile0 = side_row0 // tb
        in_specs += [
            pl.BlockSpec((2 * PEER_HEADS, tb, PEER_HALF), lambda i: (0, side_tile0 + i, 0)),
            pl.BlockSpec((2 * PEER_HEADS, PEER_KEYS, PEER_HALF), lambda i: (0, 0, 0)),
        ]
        args += [q3, subkeys]
        list_shape = (n_tokens // ROUTE_TILE, 2 * PEER_HEADS, PEER_TOPK, 1, ROUTE_TILE)
        list_block = (tb // ROUTE_TILE, 2 * PEER_HEADS, PEER_TOPK, 1, ROUTE_TILE)
        out_shape += [jax.ShapeDtypeStruct(list_shape, F32), jax.ShapeDtypeStruct(list_shape, jnp.int32)]
        out_specs += [pl.BlockSpec(list_block, lambda i: (i, 0, 0, 0, 0))] * 2
        scratch.append(pltpu.VMEM((2, PEER_KEYS, ROUTE_TILE), F32))
    outs = pl.pallas_call(
        functools.partial(_peer_apply_kernel, tb=tb, side=side is not None),
        out_shape=tuple(out_shape),
        grid=(n_tokens // tb,),
        in_specs=in_specs,
        out_specs=tuple(out_specs),
        scratch_shapes=scratch,
        input_output_aliases={3: 0},
        compiler_params=_cparams(("arbitrary",), disable_bounds_checks=True),
        name="peer_apply_side" if side is not None else "peer_apply",
    )(*args)
    if side is None:
        return outs[0]
    sv, si = (a.reshape(a.shape[:3] + a.shape[4:]) for a in outs[1:])
    return outs[0], sv, si


def _final_norm_kernel(x_ref, g_ref, o_ref):
    x = x_ref[...]
    o_ref[...] = (x * lax.rsqrt(jnp.mean(x * x, axis=-1, keepdims=True) + EPS)) * g_ref[...]


def _final_norm(x2d, g):
    t = x2d.shape[0]
    tm = min(2 * TOKEN_TILE, t)
    return pl.pallas_call(
        _final_norm_kernel,
        out_shape=jax.ShapeDtypeStruct((t, D_MODEL), F32),
        grid=(t // tm,),
        in_specs=[pl.BlockSpec((tm, D_MODEL), lambda i: (i, 0)), pl.BlockSpec((1, D_MODEL), lambda i: (0, 0))],
        out_specs=pl.BlockSpec((tm, D_MODEL), lambda i: (i, 0)),
        compiler_params=_cparams(("parallel",)),
        name="final_norm",
    )(x2d, g)


def _rope_tables(length):
    rows = length // GRID_W
    row = jnp.repeat(jnp.arange(rows, dtype=F32), GRID_W)
    col = jnp.tile(jnp.arange(GRID_W, dtype=F32), rows)
    inv = ROPE_BASE ** (-jnp.arange(ROPE_FREQS, dtype=F32) / ROPE_FREQS)
    ang = jnp.stack([row[:, None] * inv, col[:, None] * inv], axis=1)
    cos, sin = jnp.cos(ang), jnp.sin(ang)
    cos_h = jnp.broadcast_to(cos[:, :, None, :], (length, 2, 2, ROPE_FREQS)).reshape(length, HEAD_DIM)
    sin_h = jnp.stack([-sin, sin], axis=2).reshape(length, HEAD_DIM)
    reps = LANES // HEAD_DIM
    return jnp.tile(cos_h, (1, reps)), jnp.tile(sin_h, (1, reps))


def _pack_expert_tables(u_tab, v_tab):
    def bits(a):
        return lax.bitcast_convert_type(a.astype(BF16), jnp.uint16).astype(jnp.uint32)

    return ((bits(u_tab) << 16) | bits(v_tab)).reshape(-1, D_MODEL // LANES, LANES)


def _experts_by_token(experts):
    return jnp.transpose(experts, (0, 2, 1)).reshape(-1, PEER_SLOTS)


def _peer_layer(x2d, g, sc, sh, gt, rows_per_mod, wq, subkeys, uv_table, n_chunks):
    t = x2d.shape[0]
    q3, h2d = _peer_query(x2d, g, sc, sh, wq, rows_per_mod)
    n = t // n_chunks
    experts, gates = _peer_route(q3, subkeys, n)
    for c in range(n_chunks):
        side = (q3, subkeys, (c + 1) * n) if c + 1 < n_chunks else None
        res = _peer_apply(_experts_by_token(experts), gates, h2d, x2d, gt, uv_table, n, c * n, rows_per_mod, side)
        if side is None:
            x2d = res
        else:
            x2d = res[0]
            experts, gates = _peer_pairs(res[1], res[2])
    return x2d


def kernel(x, c, ctx, c_ctx, w_mod, b_mod, g_mix, g_ffn, w_in, attn_sink, pool_w, pool_scale, sgu_w, sgu_b,
           w_br_attn, w_br_pool, w_br_sgu, w_out, peer_wq, peer_subkeys, peer_u, peer_v, g_final):
    b, l, d = x.shape
    n_ctx = ctx.shape[1]
    depth = w_mod.shape[0]
    assert d == D_MODEL and l % TOKEN_TILE == 0 and n_ctx % SGU_CHUNK == 0

    n_rows = -(-(b + 1) // SUBLANES) * SUBLANES
    cond = jnp.zeros((n_rows, d), F32).at[:b].set(c).at[b].set(c_ctx)
    mod = _adaln(cond, w_mod, b_mod).reshape(depth, n_rows, N_MOD, 1, d)

    rope = _rope_tables(l)
    x2d = x.reshape(b * l, d)
    xc2d = ctx.reshape(b * n_ctx, d)

    for layer in range(depth):
        last = layer == depth - 1
        m_lat = [mod[layer, :b, i] for i in range(N_MOD)]
        m_ctx = [mod[layer, b:b + 1, i] for i in range(N_MOD)]
        g1 = g_mix[layer].reshape(1, d)
        g2 = g_ffn[layer].reshape(1, d)

        w_in_b = w_in[layer].astype(BF16)
        qkv, rest = _in_proj(x2d, g1, m_lat[1], m_lat[0], w_in_b, l, rope, l)
        qkv_c, rest_c = _in_proj(xc2d, g1, m_ctx[1], m_ctx[0], w_in_b, b * n_ctx, None, n_ctx)

        y_attn = _attention(qkv.reshape(b, l, QKV_W), qkv_c.reshape(b, n_ctx, QKV_W), attn_sink[layer], True)

        poolw_bd = jnp.zeros((POOL_W, POOL_W), F32)
        for gi in range(len(POOL_SIZES)):
            sl = slice(gi * POOL_GROUP_W, (gi + 1) * POOL_GROUP_W)
            poolw_bd = poolw_bd.at[sl, sl].set(pool_w[layer, gi])
        poolw_bd = poolw_bd.astype(BF16)
        pscale = pool_scale[layer].reshape(1, POOL_W)
        sguw = sgu_w[layer].astype(BF16)
        sgub_full = jnp.repeat(sgu_b[layer].T, SGU_GROUP_W, axis=1)
        wbr = jnp.concatenate([w_br_attn[layer], w_br_pool[layer], w_br_sgu[layer]], axis=0).astype(BF16)
        wout = w_out[layer].astype(BF16)
        mix_w = (poolw_bd, pscale, sguw, sgub_full, wbr, wout)

        x2d = _mixer(rest, y_attn.reshape(b * l, ATTN_W), x2d, m_lat[2], *mix_w, l, l)
        if not last:
            y_attn_c = _attention(qkv_c.reshape(b, n_ctx, QKV_W), qkv_c.reshape(b, n_ctx, QKV_W),
                                  attn_sink[layer], False)
            xc2d = _mixer(rest_c, y_attn_c.reshape(b * n_ctx, ATTN_W), xc2d, m_ctx[2], *mix_w,
                          b * n_ctx, n_ctx)

        wq = peer_wq[layer].astype(BF16)
        subkeys = peer_subkeys[layer].reshape(2 * PEER_HEADS, PEER_KEYS, PEER_HALF).astype(BF16)
        uv_table = _pack_expert_tables(peer_u[layer], peer_v[layer])
        x2d = _peer_layer(x2d, g2, m_lat[4], m_lat[3], m_lat[5], l, wq, subkeys, uv_table, b)
        if not last:
            xc2d = _peer_layer(xc2d, g2, m_ctx[4], m_ctx[3], m_ctx[5], b * n_ctx, wq, subkeys, uv_table, 1)

    return _final_norm(x2d, g_final.reshape(1, d)).reshape(b, l, d)
```

```python
import functools

import jax
import jax.numpy as jnp
from jax import lax
from jax.experimental import pallas as pl
from jax.experimental.pallas import tpu as pltpu

F32 = jnp.float32
BF16 = jnp.bfloat16

D_MODEL = 1024
EPS = 1e-6
N_MOD = 6
GRID_W = 64

N_HEADS = 8
N_KV_HEADS = 2
GQA_GROUP = N_HEADS // N_KV_HEADS
HEAD_DIM = 64
WINDOW = 128
ATTN_BLOCK = 128
ATTN_W = N_HEADS * HEAD_DIM
KV_W = N_KV_HEADS * HEAD_DIM
QKV_W = ATTN_W + 2 * KV_W
ROPE_BASE = 10000.0
ROPE_FREQS = HEAD_DIM // 4

POOL_SIZES = (2, 4, 8, 16)
POOL_GROUP_W = 64
POOL_W = len(POOL_SIZES) * POOL_GROUP_W
POOL_HALO = 8

SGU_CHUNK = 128
SGU_GROUPS = 4
SGU_W = 256
SGU_GROUP_W = SGU_W // SGU_GROUPS

N_BRANCH = 3
REST_W = POOL_W + 2 * SGU_W + N_BRANCH * D_MODEL
IN_W = QKV_W + REST_W
BR_W = ATTN_W + POOL_W + SGU_W

PEER_HEADS = 8
PEER_KEYS = 128
PEER_HALF = 128
PEER_TOPK = 16
PEER_SLOTS = PEER_HEADS * PEER_TOPK
PEER_QW = PEER_HEADS * 2 * PEER_HALF

LANES = 128
SUBLANES = 8
VMEM_LIMIT = 48 * 1024 * 1024

TOKEN_TILE = 256
ROUTE_TILE = 128
APPLY_TILE = 256
GATHER_SLOTS = 8
TOKEN_GROUP = 8
ISSUE_PER_GROUP = 3
ROUTE_INTERLEAVE = 2
PAIR_INTERLEAVE = 4
_NO_ROW = 1e9


def _cparams(sem, **kw):
    return pltpu.CompilerParams(dimension_semantics=sem, vmem_limit_bytes=VMEM_LIMIT, **kw)


def _dot(a, b):
    return jnp.dot(a, b, preferred_element_type=F32)


def _dot_nt(a, b):
    return lax.dot_general(a, b, (((1,), (1,)), ((), ())), preferred_element_type=F32)


def _split_bf16(a):
    hi = a.astype(BF16)
    lo = (a - hi.astype(F32)).astype(BF16)
    return hi, lo


def _adaln_kernel(c_ref, w_ref, b_ref, o_ref):
    c = c_ref[...]
    s = c * (1.0 / (1.0 + jnp.exp(-c)))
    s_hi, s_lo = _split_bf16(s)
    w_hi, w_lo = _split_bf16(w_ref[...])
    acc = _dot(s_hi, w_hi) + (_dot(s_hi, w_lo) + _dot(s_lo, w_hi))
    o_ref[...] = acc + b_ref[...]


def _adaln(cond, w_mod, b_mod):
    depth = w_mod.shape[0]
    r = cond.shape[0]
    n = w_mod.shape[2]
    tn = D_MODEL
    return pl.pallas_call(
        _adaln_kernel,
        out_shape=jax.ShapeDtypeStruct((depth, r, n), F32),
        grid=(depth, n // tn),
        in_specs=[
            pl.BlockSpec((r, D_MODEL), lambda l, j: (0, 0)),
            pl.BlockSpec((None, D_MODEL, tn), lambda l, j: (l, 0, j)),
            pl.BlockSpec((None, 1, tn), lambda l, j: (l, 0, j)),
        ],
        out_specs=pl.BlockSpec((None, r, tn), lambda l, j: (l, 0, j)),
        compiler_params=_cparams(("arbitrary", "arbitrary")),
        name="adaln",
    )(cond, w_mod, b_mod.reshape(depth, 1, n))


def _modulated_norm(x, g, sc, sh):
    y = x * lax.rsqrt(jnp.mean(x * x, axis=-1, keepdims=True) + EPS)
    return (y * g) * (1.0 + sc) + sh


def _rope_tile(x, cos, sin_signed):
    lane = lax.broadcasted_iota(jnp.int32, x.shape, 1)
    first_half = (lane % (2 * ROPE_FREQS)) < ROPE_FREQS
    partner = jnp.where(first_half, pltpu.roll(x, LANES - ROPE_FREQS, 1), pltpu.roll(x, ROPE_FREQS, 1))
    return x * cos + partner * sin_signed


def _in_proj_kernel(x_ref, g_ref, sc_ref, sh_ref, w_ref, *rest, rope):
    if rope:
        cos_ref, sin_ref, qkv_ref, rest_ref = rest
    else:
        qkv_ref, rest_ref = rest
    hb = _modulated_norm(x_ref[...], g_ref[...], sc_ref[...], sh_ref[...]).astype(BF16)
    for c2 in range(QKV_W // (2 * LANES)):
        acc2 = _dot(hb, w_ref[:, c2 * 2 * LANES:(c2 + 1) * 2 * LANES])
        for half in range(2):
            c = 2 * c2 + half
            acc = acc2[:, half * LANES:(half + 1) * LANES]
            if rope and c * LANES < ATTN_W + KV_W:
                acc = _rope_tile(acc, cos_ref[...], sin_ref[...])
            qkv_ref[:, c * LANES:(c + 1) * LANES] = acc
    chunk = 768
    for c in range(REST_W // chunk):
        rest_ref[:, c * chunk:(c + 1) * chunk] = _dot(hb, w_ref[:, QKV_W + c * chunk:QKV_W + (c + 1) * chunk])


def _in_proj(x2d, g, sc, sh, w_bf16, rows_per_mod, rope_tables, seq_len):
    t = x2d.shape[0]
    tm = min(TOKEN_TILE, t)
    tiles_per_mod = rows_per_mod // tm
    rope = rope_tables is not None
    in_specs = [
        pl.BlockSpec((tm, D_MODEL), lambda i: (i, 0)),
        pl.BlockSpec((1, D_MODEL), lambda i: (0, 0)),
        pl.BlockSpec((None, 1, D_MODEL), lambda i: (i // tiles_per_mod, 0, 0)),
        pl.BlockSpec((None, 1, D_MODEL), lambda i: (i // tiles_per_mod, 0, 0)),
        pl.BlockSpec((D_MODEL, IN_W), lambda i: (0, 0)),
    ]
    args = [x2d, g, sc, sh, w_bf16]
    if rope:
        tiles_per_seq = seq_len // tm
        in_specs += [pl.BlockSpec((tm, LANES), lambda i: (i % tiles_per_seq, 0))] * 2
        args += list(rope_tables)
    return pl.pallas_call(
        functools.partial(_in_proj_kernel, rope=rope),
        out_shape=(jax.ShapeDtypeStruct((t, QKV_W), F32), jax.ShapeDtypeStruct((t, REST_W), F32)),
        grid=(t // tm,),
        in_specs=in_specs,
        out_specs=(pl.BlockSpec((tm, QKV_W), lambda i: (i, 0)), pl.BlockSpec((tm, REST_W), lambda i: (i, 0))),
        compiler_params=_cparams(("parallel",)),
        name="in_proj",
    )(*args)


def _peer_query_kernel(x_ref, g_ref, sc_ref, sh_ref, w_ref, q_ref, h_ref):
    h = _modulated_norm(x_ref[...], g_ref[...], sc_ref[...], sh_ref[...])
    h_ref[...] = h
    hb = h.astype(BF16)
    for head in range(PEER_HEADS):
        acc = _dot(hb, w_ref[:, head * 2 * PEER_HALF:(head + 1) * 2 * PEER_HALF])
        q_ref[2 * head] = acc[:, :PEER_HALF]
        q_ref[2 * head + 1] = acc[:, PEER_HALF:]


def _peer_query(x2d, g, sc, sh, wq_bf16, rows_per_mod):
    t = x2d.shape[0]
    tm = min(TOKEN_TILE, t)
    tiles_per_mod = rows_per_mod // tm
    return pl.pallas_call(
        _peer_query_kernel,
        out_shape=(jax.ShapeDtypeStruct((2 * PEER_HEADS, t, PEER_HALF), F32),
                   jax.ShapeDtypeStruct((t, D_MODEL), F32)),
        grid=(t // tm,),
        in_specs=[
            pl.BlockSpec((tm, D_MODEL), lambda i: (i, 0)),
            pl.BlockSpec((1, D_MODEL), lambda i: (0, 0)),
            pl.BlockSpec((None, 1, D_MODEL), lambda i: (i // tiles_per_mod, 0, 0)),
            pl.BlockSpec((None, 1, D_MODEL), lambda i: (i // tiles_per_mod, 0, 0)),
            pl.BlockSpec((D_MODEL, PEER_QW), lambda i: (0, 0)),
        ],
        out_specs=(pl.BlockSpec((2 * PEER_HEADS, tm, PEER_HALF), lambda i: (0, i, 0)),
                   pl.BlockSpec((tm, D_MODEL), lambda i: (i, 0))),
        compiler_params=_cparams(("parallel",)),
        name="peer_query",
    )(x2d, g, sc, sh, wq_bf16)


def _attn_kernel(sink_ref, q_ref, *rest, window, n_blocks):
    if window:
        kvp_ref, kvc_ref, kvn_ref, ctx_ref, o_ref = rest
    else:
        ctx_ref, o_ref = rest
    n = pl.program_id(1)
    q = q_ref[...] * (HEAD_DIM ** -0.5)
    sources = []
    if window:
        qi = lax.broadcasted_iota(jnp.int32, (GQA_GROUP * ATTN_BLOCK, ATTN_BLOCK), 0) % ATTN_BLOCK
        kj = lax.broadcasted_iota(jnp.int32, (GQA_GROUP * ATTN_BLOCK, ATTN_BLOCK), 1)
        sources.append((kvp_ref[...], (kj >= qi) & (n > 0)))
        sources.append((kvc_ref[...], None))
        sources.append((kvn_ref[...], (kj <= qi) & (n < n_blocks - 1)))
    for c in range(ctx_ref.shape[0] // ATTN_BLOCK):
        sources.append((ctx_ref[c * ATTN_BLOCK:(c + 1) * ATTN_BLOCK, :], None))
    kb = [[kv[:, g * HEAD_DIM:(g + 1) * HEAD_DIM].astype(BF16) for g in range(N_KV_HEADS)] for kv, _ in sources]
    ones = jnp.ones((ATTN_BLOCK, HEAD_DIM), BF16)
    vb = [[jnp.concatenate([kv[:, KV_W + g * HEAD_DIM:KV_W + (g + 1) * HEAD_DIM].astype(BF16), ones], axis=1)
           for g in range(N_KV_HEADS)] for kv, _ in sources]
    groups = [range(g * GQA_GROUP, (g + 1) * GQA_GROUP) for g in range(N_KV_HEADS)]
    sinks = [jnp.concatenate([jnp.full((ATTN_BLOCK, 1), sink_ref[h], F32) for h in heads], axis=0)
             for heads in groups]
    scores = []
    for g, heads in enumerate(groups):
        qg = jnp.concatenate([q[:, h * HEAD_DIM:(h + 1) * HEAD_DIM] for h in heads], axis=0).astype(BF16)
        sg = []
        for si, (_, valid) in enumerate(sources):
            s = _dot_nt(qg, kb[si][g])
            sg.append(s if valid is None else jnp.where(valid, s, -jnp.inf))
        scores.append(sg)
    maxes = [jnp.maximum(jnp.max(functools.reduce(jnp.maximum, scores[g]), axis=-1, keepdims=True), sinks[g])
             for g in range(N_KV_HEADS)]
    outs = []
    for g in range(N_KV_HEADS):
        m = maxes[g]
        acc = jnp.zeros((GQA_GROUP * ATTN_BLOCK, 2 * HEAD_DIM), F32)
        for si, s in enumerate(scores[g]):
            acc = acc + _dot(jnp.exp(s - m).astype(BF16), vb[si][g])
        og = acc[:, :HEAD_DIM] / (acc[:, HEAD_DIM:] + jnp.exp(sinks[g] - m))
        outs += [og[i * ATTN_BLOCK:(i + 1) * ATTN_BLOCK, :] for i in range(GQA_GROUP)]
    o_ref[...] = jnp.concatenate(outs, axis=-1)


def _attention(qkv, kv_ctx, sink, window):
    b, l, _ = qkv.shape
    n_ctx = kv_ctx.shape[1]
    nb = l // ATTN_BLOCK
    kv_col = ATTN_W // (2 * KV_W)
    in_specs = [
        pl.BlockSpec(memory_space=pltpu.SMEM),
        pl.BlockSpec((None, ATTN_BLOCK, ATTN_W), lambda bi, n: (bi, n, 0)),
    ]
    args = [sink, qkv]
    if window:
        in_specs += [
            pl.BlockSpec((None, ATTN_BLOCK, 2 * KV_W), lambda bi, n: (bi, jnp.maximum(n - 1, 0), kv_col)),
            pl.BlockSpec((None, ATTN_BLOCK, 2 * KV_W), lambda bi, n: (bi, n, kv_col)),
            pl.BlockSpec((None, ATTN_BLOCK, 2 * KV_W), lambda bi, n: (bi, jnp.minimum(n + 1, nb - 1), kv_col)),
        ]
        args += [qkv, qkv, qkv]
    in_specs.append(pl.BlockSpec((None, n_ctx, 2 * KV_W), lambda bi, n: (bi, 0, kv_col)))
    args.append(kv_ctx)
    return pl.pallas_call(
        functools.partial(_attn_kernel, window=window, n_blocks=nb),
        out_shape=jax.ShapeDtypeStruct((b, l, ATTN_W), F32),
        grid=(b, nb),
        in_specs=in_specs,
        out_specs=pl.BlockSpec((None, ATTN_BLOCK, ATTN_W), lambda bi, n: (bi, n, 0)),
        compiler_params=_cparams(("parallel", "parallel")),
        name="attention_window" if window else "attention_context",
    )(*args)


def _gelu(x):
    return jax.nn.gelu(x, approximate=True)


def _sigmoid(x):
    return 1.0 / (1.0 + jnp.exp(-x))


def _mixer_kernel(rest_ref, hprev_ref, hnext_ref, ya_ref, x_ref, gt_ref, poolw_ref, pscale_ref, sguw_ref,
                  sgub_ref, wbr_ref, wout_ref, o_ref, ext_ref, *, tm, tiles_per_seq, seq_len):
    ti = pl.program_id(0) % tiles_per_seq
    z = rest_ref[:, 0:POOL_W]
    ext_ref[0:POOL_HALO, :] = jnp.where(ti > 0, hprev_ref[...], 0.0)
    ext_ref[POOL_HALO:POOL_HALO + tm, :] = z
    ext_ref[POOL_HALO + tm:2 * POOL_HALO + tm, :] = jnp.where(ti < tiles_per_seq - 1, hnext_ref[...], 0.0)
    pos = ti * tm + lax.broadcasted_iota(jnp.int32, (tm, LANES), 0)
    lane = lax.broadcasted_iota(jnp.int32, (tm, LANES), 1)

    def count(size):
        hi = jnp.minimum(pos + size // 2, seq_len)
        lo = jnp.maximum(pos - size // 2, 0)
        return (hi - lo).astype(F32)

    diffs = []
    for lt in range(POOL_W // LANES):
        cols = slice(lt * LANES, (lt + 1) * LANES)

        def shifted(d, cols=cols):
            return ext_ref[POOL_HALO + d:POOL_HALO + d + tm, cols]

        small, large = POOL_SIZES[2 * lt], POOL_SIZES[2 * lt + 1]
        s = shifted(-1) + shifted(0)
        width = 2
        sums = {}
        while width <= large:
            sums[width] = s
            half = width // 2
            if 2 * width <= large:
                for d in range(half, width):
                    s = s + shifted(-d - 1) + shifted(d)
            width *= 2
        mean = jnp.where(lane < POOL_GROUP_W, sums[small] / count(small), sums[large] / count(large))
        diffs.append(mean - z[:, cols])
    d = jnp.concatenate(diffs, axis=-1).astype(BF16)
    y_pool = _dot(d, poolw_ref[...]) * pscale_ref[...]

    u = _gelu(rest_ref[:, POOL_W:POOL_W + SGU_W])
    v = _gelu(rest_ref[:, POOL_W + SGU_W:POOL_W + 2 * SGU_W])
    vn = (v * lax.rsqrt(jnp.mean(v * v, axis=-1, keepdims=True) + EPS)).astype(BF16)
    group = lax.broadcasted_iota(jnp.int32, (SGU_CHUNK, SGU_W), 1) // SGU_GROUP_W
    mixed = []
    for c in range(tm // SGU_CHUNK):
        vc = vn[c * SGU_CHUNK:(c + 1) * SGU_CHUNK, :]
        mc = sgub_ref[...]
        for hg in range(SGU_GROUPS):
            mc = mc + jnp.where(group == hg, _dot(sguw_ref[hg], vc), 0.0)
        mixed.append(mc)
    y_sgu = u * jnp.concatenate(mixed, axis=0)

    g_off = POOL_W + 2 * SGU_W
    merged = _sigmoid(rest_ref[:, g_off:g_off + D_MODEL]) * _dot(ya_ref[...].astype(BF16), wbr_ref[0:ATTN_W, :])
    merged = merged + _sigmoid(rest_ref[:, g_off + D_MODEL:g_off + 2 * D_MODEL]) * _dot(
        y_pool.astype(BF16), wbr_ref[ATTN_W:ATTN_W + POOL_W, :])
    merged = merged + _sigmoid(rest_ref[:, g_off + 2 * D_MODEL:g_off + 3 * D_MODEL]) * _dot(
        y_sgu.astype(BF16), wbr_ref[ATTN_W + POOL_W:BR_W, :])
    o_ref[...] = x_ref[...] + gt_ref[...] * _dot(merged.astype(BF16), wout_ref[...])


def _mixer(rest, y_attn, x2d, gt, poolw_bd, pscale, sguw, sgub_full, wbr, wout, rows_per_mod, seq_len):
    t = x2d.shape[0]
    tm = min(TOKEN_TILE, seq_len)
    tiles_per_seq = seq_len // tm
    tiles_per_mod = rows_per_mod // tm
    halo_blocks = tm // POOL_HALO
    n_halo = t // POOL_HALO
    kern = functools.partial(_mixer_kernel, tm=tm, tiles_per_seq=tiles_per_seq, seq_len=seq_len)
    return pl.pallas_call(
        kern,
        out_shape=jax.ShapeDtypeStruct((t, D_MODEL), F32),
        grid=(t // tm,),
        in_specs=[
            pl.BlockSpec((tm, REST_W), lambda i: (i, 0)),
            pl.BlockSpec((POOL_HALO, POOL_W), lambda i: (jnp.maximum(i * halo_blocks - 1, 0), 0)),
            pl.BlockSpec((POOL_HALO, POOL_W), lambda i: (jnp.minimum((i + 1) * halo_blocks, n_halo - 1), 0)),
            pl.BlockSpec((tm, ATTN_W), lambda i: (i, 0)),
            pl.BlockSpec((tm, D_MODEL), lambda i: (i, 0)),
            pl.BlockSpec((None, 1, D_MODEL), lambda i: (i // tiles_per_mod, 0, 0)),
            pl.BlockSpec((POOL_W, POOL_W), lambda i: (0, 0)),
            pl.BlockSpec((1, POOL_W), lambda i: (0, 0)),
            pl.BlockSpec((SGU_GROUPS, SGU_CHUNK, SGU_CHUNK), lambda i: (0, 0, 0)),
            pl.BlockSpec((SGU_CHUNK, SGU_W), lambda i: (0, 0)),
            pl.BlockSpec((BR_W, D_MODEL), lambda i: (0, 0)),
            pl.BlockSpec((D_MODEL, D_MODEL), lambda i: (0, 0)),
        ],
        out_specs=pl.BlockSpec((tm, D_MODEL), lambda i: (i, 0)),
        scratch_shapes=[pltpu.VMEM((tm + 2 * POOL_HALO, POOL_W), F32)],
        compiler_params=_cparams(("parallel",)),
        name="mixer",
    )(rest, rest, rest, y_attn, x2d, gt, poolw_bd, pscale, sguw, sgub_full, wbr, wout)


def _top16_rows(problems, emit):
    scores = [p[0] for p in problems]
    for r in range(PEER_TOPK):
        for i, (_, ids) in enumerate(problems):
            s = scores[i]
            m = jnp.max(s, axis=0, keepdims=True)
            idx = jnp.min(jnp.where(s == m, ids, _NO_ROW), axis=0, keepdims=True)
            hit = ids == idx
            emit(i, r, m, hit, idx)
            scores[i] = jnp.where(hit, -jnp.inf, s)


def _pair_candidates(sv0, sv1, si0, si1, sub):
    subf = sub.astype(F32)
    sums, experts, flat = [], [], []
    for a in range(SUBLANES):
        n_b = PEER_TOPK // (a + 1)
        for b0 in range(0, n_b, SUBLANES):
            s = sv0[a:a + 1, :] + sv1[b0:b0 + SUBLANES, :]
            if n_b - b0 < SUBLANES:
                s = jnp.where(sub < n_b - b0, s, -jnp.inf)
            sums.append(s)
            experts.append(si0[a:a + 1, :] * PEER_KEYS + si1[b0:b0 + SUBLANES, :])
            flat.append(subf + float(a * PEER_TOPK + b0))
    assert PEER_TOPK // (SUBLANES + 1) == 1
    sums.append(sv0[SUBLANES:, :] + sv1[0:1, :])
    experts.append(si0[SUBLANES:, :] * PEER_KEYS + si1[0:1, :])
    flat.append((subf + float(SUBLANES)) * float(PEER_TOPK))
    return jnp.concatenate(sums, axis=0), jnp.concatenate(experts, axis=0), jnp.concatenate(flat, axis=0)


def _route_sub_keys(q_ref, sk_ref, sv_ref, si_ref, tm):
    key_id = lax.broadcasted_iota(jnp.int32, (PEER_KEYS, tm), 0).astype(F32)

    def sub_keys(it, carry):
        hps = [it * ROUTE_INTERLEAVE + u for u in range(ROUTE_INTERLEAVE)]
        problems = [(_dot_nt(sk_ref[hp], q_ref[hp].astype(BF16)), key_id) for hp in hps]

        def emit(i, r, m, hit, idx):
            sv_ref[hps[i], r:r + 1, :] = m
            si_ref[hps[i], r:r + 1, :] = idx.astype(jnp.int32)

        _top16_rows(problems, emit)
        return carry

    lax.fori_loop(0, 2 * PEER_HEADS // ROUTE_INTERLEAVE, sub_keys, 0)


def _route_pairs(sv_ref, si_ref, e_ref, g_ref, ts_ref, te_ref, tm):
    sub = lax.broadcasted_iota(jnp.int32, (SUBLANES, tm), 0)

    def heads(it, carry):
        hs = [it * PAIR_INTERLEAVE + u for u in range(PAIR_INTERLEAVE)]
        problems = []
        for h in hs:
            cs, _, cid = _pair_candidates(sv_ref[2 * h], sv_ref[2 * h + 1], si_ref[2 * h], si_ref[2 * h + 1], sub)
            problems.append((cs, cid))

        def emit(i, r, m, hit, idx):
            ts_ref[i, r:r + 1, :] = m
            te_ref[i, r:r + 1, :] = idx.astype(jnp.int32)

        _top16_rows(problems, emit)
        for i, h in enumerate(hs):
            row0 = pl.multiple_of(h * PEER_TOPK, PEER_TOPK)
            flat = te_ref[i]
            a, b = flat >> (PEER_TOPK.bit_length() - 1), flat & (PEER_TOPK - 1)
            si0, si1 = si_ref[2 * h], si_ref[2 * h + 1]
            key0 = sum(jnp.where(a == r, si0[r:r + 1, :], 0) for r in range(PEER_TOPK))
            key1 = sum(jnp.where(b == r, si1[r:r + 1, :], 0) for r in range(PEER_TOPK))
            e_ref[pl.ds(row0, PEER_TOPK), :] = key0 * PEER_KEYS + key1
            ts = ts_ref[i]
            ex = jnp.exp(ts - jnp.max(ts, axis=0, keepdims=True))
            g_ref[pl.ds(row0, PEER_TOPK), :] = ex / jnp.sum(ex, axis=0, keepdims=True)
        return carry

    lax.fori_loop(0, PEER_HEADS // PAIR_INTERLEAVE, heads, 0)


def _peer_route_kernel(q_ref, sk_ref, e_ref, g_ref, sv_ref, si_ref, ts_ref, te_ref, *, tm):
    _route_sub_keys(q_ref, sk_ref, sv_ref, si_ref, tm)
    _route_pairs(sv_ref, si_ref, e_ref, g_ref, ts_ref, te_ref, tm)


def _peer_pairs_kernel(sv_ref, si_ref, e_ref, g_ref, ts_ref, te_ref, *, tm):
    _route_pairs(sv_ref, si_ref, e_ref, g_ref, ts_ref, te_ref, tm)


def _route_out(nblk, tm):
    shape = (jax.ShapeDtypeStruct((nblk, PEER_SLOTS, tm), jnp.int32), jax.ShapeDtypeStruct((nblk, PEER_SLOTS, tm), F32))
    specs = (pl.BlockSpec((None, PEER_SLOTS, tm), lambda i: (i, 0, 0)),
             pl.BlockSpec((None, PEER_SLOTS, tm), lambda i: (i, 0, 0)))
    scratch = [pltpu.VMEM((PAIR_INTERLEAVE, PEER_TOPK, tm), F32), pltpu.VMEM((PAIR_INTERLEAVE, PEER_TOPK, tm), jnp.int32)]
    return shape, specs, scratch


def _peer_route(q3, subkeys_bf16, n_tokens):
    tm = ROUTE_TILE
    nblk = n_tokens // tm
    out_shape, out_specs, out_scratch = _route_out(nblk, tm)
    return pl.pallas_call(
        functools.partial(_peer_route_kernel, tm=tm),
        out_shape=out_shape,
        grid=(nblk,),
        in_specs=[
            pl.BlockSpec((2 * PEER_HEADS, tm, PEER_HALF), lambda i: (0, i, 0)),
            pl.BlockSpec((2 * PEER_HEADS, PEER_KEYS, PEER_HALF), lambda i: (0, 0, 0)),
        ],
        out_specs=out_specs,
        scratch_shapes=[
            pltpu.VMEM((2 * PEER_HEADS, PEER_TOPK, tm), F32),
            pltpu.VMEM((2 * PEER_HEADS, PEER_TOPK, tm), jnp.int32),
        ] + out_scratch,
        compiler_params=_cparams(("parallel",)),
        name="peer_route",
    )(q3, subkeys_bf16)


def _peer_pairs(sv, si):
    nblk, _, _, tm = sv.shape
    out_shape, out_specs, out_scratch = _route_out(nblk, tm)
    list_spec = pl.BlockSpec((None, 2 * PEER_HEADS, PEER_TOPK, tm), lambda i: (i, 0, 0, 0))
    return pl.pallas_call(
        functools.partial(_peer_pairs_kernel, tm=tm),
        out_shape=out_shape,
        grid=(nblk,),
        in_specs=[list_spec, list_spec],
        out_specs=out_specs,
        scratch_shapes=out_scratch,
        compiler_params=_cparams(("parallel",)),
        name="peer_pairs",
    )(sv, si)


def _peer_apply_kernel(*refs, tb, side):
    if side:
        (idx_ref, gate_ref, h_ref, x_ref, gt_ref, uv_hbm, q_ref, sk_ref, o_ref, sv_ref, si_ref,
         buf_ref, sem_ref, part_ref, acc_ref, s_ref) = refs
    else:
        idx_ref, gate_ref, h_ref, x_ref, gt_ref, uv_hbm, o_ref, buf_ref, sem_ref, part_ref, acc_ref = refs
    lookahead = GATHER_SLOTS - 1
    n_groups = tb // TOKEN_GROUP
    n_tiles = D_MODEL // LANES
    n_slot_groups = PEER_SLOTS // SUBLANES

    def start_rows(tok, slot, k0, n):
        for k in range(k0, k0 + n):
            pltpu.make_async_copy(uv_hbm.at[idx_ref[tok, k]], buf_ref.at[slot, k],
                                  sem_ref.at[slot]).start(priority=k % 2)

    def wait_gather(slot):
        pltpu.make_async_copy(uv_hbm.at[pl.ds(0, PEER_SLOTS)], buf_ref.at[slot], sem_ref.at[slot]).wait()

    for tok in range(lookahead):
        start_rows(tok, tok % GATHER_SLOTS, 0, PEER_SLOTS)

    tok_lane = lax.broadcasted_iota(jnp.int32, (PEER_SLOTS, ROUTE_TILE), 1)
    sub = lax.broadcasted_iota(jnp.int32, (SUBLANES, LANES), 0)

    def fold(x0, x1, d):
        t0 = x0 + pltpu.roll(x0, SUBLANES - d, 0)
        t1 = x1 + pltpu.roll(x1, d, 0)
        return jnp.where((sub & d) == 0, t0, t1)

    key_id = lax.broadcasted_iota(jnp.int32, (PEER_KEYS, ROUTE_TILE), 0).astype(F32)
    groups_per_list_pair = PEER_TOPK // TOKEN_GROUP
    groups_per_tile = groups_per_list_pair * PEER_HEADS
    if side:
        assert n_groups * TOKEN_GROUP == tb and tb % ROUTE_TILE == 0 and groups_per_tile * TOKEN_GROUP == ROUTE_TILE

    def group(gi, is_last):
        base = gi * TOKEN_GROUP if is_last else pl.multiple_of(gi * TOKEN_GROUP, TOKEN_GROUP)
        h8 = h_ref[pl.ds(base, TOKEN_GROUP), :]
        rows = []
        if side:
            tile = gi // groups_per_tile
            hp0 = 2 * ((gi // groups_per_list_pair) % PEER_HEADS)
            r0 = (gi % groups_per_list_pair) * TOKEN_GROUP

            def load_scores():
                rows0 = tile * ROUTE_TILE if is_last else pl.multiple_of(tile * ROUTE_TILE, ROUTE_TILE)
                for pi in range(2):
                    s_ref[pi] = _dot_nt(sk_ref[hp0 + pi], q_ref[hp0 + pi, pl.ds(rows0, ROUTE_TILE), :].astype(BF16))

            if is_last:
                if r0 == 0:
                    load_scores()
            else:
                pl.when(r0 == 0)(load_scores)
        for j in range(TOKEN_GROUP):
            tok = base + j
            slot = j % GATHER_SLOTS
            prefetch = not is_last or j + lookahead < TOKEN_GROUP

            def start_ahead(k0, n, tok=tok, j=j, prefetch=prefetch):
                if prefetch:
                    start_rows(tok + lookahead, (j + lookahead) % GATHER_SLOTS, k0, n)

            wait_gather(slot)
            h_tile = jnp.concatenate([h8[j:j + 1, s * LANES:(s + 1) * LANES] for s in range(n_tiles)], axis=0)
            for g in range(n_slot_groups):
                p = [lax.bitcast_convert_type(buf_ref[slot, g * SUBLANES + c] & jnp.uint32(0xFFFF0000), F32) * h_tile
                     for c in range(SUBLANES)]
                y = [fold(p[c], p[c + 4], 4) for c in range(4)]
                z = [fold(y[c], y[c + 2], 2) for c in range(2)]
                part_ref[g * SUBLANES:(g + 1) * SUBLANES, :] = fold(z[0], z[1], 1)
                start_ahead(g * ISSUE_PER_GROUP, ISSUE_PER_GROUP)
            acc_ref[...] = jnp.zeros(acc_ref.shape, F32)
            start_ahead(n_slot_groups * ISSUE_PER_GROUP, PEER_SLOTS - 2 * n_slot_groups * ISSUE_PER_GROUP)
            if side:
                for pi in range(2):
                    sc = s_ref[pi]
                    m = jnp.max(sc, axis=0, keepdims=True)
                    idx = jnp.min(jnp.where(sc == m, key_id, _NO_ROW), axis=0, keepdims=True)
                    sv_ref[tile, hp0 + pi, r0 + j] = m
                    si_ref[tile, hp0 + pi, r0 + j] = idx.astype(jnp.int32)
                    s_ref[pi] = jnp.where(key_id == idx, -jnp.inf, sc)
            act = jnp.sum(part_ref[...], axis=1, keepdims=True)
            gates = gate_ref[tok // ROUTE_TILE]
            gate = jnp.sum(jnp.where(tok_lane == tok % ROUTE_TILE, gates, 0.0), axis=1, keepdims=True)
            w = gate * _gelu(act)
            for g in range(n_slot_groups):
                terms = [jnp.broadcast_to(w[k:k + 1, :], (SUBLANES, LANES))
                         * lax.bitcast_convert_type(buf_ref[slot, k] << 16, F32)
                         for k in range(g * SUBLANES, (g + 1) * SUBLANES)]
                s4 = [terms[i] + terms[i + 4] for i in range(4)]
                acc_ref[g % 2] = acc_ref[g % 2] + ((s4[0] + s4[1]) + (s4[2] + s4[3]))
                start_ahead(PEER_SLOTS - (n_slot_groups - g) * ISSUE_PER_GROUP, ISSUE_PER_GROUP)
            acc = acc_ref[0] + acc_ref[1]
            rows.append(jnp.concatenate([acc[s:s + 1, :] for s in range(n_tiles)], axis=1))
        out8 = jnp.concatenate(rows, axis=0)
        o_ref[pl.ds(base, TOKEN_GROUP), :] = x_ref[pl.ds(base, TOKEN_GROUP), :] + gt_ref[...] * out8

    def body(gi, carry):
        group(gi, False)
        return carry

    lax.fori_loop(0, n_groups - 1, body, 0)
    group(n_groups - 1, True)


def _peer_apply(experts_tok, gates, h2d, x2d, gt, uv_table, n_tokens, row0, rows_per_mod, side=None):
    tb = min(APPLY_TILE, n_tokens)
    tiles_per_mod = rows_per_mod // tb
    tile0 = row0 // tb
    in_specs = [
        pl.BlockSpec((tb, PEER_SLOTS), lambda i: (i, 0), memory_space=pltpu.SMEM),
        pl.BlockSpec((tb // ROUTE_TILE, PEER_SLOTS, ROUTE_TILE), lambda i: (i, 0, 0)),
        pl.BlockSpec((tb, D_MODEL), lambda i: (tile0 + i, 0)),
        pl.BlockSpec((tb, D_MODEL), lambda i: (tile0 + i, 0)),
        pl.BlockSpec((None, 1, D_MODEL), lambda i: ((tile0 + i) // tiles_per_mod, 0, 0)),
        pl.BlockSpec(memory_space=pl.ANY),
    ]
    args = [experts_tok, gates, h2d, x2d, gt, uv_table]
    out_shape = [jax.ShapeDtypeStruct(x2d.shape, F32)]
    out_specs = [pl.BlockSpec((tb, D_MODEL), lambda i: (tile0 + i, 0))]
    scratch = [
        pltpu.VMEM((GATHER_SLOTS, PEER_SLOTS, SUBLANES, LANES), jnp.uint32),
        pltpu.SemaphoreType.DMA((GATHER_SLOTS,)),
        pltpu.VMEM((PEER_SLOTS, LANES), F32),
        pltpu.VMEM((2, SUBLANES, LANES), F32),
    ]
    if side is not None:
        q3, subkeys, side_row0 = side
        side_tile0 = side_row0 // tb
        in_specs += [
            pl.BlockSpec((2 * PEER_HEADS, tb, PEER_HALF), lambda i: (0, side_tile0 + i, 0)),
            pl.BlockSpec((2 * PEER_HEADS, PEER_KEYS, PEER_HALF), lambda i: (0, 0, 0)),
        ]
        args += [q3, subkeys]
        list_shape = (n_tokens // ROUTE_TILE, 2 * PEER_HEADS, PEER_TOPK, 1, ROUTE_TILE)
        list_block = (tb // ROUTE_TILE, 2 * PEER_HEADS, PEER_TOPK, 1, ROUTE_TILE)
        out_shape += [jax.ShapeDtypeStruct(list_shape, F32), jax.ShapeDtypeStruct(list_shape, jnp.int32)]
        out_specs += [pl.BlockSpec(list_block, lambda i: (i, 0, 0, 0, 0))] * 2
        scratch.append(pltpu.VMEM((2, PEER_KEYS, ROUTE_TILE), F32))
    outs = pl.pallas_call(
        functools.partial(_peer_apply_kernel, tb=tb, side=side is not None),
        out_shape=tuple(out_shape),
        grid=(n_tokens // tb,),
        in_specs=in_specs,
        out_specs=tuple(out_specs),
        scratch_shapes=scratch,
        input_output_aliases={3: 0},
        compiler_params=_cparams(("arbitrary",), disable_bounds_checks=True),
        name="peer_apply_side" if side is not None else "peer_apply",
    )(*args)
    if side is None:
        return outs[0]
    sv, si = (a.reshape(a.shape[:3] + a.shape[4:]) for a in outs[1:])
    return outs[0], sv, si


def _final_norm_kernel(x_ref, g_ref, o_ref):
    x = x_ref[...]
    o_ref[...] = (x * lax.rsqrt(jnp.mean(x * x, axis=-1, keepdims=True) + EPS)) * g_ref[...]


def _final_norm(x2d, g):
    t = x2d.shape[0]
    tm = min(2 * TOKEN_TILE, t)
    return pl.pallas_call(
        _final_norm_kernel,
        out_shape=jax.ShapeDtypeStruct((t, D_MODEL), F32),
        grid=(t // tm,),
        in_specs=[pl.BlockSpec((tm, D_MODEL), lambda i: (i, 0)), pl.BlockSpec((1, D_MODEL), lambda i: (0, 0))],
        out_specs=pl.BlockSpec((tm, D_MODEL), lambda i: (i, 0)),
        compiler_params=_cparams(("parallel",)),
        name="final_norm",
    )(x2d, g)


def _rope_tables(length):
    rows = length // GRID_W
    row = jnp.repeat(jnp.arange(rows, dtype=F32), GRID_W)
    col = jnp.tile(jnp.arange(GRID_W, dtype=F32), rows)
    inv = ROPE_BASE ** (-jnp.arange(ROPE_FREQS, dtype=F32) / ROPE_FREQS)
    ang = jnp.stack([row[:, None] * inv, col[:, None] * inv], axis=1)
    cos, sin = jnp.cos(ang), jnp.sin(ang)
    cos_h = jnp.broadcast_to(cos[:, :, None, :], (length, 2, 2, ROPE_FREQS)).reshape(length, HEAD_DIM)
    sin_h = jnp.stack([-sin, sin], axis=2).reshape(length, HEAD_DIM)
    reps = LANES // HEAD_DIM
    return jnp.tile(cos_h, (1, reps)), jnp.tile(sin_h, (1, reps))


def _pack_expert_tables(u_tab, v_tab):
    def bits(a):
        return lax.bitcast_convert_type(a.astype(BF16), jnp.uint16).astype(jnp.uint32)

    return ((bits(u_tab) << 16) | bits(v_tab)).reshape(-1, D_MODEL // LANES, LANES)


def _experts_by_token(experts):
    return jnp.transpose(experts, (0, 2, 1)).reshape(-1, PEER_SLOTS)


def _peer_layer(x2d, g, sc, sh, gt, rows_per_mod, wq, subkeys, uv_table, n_chunks):
    t = x2d.shape[0]
    q3, h2d = _peer_query(x2d, g, sc, sh, wq, rows_per_mod)
    n = t // n_chunks
    experts, gates = _peer_route(q3, subkeys, n)
    for c in range(n_chunks):
        side = (q3, subkeys, (c + 1) * n) if c + 1 < n_chunks else None
        res = _peer_apply(_experts_by_token(experts), gates, h2d, x2d, gt, uv_table, n, c * n, rows_per_mod, side)
        if side is None:
            x2d = res
        else:
            x2d = res[0]
            experts, gates = _peer_pairs(res[1], res[2])
    return x2d


def kernel(x, c, ctx, c_ctx, w_mod, b_mod, g_mix, g_ffn, w_in, attn_sink, pool_w, pool_scale, sgu_w, sgu_b,
           w_br_attn, w_br_pool, w_br_sgu, w_out, peer_wq, peer_subkeys, peer_u, peer_v, g_final):
    b, l, d = x.shape
    n_ctx = ctx.shape[1]
    depth = w_mod.shape[0]
    assert d == D_MODEL and l % TOKEN_TILE == 0 and n_ctx % SGU_CHUNK == 0

    n_rows = -(-(b + 1) // SUBLANES) * SUBLANES
    cond = jnp.zeros((n_rows, d), F32).at[:b].set(c).at[b].set(c_ctx)
    mod = _adaln(cond, w_mod, b_mod).reshape(depth, n_rows, N_MOD, 1, d)

    rope = _rope_tables(l)
    x2d = x.reshape(b * l, d)
    xc2d = ctx.reshape(b * n_ctx, d)

    for layer in range(depth):
        last = layer == depth - 1
        m_lat = [mod[layer, :b, i] for i in range(N_MOD)]
        m_ctx = [mod[layer, b:b + 1, i] for i in range(N_MOD)]
        g1 = g_mix[layer].reshape(1, d)
        g2 = g_ffn[layer].reshape(1, d)

        w_in_b = w_in[layer].astype(BF16)
        qkv, rest = _in_proj(x2d, g1, m_lat[1], m_lat[0], w_in_b, l, rope, l)
        qkv_c, rest_c = _in_proj(xc2d, g1, m_ctx[1], m_ctx[0], w_in_b, b * n_ctx, None, n_ctx)

        y_attn = _attention(qkv.reshape(b, l, QKV_W), qkv_c.reshape(b, n_ctx, QKV_W), attn_sink[layer], True)

        poolw_bd = jnp.zeros((POOL_W, POOL_W), F32)
        for gi in range(len(POOL_SIZES)):
            sl = slice(gi * POOL_GROUP_W, (gi + 1) * POOL_GROUP_W)
            poolw_bd = poolw_bd.at[sl, sl].set(pool_w[layer, gi])
        poolw_bd = poolw_bd.astype(BF16)
        pscale = pool_scale[layer].reshape(1, POOL_W)
        sguw = sgu_w[layer].astype(BF16)
        sgub_full = jnp.repeat(sgu_b[layer].T, SGU_GROUP_W, axis=1)
        wbr = jnp.concatenate([w_br_attn[layer], w_br_pool[layer], w_br_sgu[layer]], axis=0).astype(BF16)
        wout = w_out[layer].astype(BF16)
        mix_w = (poolw_bd, pscale, sguw, sgub_full, wbr, wout)

        x2d = _mixer(rest, y_attn.reshape(b * l, ATTN_W), x2d, m_lat[2], *mix_w, l, l)
        if not last:
            y_attn_c = _attention(qkv_c.reshape(b, n_ctx, QKV_W), qkv_c.reshape(b, n_ctx, QKV_W),
                                  attn_sink[layer], False)
            xc2d = _mixer(rest_c, y_attn_c.reshape(b * n_ctx, ATTN_W), xc2d, m_ctx[2], *mix_w,
                          b * n_ctx, n_ctx)

        wq = peer_wq[layer].astype(BF16)
        subkeys = peer_subkeys[layer].reshape(2 * PEER_HEADS, PEER_KEYS, PEER_HALF).astype(BF16)
        uv_table = _pack_expert_tables(peer_u[layer], peer_v[layer])
        x2d = _peer_layer(x2d, g2, m_lat[4], m_lat[3], m_lat[5], l, wq, subkeys, uv_table, b)
        if not last:
            xc2d = _peer_layer(xc2d, g2, m_ctx[4], m_ctx[3], m_ctx[5], b * n_ctx, wq, subkeys, uv_table, 1)

    return _final_norm(x2d, g_final.reshape(1, d)).reshape(b, l, d)
```
